```python
import math, functools
import jax, jax.numpy as jnp
from jax import lax
import numpy as np

D_MODEL = 1024
BATCH = 1
SEQ = 16384
DEPTH = 1
DEC_BATCH = 32
DEC_SEQ = 4
PAST_LEN = 16384
PAGE_SIZE = 128

HEAD_DIM = 64
N_HEADS = D_MODEL // HEAD_DIM
N_ATT_HEADS = N_HEADS // 2
N_SGU_GROUPS = N_HEADS - N_ATT_HEADS
ATT_WIDTH = N_ATT_HEADS * HEAD_DIM
SGU_WIDTH = N_SGU_GROUPS * HEAD_DIM
IN_WIDTH = 3 * ATT_WIDTH + 2 * SGU_WIDTH
CHUNK = 128
MOBA_BLOCK = 256
MOBA_TOPK = 3
Q_CHUNK = 64
N_EXPERTS = 32
TOP_K = 4
D_FF = D_MODEL
SWIGLU_LIMIT = 7.0
SWIGLU_ALPHA = 1.702
MOE_BLOCK = 128
EPS = 1e-6
MASK_VALUE = -1e30

kernel_name = "hymba_moba_sgu_moe_decode_step"


def rms_norm(x, g):
    xf = x.astype(jnp.float32)
    xf = xf * lax.rsqrt(jnp.mean(xf * xf, axis=-1, keepdims=True) + EPS)
    return (xf * g.astype(jnp.float32)).astype(x.dtype)


def alibi_slopes():
    return jnp.exp2(-8.0 * jnp.arange(1, N_ATT_HEADS + 1, dtype=jnp.float32) / N_ATT_HEADS)


def moba_attend(q, pos, kb, vb, kmean, slopes):
    B, Q, H, _ = q.shape
    NB = kb.shape[1]
    n_sel = min(MOBA_TOPK, NB)
    own = pos // MOBA_BLOCK
    gate = jnp.einsum('bqhd,bnhd->bqhn', q.astype(jnp.float32), kmean)
    past = jnp.arange(NB)[None, :] < own[:, None]
    gate = jnp.where(past[None, :, None, :], gate, -jnp.inf)
    _, top = lax.top_k(gate, n_sel)
    own_b = jnp.broadcast_to(own[None, :, None, None], (B, Q, H, 1))
    sel = jnp.concatenate([top.astype(jnp.int32), own_b.astype(jnp.int32)], axis=-1)
    slot_ok = jnp.concatenate([jnp.arange(n_sel)[None, :] < own[:, None],
                               jnp.ones((Q, 1), dtype=bool)], axis=-1)
    bi = jnp.arange(B)[:, None, None, None]
    hi = jnp.arange(H)[None, None, :, None]
    kg = kb[bi, sel, :, hi]
    vg = vb[bi, sel, :, hi]
    kpos = sel[..., None] * MOBA_BLOCK + jnp.arange(MOBA_BLOCK)
    dist = pos[None, :, None, None, None] - kpos
    s = jnp.einsum('bqhd,bqhjkd->bqhjk', q, kg).astype(jnp.float32) * (HEAD_DIM ** -0.5)
    s = s - slopes[None, None, :, None, None] * dist.astype(jnp.float32)
    s = jnp.where(slot_ok[None, :, None, :, None] & (dist >= 0), s, MASK_VALUE)
    p = jax.nn.softmax(s.reshape(B, Q, H, -1), axis=-1).reshape(s.shape)
    return jnp.einsum('bqhjk,bqhjkd->bqhd', p.astype(vg.dtype), vg)


def to_blocks(k_full, total_len):
    B = k_full.shape[0]
    lp = -(-total_len // MOBA_BLOCK) * MOBA_BLOCK
    pad = jnp.zeros((B, lp - total_len, N_ATT_HEADS, HEAD_DIM), k_full.dtype)
    return jnp.concatenate([k_full, pad], axis=1).reshape(B, lp // MOBA_BLOCK, MOBA_BLOCK, N_ATT_HEADS, HEAD_DIM)


def moba_prompt(q, k, v, slopes):
    B, T = q.shape[:2]
    kb = to_blocks(k, T)
    vb = to_blocks(v, T)
    kmean = jnp.mean(kb.astype(jnp.float32), axis=2)
    n_c = T // Q_CHUNK
    qs = q.reshape(B, n_c, Q_CHUNK, N_ATT_HEADS, HEAD_DIM).transpose(1, 0, 2, 3, 4)
    ps = jnp.arange(T, dtype=jnp.int32).reshape(n_c, Q_CHUNK)
    out = lax.map(lambda a: moba_attend(a[0], a[1], kb, vb, kmean, slopes), (qs, ps))
    return out.transpose(1, 0, 2, 3, 4).reshape(B, T, N_ATT_HEADS, HEAD_DIM)


def moba_sample(q, k_new, v_new, cache_k_l, cache_v_l, page_table, slopes):
    DB, S = q.shape[:2]
    past_len = page_table.shape[1] * PAGE_SIZE
    k_past = cache_k_l[page_table].reshape(DB, past_len, N_ATT_HEADS, HEAD_DIM)
    v_past = cache_v_l[page_table].reshape(DB, past_len, N_ATT_HEADS, HEAD_DIM)
    total = past_len + S
    kb = to_blocks(jnp.concatenate([k_past, k_new.astype(k_past.dtype)], axis=1), total)
    vb = to_blocks(jnp.concatenate([v_past, v_new.astype(v_past.dtype)], axis=1), total)
    kmean = jnp.mean(kb.astype(jnp.float32), axis=2)
    pos = past_len + jnp.arange(S, dtype=jnp.int32)
    return moba_attend(q, pos, kb, vb, kmean, slopes)


def spatial_gate(u, v, w_s, b_s):
    B, T, G, HD = v.shape
    tp = -(-T // CHUNK) * CHUNK
    vp = jnp.concatenate([v, jnp.zeros((B, tp - T, G, HD), v.dtype)], axis=1)
    vc = vp.reshape(B, tp // CHUNK, CHUNK, G, HD)
    w = w_s * jnp.tril(jnp.ones((CHUNK, CHUNK), w_s.dtype))
    mixed = jnp.einsum('gts,bnsgd->bntgd', w, vc) + b_s.T[None, None, :, :, None]
    return u * mixed.reshape(B, tp, G, HD)[:, :T]


def moe_ffn(h, w_router, b_router, w_gu, b_gu, w_down, b_down):
    lead = h.shape[:-1]
    x = h.reshape(-1, D_MODEL)
    n = x.shape[0]
    logits = (x @ w_router).astype(jnp.float32) + b_router.astype(jnp.float32)
    top_val, top_idx = lax.top_k(logits, TOP_K)
    gate = jax.nn.softmax(top_val, axis=-1)
    n_assign = n * TOP_K
    e_a = top_idx.reshape(-1).astype(jnp.int32)
    t_a = jnp.repeat(jnp.arange(n, dtype=jnp.int32), TOP_K)
    w_a = gate.reshape(-1)
    order = jnp.argsort(e_a)
    e_s, t_s, w_s = e_a[order], t_a[order], w_a[order]
    cnt = jnp.bincount(e_a, length=N_EXPERTS).astype(jnp.int32)
    start = jnp.cumsum(cnt) - cnt
    pcnt = (cnt + MOE_BLOCK - 1) // MOE_BLOCK * MOE_BLOCK
    pend = jnp.cumsum(pcnt)
    pstart = pend - pcnt
    dest = pstart[e_s] + (jnp.arange(n_assign, dtype=jnp.int32) - start[e_s])
    n_blk = -(-n_assign // MOE_BLOCK) + N_EXPERTS
    n_slot = n_blk * MOE_BLOCK
    slot_tok = jnp.full((n_slot,), n, jnp.int32).at[dest].set(t_s)
    slot_w = jnp.zeros((n_slot,), jnp.float32).at[dest].set(w_s)
    blk_e = jnp.clip(jnp.searchsorted(pend, jnp.arange(n_blk, dtype=jnp.int32) * MOE_BLOCK, side='right'),
                     0, N_EXPERTS - 1)
    x_pad = jnp.concatenate([x, jnp.zeros((1, D_MODEL), x.dtype)], axis=0)

    def expert_block(args):
        tok, e = args
        gu = x_pad[tok] @ w_gu[e] + b_gu[e]
        g = jnp.minimum(gu[:, :D_FF], SWIGLU_LIMIT)
        lin = jnp.clip(gu[:, D_FF:], -SWIGLU_LIMIT, SWIGLU_LIMIT)
        act = g * jax.nn.sigmoid(SWIGLU_ALPHA * g) * (lin + 1.0)
        return act @ w_down[e] + b_down[e]

    out = lax.map(expert_block, (slot_tok.reshape(n_blk, MOE_BLOCK), blk_e)).reshape(n_slot, D_MODEL)
    y = jax.ops.segment_sum(out * slot_w[:, None].astype(out.dtype), slot_tok, num_segments=n + 1)[:n]
    return y.reshape(lead + (D_MODEL,)).astype(h.dtype)


def trunk(x, c, attend, g_mix, w_ada, b_ada, w_in, g_sgu, w_sgu, b_sgu, g_out_att, g_out_sgu, w_out,
          g_ffn, w_router, b_router, w_gu, b_gu, w_down, b_down, g_final):
    B, T, _ = x.shape
    k_rows, v_rows, sgu_rows = [], [], []
    for l in range(DEPTH):
        mod = jax.nn.silu(c) @ w_ada[l] + b_ada[l]
        sh1, sc1, gt1, sh2, sc2, gt2 = jnp.split(mod[:, None, :], 6, axis=-1)
        h = rms_norm(x, g_mix[l]) * (1.0 + sc1) + sh1
        z = h @ w_in[l]
        q, k, v, u, vs = jnp.split(z, [ATT_WIDTH, 2 * ATT_WIDTH, 3 * ATT_WIDTH, 3 * ATT_WIDTH + SGU_WIDTH], axis=-1)
        q = q.reshape(B, T, N_ATT_HEADS, HEAD_DIM)
        k = k.reshape(B, T, N_ATT_HEADS, HEAD_DIM)
        v = v.reshape(B, T, N_ATT_HEADS, HEAD_DIM)
        attn = attend(l, q, k, v)
        u = jax.nn.gelu(u.reshape(B, T, N_SGU_GROUPS, HEAD_DIM))
        vs = rms_norm(jax.nn.gelu(vs.reshape(B, T, N_SGU_GROUPS, HEAD_DIM)), g_sgu[l])
        sgu = spatial_gate(u, vs, w_sgu[l], b_sgu[l])
        merged = jnp.concatenate([rms_norm(attn.reshape(B, T, ATT_WIDTH), g_out_att[l]),
                                  rms_norm(sgu.reshape(B, T, SGU_WIDTH), g_out_sgu[l])], axis=-1) @ w_out[l]
        x = x + gt1 * merged
        h2 = rms_norm(x, g_ffn[l]) * (1.0 + sc2) + sh2
        x = x + gt2 * moe_ffn(h2, w_router[l], b_router[l], w_gu[l], b_gu[l], w_down[l], b_down[l])
        k_rows.append(k)
        v_rows.append(v)
        sgu_rows.append(vs)
    return rms_norm(x, g_final), jnp.stack(k_rows), jnp.stack(v_rows), jnp.stack(sgu_rows)


def setup_inputs(seed: int = 0) -> dict:
    key = jax.random.key(seed)
    ks = jax.random.split(key, 32)
    f32 = jnp.float32

    def nrm(k, shape, scale):
        return jax.random.normal(k, shape, f32) * scale

    n_pages = PAST_LEN // PAGE_SIZE
    n_used = DEC_BATCH * n_pages
    n_pool = n_used + max(1, n_used // 4)
    L = DEPTH
    return {
        "x_prompt": nrm(ks[0], (BATCH, SEQ, D_MODEL), 1.0),
        "x_sample": nrm(ks[1], (DEC_BATCH, DEC_SEQ, D_MODEL), 1.0),
        "cache_k": nrm(ks[2], (L, n_pool, PAGE_SIZE, N_ATT_HEADS, HEAD_DIM), 1.0),
        "cache_v": nrm(ks[3], (L, n_pool, PAGE_SIZE, N_ATT_HEADS, HEAD_DIM), 1.0),
        "page_table": jax.random.permutation(ks[4], n_pool)[:n_used].reshape(DEC_BATCH, n_pages).astype(jnp.int32),
        "c_prompt": nrm(ks[5], (BATCH, D_MODEL), 1.0),
        "c_sample": nrm(ks[6], (DEC_BATCH, D_MODEL), 1.0),
        "g_mix": 1.0 + nrm(ks[7], (L, D_MODEL), 0.05),
        "w_ada": nrm(ks[8], (L, D_MODEL, 6 * D_MODEL), 0.5 * D_MODEL ** -0.5),
        "b_ada": nrm(ks[9], (L, 6 * D_MODEL), 0.02),
        "w_in": nrm(ks[10], (L, D_MODEL, IN_WIDTH), D_MODEL ** -0.5),
        "g_sgu": 1.0 + nrm(ks[11], (L, N_SGU_GROUPS, HEAD_DIM), 0.05),
        "w_sgu": nrm(ks[12], (L, N_SGU_GROUPS, CHUNK, CHUNK), 0.1),
        "b_sgu": 1.0 + nrm(ks[13], (L, N_SGU_GROUPS, CHUNK), 0.1),
        "g_out_att": 1.0 + nrm(ks[14], (L, ATT_WIDTH), 0.05),
        "g_out_sgu": 1.0 + nrm(ks[15], (L, SGU_WIDTH), 0.05),
        "w_out": nrm(ks[16], (L, ATT_WIDTH + SGU_WIDTH, D_MODEL), (ATT_WIDTH + SGU_WIDTH) ** -0.5),
        "g_ffn": 1.0 + nrm(ks[17], (L, D_MODEL), 0.05),
        "w_router": nrm(ks[18], (L, D_MODEL, N_EXPERTS), D_MODEL ** -0.5),
        "b_router": nrm(ks[19], (L, N_EXPERTS), 0.01),
        "w_gu": nrm(ks[20], (L, N_EXPERTS, D_MODEL, 2 * D_FF), D_MODEL ** -0.5),
        "b_gu": nrm(ks[21], (L, N_EXPERTS, 2 * D_FF), 0.02),
        "w_down": nrm(ks[22], (L, N_EXPERTS, D_FF, D_MODEL), D_FF ** -0.5),
        "b_down": nrm(ks[23], (L, N_EXPERTS, D_MODEL), 0.02),
        "g_final": 1.0 + nrm(ks[24], (D_MODEL,), 0.05),
    }


def reference(x_prompt, x_sample, cache_k, cache_v, page_table, c_prompt, c_sample, g_mix, w_ada, b_ada,
              w_in, g_sgu, w_sgu, b_sgu, g_out_att, g_out_sgu, w_out, g_ffn, w_router, b_router, w_gu, b_gu,
              w_down, b_down, g_final):
    slopes = alibi_slopes()
    weights = (g_mix, w_ada, b_ada, w_in, g_sgu, w_sgu, b_sgu, g_out_att, g_out_sgu, w_out, g_ffn,
               w_router, b_router, w_gu, b_gu, w_down, b_down, g_final)

    def attend_prompt(l, q, k, v):
        return moba_prompt(q, k, v, slopes)

    def attend_sample(l, q, k, v):
        return moba_sample(q, k, v, cache_k[l], cache_v[l], page_table, slopes)

    y_prompt, k_p, v_p, _ = trunk(x_prompt, c_prompt, attend_prompt, *weights)
    y_sample, k_s, v_s, sgu_v_s = trunk(x_sample, c_sample, attend_sample, *weights)
    k_prompt_pages = k_p.reshape(DEPTH, BATCH, SEQ // PAGE_SIZE, PAGE_SIZE, N_ATT_HEADS, HEAD_DIM)
    v_prompt_pages = v_p.reshape(DEPTH, BATCH, SEQ // PAGE_SIZE, PAGE_SIZE, N_ATT_HEADS, HEAD_DIM)
    return (y_prompt, y_sample, k_prompt_pages, v_prompt_pages, k_s, v_s, sgu_v_s)
```

```python
import functools

import numpy as np
import jax
import jax.numpy as jnp
from jax import lax
from jax.experimental import pallas as pl
from jax.experimental.pallas import tpu as pltpu

F32 = jnp.float32
BF16 = jnp.bfloat16

D_MODEL = 1024
HEAD_DIM = 64
N_ATT_HEADS = 8
N_SGU_GROUPS = 8
ATT_WIDTH = N_ATT_HEADS * HEAD_DIM
SGU_WIDTH = N_SGU_GROUPS * HEAD_DIM
IN_WIDTH = 3 * ATT_WIDTH + 2 * SGU_WIDTH
CHUNK = 128
MOBA_BLOCK = 256
MOBA_TOPK = 3
PAGE_SIZE = 128
N_EXPERTS = 32
TOP_K = 4
D_FF = D_MODEL
SWIGLU_LIMIT = 7.0
SWIGLU_ALPHA = 1.702
EPS = 1e-6
MASK_VALUE = -1e30

HEADS_PER_GROUP = 4
GROUP_WIDTH = HEADS_PER_GROUP * HEAD_DIM
ROUTE_LANES = 128
MOE_TILE = 256
DISPATCH_TOKENS = 128
COMBINE_TOKENS = 128
OWN_ROWS = 16
VMEM_LIMIT = 56 * 1024 * 1024

_SLOPES = [2.0 ** (-8.0 * (h + 1) / N_ATT_HEADS) for h in range(N_ATT_HEADS)]


def _params(*sem):
    return pltpu.CompilerParams(dimension_semantics=sem, vmem_limit_bytes=VMEM_LIMIT)


def _split(a):
    hi = a.astype(BF16)
    lo = (a - hi.astype(F32)).astype(BF16)
    return hi, lo


def _dot(a, b):
    return jnp.dot(a, b, preferred_element_type=F32)


def _dot_nt(a, b):
    return lax.dot_general(a, b, (((1,), (1,)), ((), ())), preferred_element_type=F32)


def _dot3(a, b, dot=_dot):
    a_hi, a_lo = _split(a)
    b_hi, b_lo = _split(b)
    return dot(a_hi, b_hi) + (dot(a_hi, b_lo) + dot(a_lo, b_hi))


def _rms(x, g):
    return x * lax.rsqrt(jnp.mean(x * x, axis=-1, keepdims=True) + EPS) * g


def _gelu(x):
    return x * (0.5 * (1.0 + jnp.tanh(0.7978845608028654 * (x + 0.044715 * (x * x * x)))))


def _ada_body(c_ref, w_ref, b_ref, o_ref):
    c = c_ref[...]
    s = c * (1.0 / (1.0 + jnp.exp(-c)))
    o_ref[...] = _dot3(s, w_ref[...]) + b_ref[...]


def _ada(c, w, b):
    rows = c.shape[0]
    n = w.shape[1]
    tn = 1536
    return pl.pallas_call(
        _ada_body,
        grid=(n // tn,),
        in_specs=[pl.BlockSpec((rows, D_MODEL), lambda j: (0, 0)),
                  pl.BlockSpec((D_MODEL, tn), lambda j: (0, j)),
                  pl.BlockSpec((1, tn), lambda j: (0, j))],
        out_specs=pl.BlockSpec((rows, tn), lambda j: (0, j)),
        out_shape=jax.ShapeDtypeStruct((rows, n), F32),
        compiler_params=_params("arbitrary"),
        name="ada",
    )(c, w, b.reshape(1, n))


def _inproj_body(x_ref, g_ref, sc_ref, sh_ref, w_ref, gs_ref, bd_ref, q_ref, k_ref, v_ref, ug_ref, vsn_ref,
                 *prompt_refs):
    h = _rms(x_ref[...], g_ref[...]) * (1.0 + sc_ref[...]) + sh_ref[...]
    z = _dot(h.astype(BF16), w_ref[...])
    a = ATT_WIDTH
    q_ref[...] = z[:, :a]
    k = z[:, a:2 * a]
    v = z[:, 2 * a:3 * a]
    k_ref[...] = k
    v_ref[...] = v
    ug_ref[...] = _gelu(z[:, 3 * a:3 * a + SGU_WIDTH])
    vg = _gelu(z[:, 3 * a + SGU_WIDTH:])
    sq_hi, sq_lo = _split(vg * vg)
    gmean = _dot(sq_hi, bd_ref[...]) + _dot(sq_lo, bd_ref[...])
    vsn_ref[...] = vg * lax.rsqrt(gmean + EPS) * gs_ref[...]
    if prompt_refs:
        kb_ref, vt_ref, km_ref = prompt_refs
        kb_ref[...] = k.astype(BF16)
        vt_ref[...] = v.T.astype(BF16)
        km_ref[0] = jnp.mean(k, axis=0, keepdims=True)


def _inproj(x, g_mix, sc, sh, w_in_bf, g_sgu, bd, prompt):
    n = x.shape[0]
    tm = MOBA_BLOCK if prompt else n
    mod_rows = sc.shape[0]
    mod_spec = (pl.BlockSpec((1, D_MODEL), lambda i: (0, 0)) if mod_rows == 1
                else pl.BlockSpec((tm, D_MODEL), lambda i: (i, 0)))
    row = lambda w: pl.BlockSpec((tm, w), lambda i: (i, 0))
    out_specs = [row(ATT_WIDTH)] * 3 + [row(SGU_WIDTH)] * 2
    out_shape = [jax.ShapeDtypeStruct((n, ATT_WIDTH), F32)] * 3 + [jax.ShapeDtypeStruct((n, SGU_WIDTH), F32)] * 2
    if prompt:
        out_specs += [row(ATT_WIDTH), pl.BlockSpec((ATT_WIDTH, tm), lambda i: (0, i)),
                      pl.BlockSpec((1, 1, ATT_WIDTH), lambda i: (i, 0, 0))]
        out_shape += [jax.ShapeDtypeStruct((n, ATT_WIDTH), BF16), jax.ShapeDtypeStruct((ATT_WIDTH, n), BF16),
                      jax.ShapeDtypeStruct((n // tm, 1, ATT_WIDTH), F32)]
    return pl.pallas_call(
        _inproj_body,
        grid=(n // tm,),
        in_specs=[row(D_MODEL),
                  pl.BlockSpec((1, D_MODEL), lambda i: (0, 0)),
                  mod_spec, mod_spec,
                  pl.BlockSpec((D_MODEL, IN_WIDTH), lambda i: (0, 0)),
                  pl.BlockSpec((1, SGU_WIDTH), lambda i: (0, 0)),
                  pl.BlockSpec((SGU_WIDTH, SGU_WIDTH), lambda i: (0, 0))],
        out_specs=out_specs,
        out_shape=out_shape,
        compiler_params=_params("arbitrary"),
        name="inproj_prompt" if prompt else "inproj_sample",
    )(x, g_mix.reshape(1, D_MODEL), sc, sh, w_in_bf, g_sgu.reshape(1, SGU_WIDTH), bd)


def _select_bias(gate, n_past, axis):
    blk = lax.broadcasted_iota(jnp.int32, gate.shape, axis).astype(F32)
    past = blk < n_past
    gate = jnp.where(past, gate, -jnp.inf)
    bias = jnp.full(gate.shape, MASK_VALUE, F32)
    for _ in range(MOBA_TOPK):
        mx = jnp.max(gate, axis=axis, keepdims=True)
        first = jnp.min(jnp.where(gate == mx, blk, float(gate.shape[axis])), axis=axis, keepdims=True)
        hit = blk == first
        bias = jnp.where(hit, 0.0, bias)
        gate = jnp.where(hit, -jnp.inf, gate)
    return jnp.where(past, bias, MASK_VALUE)


def _moba_body(i_of, j_of, q_ref, kb_ref, vt_ref, km_ref, kp_ref, o_ref, qm_ref, sel_ref, m_ref, l_ref, acc_ref):
    s = pl.program_id(0)
    i = i_of[s]
    j = j_of[s]
    bq = MOBA_BLOCK

    def sweep(diag):
        lane_q = lax.broadcasted_iota(jnp.int32, (1, bq), 1).astype(F32)
        dist0 = ((i - j) * MOBA_BLOCK).astype(F32) + lane_q
        if diag:
            key = lax.broadcasted_iota(jnp.int32, (bq, bq), 0)
            qry = lax.broadcasted_iota(jnp.int32, (bq, bq), 1)
            causal = jnp.where(key <= qry, 0.0, MASK_VALUE)
        for h in range(N_ATT_HEADS):
            g = h // HEADS_PER_GROUP
            k4 = kb_ref[:, g * GROUP_WIDTH:(g + 1) * GROUP_WIDTH]
            s0 = _dot(k4, qm_ref[h]) + kp_ref[h]
            brow = -_SLOPES[h] * dist0
            if diag:
                s0 = s0 + causal
                m_old = jnp.full((1, bq), MASK_VALUE, F32)
                l_old = jnp.zeros((1, bq), F32)
            else:
                brow = brow + sel_ref[h, pl.ds(j, 1), :]
                m_old = m_ref[h:h + 1, :]
                l_old = l_ref[h:h + 1, :]
            m_new = jnp.maximum(m_old, jnp.max(s0, axis=0, keepdims=True) + brow)
            p = jnp.exp(s0 - (m_new - brow))
            alpha = jnp.exp(m_old - m_new)
            l_ref[h:h + 1, :] = alpha * l_old + jnp.sum(p, axis=0, keepdims=True)
            m_ref[h:h + 1, :] = m_new
            rows = slice(h * HEAD_DIM, (h + 1) * HEAD_DIM)
            pv = _dot(vt_ref[rows, :], p.astype(BF16))
            if diag:
                acc_ref[rows, :] = pv
            else:
                acc_ref[rows, :] = alpha * acc_ref[rows, :] + pv

    @pl.when(j == i)
    def _first():
        qt = (q_ref[...] * (HEAD_DIM ** -0.5)).T
        row = lax.broadcasted_iota(jnp.int32, (GROUP_WIDTH, bq), 0) // HEAD_DIM
        for h in range(N_ATT_HEADS):
            g = h // HEADS_PER_GROUP
            qm = jnp.where(row == h % HEADS_PER_GROUP, qt[g * GROUP_WIDTH:(g + 1) * GROUP_WIDTH, :], 0.0)
            qm_ref[h] = qm.astype(BF16)
            gate = _dot3(km_ref[:, g * GROUP_WIDTH:(g + 1) * GROUP_WIDTH], qm)
            sel_ref[h] = _select_bias(gate, i.astype(F32), 0)
        sweep(True)

    @pl.when(j != i)
    def _past():
        sweep(False)

    @pl.when((j == i - 1) | (i == 0))
    def _last():
        for h in range(N_ATT_HEADS):
            rows = slice(h * HEAD_DIM, (h + 1) * HEAD_DIM)
            acc_ref[rows, :] = acc_ref[rows, :] / l_ref[h:h + 1, :]
        o_ref[...] = acc_ref[...].T


def _moba_prompt(q, kb, vt, kmean, kp):
    t = q.shape[0]
    nb = t // MOBA_BLOCK
    i_of = np.concatenate([np.full(i + 1, i, np.int32) for i in range(nb)])
    j_of = np.concatenate([np.concatenate([[i], np.arange(i)]).astype(np.int32) for i in range(nb)])
    grid_spec = pltpu.PrefetchScalarGridSpec(
        num_scalar_prefetch=2,
        grid=(len(i_of),),
        in_specs=[pl.BlockSpec((MOBA_BLOCK, ATT_WIDTH), lambda s, io, jo: (io[s], 0)),
                  pl.BlockSpec((MOBA_BLOCK, ATT_WIDTH), lambda s, io, jo: (jo[s], 0)),
                  pl.BlockSpec((ATT_WIDTH, MOBA_BLOCK), lambda s, io, jo: (0, jo[s])),
                  pl.BlockSpec((nb, ATT_WIDTH), lambda s, io, jo: (0, 0)),
                  pl.BlockSpec((N_ATT_HEADS, MOBA_BLOCK, MOBA_BLOCK), lambda s, io, jo: (0, 0, 0))],
        out_specs=pl.BlockSpec((MOBA_BLOCK, ATT_WIDTH), lambda s, io, jo: (io[s], 0)),
        scratch_shapes=[pltpu.VMEM((N_ATT_HEADS, GROUP_WIDTH, MOBA_BLOCK), BF16),
                        pltpu.VMEM((N_ATT_HEADS, nb, MOBA_BLOCK), F32),
                        pltpu.VMEM((N_ATT_HEADS, MOBA_BLOCK), F32),
                        pltpu.VMEM((N_ATT_HEADS, MOBA_BLOCK), F32),
                        pltpu.VMEM((ATT_WIDTH, MOBA_BLOCK), F32)])
    return pl.pallas_call(
        _moba_body,
        grid_spec=grid_spec,
        out_shape=jax.ShapeDtypeStruct((t, ATT_WIDTH), F32),
        compiler_params=_params("arbitrary"),
        name="moba_prompt",
    )(jnp.asarray(i_of), jnp.asarray(j_of), q, kb, vt, kmean, kp)


def _decode_body(pt_ref, q_ref, kn_ref, vn_ref, *refs, pages, n_blocks, past_len, n_q):
    k_pages = refs[:pages]
    v_pages = refs[pages:2 * pages]
    o_ref = refs[2 * pages]
    km_ref, op_ref, m_ref, l_ref = refs[2 * pages + 1:]
    step = pl.program_id(1)
    rows = N_ATT_HEADS * n_q
    head_of_row = lax.broadcasted_iota(jnp.int32, (rows, ATT_WIDTH), 0) % N_ATT_HEADS
    head_of_lane = lax.broadcasted_iota(jnp.int32, (rows, ATT_WIDTH), 1) // HEAD_DIM
    own_head = head_of_row == head_of_lane
    rid = lax.broadcasted_iota(jnp.int32, (rows, 1), 0)
    slope = jnp.zeros((rows, 1), F32)
    for h in range(N_ATT_HEADS):
        slope = jnp.where(rid % N_ATT_HEADS == h, _SLOPES[h], slope)
    qidx = rid // N_ATT_HEADS
    qpos = (past_len + qidx).astype(F32)
    blk_lane = lax.broadcasted_iota(jnp.int32, (rows, 128), 1)

    @pl.when(step == 0)
    def _init():
        m_ref[...] = jnp.full(m_ref.shape, MASK_VALUE, F32)
        l_ref[...] = jnp.zeros(l_ref.shape, F32)
        km_ref[...] = jnp.zeros(km_ref.shape, F32)

    qm = jnp.where(own_head, q_ref[0] * (HEAD_DIM ** -0.5), 0.0)
    qm_bf = qm.astype(BF16)
    km_lane = lax.broadcasted_iota(jnp.int32, (ATT_WIDTH, 128), 1)
    for b in range(pages // 2):
        blk = step * (pages // 2) + b
        kt = jnp.concatenate([k_pages[2 * b][0], k_pages[2 * b + 1][0]], axis=1)
        vt = jnp.concatenate([v_pages[2 * b][0], v_pages[2 * b + 1][0]], axis=1)
        km_ref[...] = jnp.where(km_lane == blk, jnp.sum(kt, axis=1, keepdims=True) * (1.0 / MOBA_BLOCK), km_ref[...])
        kpos = (blk * MOBA_BLOCK + lax.broadcasted_iota(jnp.int32, (1, MOBA_BLOCK), 1)).astype(F32)
        s = _dot(qm_bf, kt.astype(BF16)) - slope * (qpos - kpos)
        m = jnp.max(s, axis=-1, keepdims=True)
        p = jnp.exp(s - m)
        m_ref[...] = jnp.where(blk_lane == blk, m, m_ref[...])
        l_ref[...] = jnp.where(blk_lane == blk, jnp.sum(p, axis=-1, keepdims=True), l_ref[...])
        op_ref[blk] = jnp.where(own_head, _dot_nt(p.astype(BF16), vt.astype(BF16)), 0.0)

    @pl.when(step == pl.num_programs(1) - 1)
    def _merge():
        gate = _dot3(qm, km_ref[...])
        sel = _select_bias(gate, float(n_blocks), 1)
        kn = kn_ref[0]
        vn = vn_ref[0]
        kidx = lax.broadcasted_iota(jnp.int32, (rows, kn.shape[0]), 1)
        dist = (qidx - kidx).astype(F32)
        s_own = _dot_nt(qm_bf, kn.astype(BF16)) - slope * dist
        s_own = jnp.where(dist >= 0, s_own, MASK_VALUE)
        m_blk = m_ref[...] + sel
        m_all = jnp.maximum(jnp.max(m_blk, axis=-1, keepdims=True), jnp.max(s_own, axis=-1, keepdims=True))
        w = jnp.exp(m_blk - m_all)
        p_own = jnp.exp(s_own - m_all)
        den = jnp.sum(w * l_ref[...], axis=-1, keepdims=True) + jnp.sum(p_own, axis=-1, keepdims=True)
        acc = jnp.where(own_head, _dot(p_own.astype(BF16), vn.astype(BF16)), 0.0)
        for b in range(n_blocks):
            acc = acc + w[:, b:b + 1] * op_ref[b]
        acc = acc / den
        for qi in range(n_q):
            o_ref[0, qi:qi + 1, :] = jnp.sum(acc[qi * N_ATT_HEADS:(qi + 1) * N_ATT_HEADS], axis=0, keepdims=True)


def _moba_decode(q, k_new, v_new, cache_k, cache_v, page_table):
    b, n_q, _ = q.shape
    n_pages = page_table.shape[1]
    past_len = n_pages * PAGE_SIZE
    n_blocks = past_len // MOBA_BLOCK
    assert n_blocks <= 128 and n_pages % 2 == 0
    pages = 8 if n_pages % 8 == 0 else 2
    rows = N_ATT_HEADS * n_q
    q_rows = jnp.repeat(q, N_ATT_HEADS, axis=1)
    pad = ((0, 0), (0, OWN_ROWS - n_q), (0, 0))
    by_seq = lambda r: pl.BlockSpec((1, r, ATT_WIDTH), lambda bi, st, pt: (bi, 0, 0))

    def page_spec(t):
        return pl.BlockSpec((1, ATT_WIDTH, PAGE_SIZE), lambda bi, st, pt: (pt[bi, st * pages + t], 0, 0))

    grid_spec = pltpu.PrefetchScalarGridSpec(
        num_scalar_prefetch=1,
        grid=(b, n_pages // pages),
        in_specs=[by_seq(rows), by_seq(OWN_ROWS), by_seq(OWN_ROWS)] + [page_spec(t) for t in range(pages)] * 2,
        out_specs=by_seq(n_q),
        scratch_shapes=[pltpu.VMEM((ATT_WIDTH, 128), F32),
                        pltpu.VMEM((n_blocks, rows, ATT_WIDTH), F32),
                        pltpu.VMEM((rows, 128), F32),
                        pltpu.VMEM((rows, 128), F32)])
    return pl.pallas_call(
        functools.partial(_decode_body, pages=pages, n_blocks=n_blocks, past_len=past_len, n_q=n_q),
        grid_spec=grid_spec,
        out_shape=jax.ShapeDtypeStruct((b, n_q, ATT_WIDTH), F32),
        compiler_params=_params("arbitrary", "arbitrary"),
        name="moba_decode",
    )(page_table, q_rows, jnp.pad(k_new, pad), jnp.pad(v_new, pad), *([cache_k] * pages), *([cache_v] * pages))


def _mix_body(attn_ref, ug_ref, vsn_ref, x_ref, wsg_ref, bsg_ref, ga_ref, gs_ref, wo_ref, gt_ref, sc_ref, sh_ref,
              gf_ref, wr_ref, br_ref, cin_ref, x1_ref, h2_ref, route_ref, cnt_ref, carry_ref):
    tm = x_ref.shape[0]

    @pl.when(pl.program_id(0) == 0)
    def _():
        carry_ref[...] = cin_ref[...]

    r = lax.broadcasted_iota(jnp.int32, (CHUNK, CHUNK), 0)
    c = lax.broadcasted_iota(jnp.int32, (CHUNK, CHUNK), 1)
    wcat = jnp.concatenate([jnp.where(c <= r, wsg_ref[g], 0.0).astype(BF16) for g in range(N_SGU_GROUPS)], axis=1)
    lane_group = lax.broadcasted_iota(jnp.int32, (CHUNK, SGU_WIDTH), 1) // HEAD_DIM
    mixed = []
    for ch in range(tm // CHUNK):
        vs = vsn_ref[ch * CHUNK:(ch + 1) * CHUNK, :]
        stack = jnp.concatenate([jnp.where(lane_group == g, vs, 0.0).astype(BF16) for g in range(N_SGU_GROUPS)],
                                axis=0)
        mixed.append(_dot(wcat, stack) + bsg_ref[...])
    sgu = ug_ref[...] * (mixed[0] if len(mixed) == 1 else jnp.concatenate(mixed, axis=0))
    merged = jnp.concatenate([_rms(attn_ref[...], ga_ref[...]), _rms(sgu, gs_ref[...])], axis=1)
    x1 = x_ref[...] + gt_ref[...] * _dot(merged.astype(BF16), wo_ref[...])
    x1_ref[...] = x1
    h2 = _rms(x1, gf_ref[...]) * (1.0 + sc_ref[...]) + sh_ref[...]
    h2_ref[...] = h2

    logits = _dot3(h2, wr_ref[...]) + br_ref[...]
    lane_f = lax.broadcasted_iota(jnp.int32, (tm, N_EXPERTS), 1).astype(F32)
    vals, hits = [], []
    work = logits
    for _ in range(TOP_K):
        mx = jnp.max(work, axis=-1, keepdims=True)
        first = jnp.min(jnp.where(work == mx, lane_f, float(N_EXPERTS)), axis=-1, keepdims=True)
        hit = lane_f == first
        vals.append(mx)
        hits.append(hit)
        work = jnp.where(hit, -jnp.inf, work)
    ex = [jnp.exp(v - vals[0]) for v in vals]
    den = ex[0] + ex[1] + ex[2] + ex[3]
    chosen = jnp.zeros((tm, N_EXPERTS), F32)
    for hit in hits:
        chosen = jnp.where(hit, 1.0, chosen)
    rr = lax.broadcasted_iota(jnp.int32, (tm, tm), 0)
    cc = lax.broadcasted_iota(jnp.int32, (tm, tm), 1)
    before = _dot(jnp.where(cc < rr, 1.0, 0.0).astype(BF16), chosen.astype(BF16)) + carry_ref[...]
    rl = lax.broadcasted_iota(jnp.int32, (tm, ROUTE_LANES), 1)
    route = jnp.zeros((tm, ROUTE_LANES), F32)
    for t in range(TOP_K):
        idx = jnp.sum(jnp.where(hits[t], lane_f, 0.0), axis=-1, keepdims=True)
        rank = jnp.sum(jnp.where(hits[t], before, 0.0), axis=-1, keepdims=True)
        route = jnp.where(rl == t, idx, route)
        route = jnp.where(rl == TOP_K + t, ex[t] / den, route)
        route = jnp.where(rl == 2 * TOP_K + t, rank, route)
    route_ref[...] = route
    carry_ref[...] = carry_ref[...] + jnp.sum(chosen, axis=0, keepdims=True)
    cnt_ref[...] = carry_ref[...]


def _mix(attn, ug, vsn, x, wsg, bsg, g_att, g_sgu, w_out_bf, gt, sc, sh, g_ffn, w_router, b_router, cnt_in, tm):
    n = x.shape[0]
    mod_rows = gt.shape[0]
    mod_spec = (pl.BlockSpec((1, D_MODEL), lambda i: (0, 0)) if mod_rows == 1
                else pl.BlockSpec((tm, D_MODEL), lambda i: (i, 0)))
    row = lambda w: pl.BlockSpec((tm, w), lambda i: (i, 0))
    full = lambda *shape: pl.BlockSpec(shape, lambda i: (0,) * len(shape))
    return pl.pallas_call(
        _mix_body,
        grid=(n // tm,),
        in_specs=[row(ATT_WIDTH), row(SGU_WIDTH), row(SGU_WIDTH), row(D_MODEL),
                  full(N_SGU_GROUPS, CHUNK, CHUNK), full(CHUNK, SGU_WIDTH),
                  full(1, ATT_WIDTH), full(1, SGU_WIDTH), full(D_MODEL, D_MODEL),
                  mod_spec, mod_spec, mod_spec,
                  full(1, D_MODEL), full(D_MODEL, N_EXPERTS), full(1, N_EXPERTS), full(1, N_EXPERTS)],
        out_specs=[row(D_MODEL), row(D_MODEL), row(ROUTE_LANES), full(1, N_EXPERTS)],
        out_shape=[jax.ShapeDtypeStruct((n, D_MODEL), F32), jax.ShapeDtypeStruct((n, D_MODEL), F32),
                   jax.ShapeDtypeStruct((n, ROUTE_LANES), F32), jax.ShapeDtypeStruct((1, N_EXPERTS), F32)],
        scratch_shapes=[pltpu.VMEM((1, N_EXPERTS), F32)],
        compiler_params=_params("arbitrary"),
        name="mix",
    )(attn, ug, vsn, x, wsg, bsg, g_att.reshape(1, ATT_WIDTH), g_sgu.reshape(1, SGU_WIDTH), w_out_bf,
      gt, sc, sh, g_ffn.reshape(1, D_MODEL), w_router, b_router.reshape(1, N_EXPERTS), cnt_in)


def _dispatch_body(cnt_ref, pcnt_ref, pstart_ref, dest_ref, h2_ref, zero_ref, xs_ref, sem, tile_sem):
    step = pl.program_id(0)
    n_assign = DISPATCH_TOKENS * TOP_K

    def row_copy(a):
        tok = step * DISPATCH_TOKENS + a // TOP_K
        return pltpu.make_async_copy(h2_ref.at[pl.ds(tok, 1), :], xs_ref.at[pl.ds(dest_ref[a], 1), :], sem)

    def start(a, carry):
        row_copy(a).start()
        return carry

    def wait(a, carry):
        row_copy(a).wait()
        return carry

    lax.fori_loop(0, n_assign, start, 0)
    lax.fori_loop(0, n_assign, wait, 0)

    @pl.when(step == 0)
    def _pad():
        last = N_EXPERTS - 1
        n_used = (pstart_ref[last] + pcnt_ref[last]) // MOE_TILE
        for phase in (True, False):
            def per_expert(e, carry):
                def per_row(rw, c2):
                    cp = pltpu.make_async_copy(zero_ref.at[pl.ds(0, 1), :],
                                               xs_ref.at[pl.ds(pstart_ref[e] + rw, 1), :], sem)
                    cp.start() if phase else cp.wait()
                    return c2
                return lax.fori_loop(cnt_ref[e], pcnt_ref[e], per_row, carry)
            lax.fori_loop(0, N_EXPERTS, per_expert, 0)

            def per_tile(t, carry):
                cp = pltpu.make_async_copy(zero_ref, xs_ref.at[pl.ds(t * MOE_TILE, MOE_TILE), :], tile_sem)
                cp.start() if phase else cp.wait()
                return carry
            lax.fori_loop(n_used, xs_ref.shape[0] // MOE_TILE, per_tile, 0)


def _dispatch(h2, dest_flat, cnt, pcnt, pstart, n_slot):
    n = h2.shape[0]
    n_assign = DISPATCH_TOKENS * TOP_K
    grid_spec = pltpu.PrefetchScalarGridSpec(
        num_scalar_prefetch=3,
        grid=(n // DISPATCH_TOKENS,),
        in_specs=[pl.BlockSpec((n_assign,), lambda s, *_: (s,), memory_space=pltpu.SMEM),
                  pl.BlockSpec(memory_space=pl.ANY),
                  pl.BlockSpec(memory_space=pl.ANY)],
        out_specs=pl.BlockSpec(memory_space=pl.ANY),
        scratch_shapes=[pltpu.SemaphoreType.DMA(()), pltpu.SemaphoreType.DMA(())])
    return pl.pallas_call(
        _dispatch_body,
        grid_spec=grid_spec,
        out_shape=jax.ShapeDtypeStruct((n_slot, D_MODEL), F32),
        compiler_params=_params("arbitrary"),
        name="dispatch",
    )(cnt, pcnt, pstart, dest_flat, h2, jnp.zeros((MOE_TILE, D_MODEL), F32))


def _experts_body(tile_e, n_used, xs_ref, wgu_ref, bgu_ref, wd_ref, bd_ref, o_ref, wgu_bf, wd_bf):
    t = pl.program_id(0)
    e = tile_e[t]
    prev = tile_e[jnp.maximum(t - 1, 0)]

    @pl.when((t == 0) | (e != prev))
    def _():
        wgu_bf[...] = wgu_ref[0].astype(BF16)
        wd_bf[...] = wd_ref[0].astype(BF16)

    @pl.when(t < n_used[0])
    def _():
        gu = _dot(xs_ref[...].astype(BF16), wgu_bf[...]) + bgu_ref[0]
        g = jnp.minimum(gu[:, :D_FF], SWIGLU_LIMIT)
        lin = jnp.clip(gu[:, D_FF:], -SWIGLU_LIMIT, SWIGLU_LIMIT)
        act = g * (1.0 / (1.0 + jnp.exp(-SWIGLU_ALPHA * g))) * (lin + 1.0)
        o_ref[...] = _dot(act.astype(BF16), wd_bf[...]) + bd_ref[0]

    @pl.when(t >= n_used[0])
    def _():
        o_ref[...] = jnp.zeros(o_ref.shape, F32)


def _experts(xs, tile_e, n_used, w_gu, b_gu, w_down, b_down):
    n_slot = xs.shape[0]
    n_tiles = n_slot // MOE_TILE
    by_e = lambda *blk: pl.BlockSpec((1,) + blk, lambda t, te, nu: (te[t],) + (0,) * len(blk))
    grid_spec = pltpu.PrefetchScalarGridSpec(
        num_scalar_prefetch=2,
        grid=(n_tiles,),
        in_specs=[pl.BlockSpec((MOE_TILE, D_MODEL), lambda t, te, nu: (jnp.minimum(t, nu[0] - 1), 0)),
                  by_e(D_MODEL, 2 * D_FF), by_e(1, 2 * D_FF), by_e(D_FF, D_MODEL), by_e(1, D_MODEL)],
        out_specs=pl.BlockSpec((MOE_TILE, D_MODEL), lambda t, te, nu: (t, 0)),
        scratch_shapes=[pltpu.VMEM((D_MODEL, 2 * D_FF), BF16), pltpu.VMEM((D_FF, D_MODEL), BF16)])
    return pl.pallas_call(
        _experts_body,
        grid_spec=grid_spec,
        out_shape=jax.ShapeDtypeStruct((n_slot, D_MODEL), F32),
        compiler_params=_params("arbitrary"),
        name="experts",
    )(tile_e, n_used, xs, w_gu, b_gu.reshape(N_EXPERTS, 1, 2 * D_FF), w_down, b_down.reshape(N_EXPERTS, 1, D_MODEL))


def _combine_body(dest_ref, ys_ref, x1_ref, route_ref, gt_ref, gfin_ref, o_ref, buf, sem):
    tc = x1_ref.shape[0]

    def row_copy(a):
        return pltpu.make_async_copy(ys_ref.at[pl.ds(dest_ref[a], 1), :],
                                     buf.at[a % TOP_K, pl.ds(a // TOP_K, 1), :], sem)

    def start(a, carry):
        row_copy(a).start()
        return carry

    def wait(a, carry):
        row_copy(a).wait()
        return carry

    lax.fori_loop(0, tc * TOP_K, start, 0)
    lax.fori_loop(0, tc * TOP_K, wait, 0)
    route = route_ref[...]
    moe = route[:, TOP_K:TOP_K + 1] * buf[0]
    for t in range(1, TOP_K):
        moe = moe + route[:, TOP_K + t:TOP_K + t + 1] * buf[t]
    o_ref[...] = _rms(x1_ref[...] + gt_ref[...] * moe, gfin_ref[...])


def _combine(ys, dest_flat, x1, route, gt, g_final):
    n = x1.shape[0]
    tc = COMBINE_TOKENS
    mod_spec = (pl.BlockSpec((1, D_MODEL), lambda i: (0, 0)) if gt.shape[0] == 1
                else pl.BlockSpec((tc, D_MODEL), lambda i: (i, 0)))
    return pl.pallas_call(
        _combine_body,
        grid=(n // tc,),
        in_specs=[pl.BlockSpec((tc * TOP_K,), lambda i: (i,), memory_space=pltpu.SMEM),
                  pl.BlockSpec(memory_space=pl.ANY),
                  pl.BlockSpec((tc, D_MODEL), lambda i: (i, 0)),
                  pl.BlockSpec((tc, ROUTE_LANES), lambda i: (i, 0)),
                  mod_spec,
                  pl.BlockSpec((1, D_MODEL), lambda i: (0, 0))],
        out_specs=pl.BlockSpec((tc, D_MODEL), lambda i: (i, 0)),
        out_shape=jax.ShapeDtypeStruct((n, D_MODEL), F32),
        scratch_shapes=[pltpu.VMEM((TOP_K, tc, D_MODEL), F32), pltpu.SemaphoreType.DMA(())],
        compiler_params=_params("arbitrary"),
        name="combine",
    )(dest_flat, ys, x1, route, gt, g_final.reshape(1, D_MODEL))


def _pages_keys_minor(cache):
    return cache.transpose(0, 2, 3, 1).reshape(cache.shape[0], ATT_WIDTH, PAGE_SIZE)


def _group_mean_matrix():
    g = np.arange(SGU_WIDTH) // HEAD_DIM
    return jnp.asarray((g[:, None] == g[None, :]).astype(np.float32) / HEAD_DIM, BF16)


def _key_offset_bias():
    off = np.arange(MOBA_BLOCK, dtype=np.float32)[None, :, None]
    return jnp.asarray(np.broadcast_to(np.asarray(_SLOPES, np.float32)[:, None, None] * off,
                                       (N_ATT_HEADS, MOBA_BLOCK, MOBA_BLOCK)))


def kernel(x_prompt, x_sample, cache_k, cache_v, page_table, c_prompt, c_sample, g_mix, w_ada, b_ada, w_in,
           g_sgu, w_sgu, b_sgu, g_out_att, g_out_sgu, w_out, g_ffn, w_router, b_router, w_gu, b_gu, w_down,
           b_down, g_final):
    depth = w_in.shape[0]
    batch, seq, _ = x_prompt.shape
    dec_batch, dec_seq, _ = x_sample.shape
    assert depth == 1, "the final norm is fused into the last stage of the single layer"
    assert batch == 1 and seq % MOBA_BLOCK == 0
    n_dec = dec_batch * dec_seq
    assert n_dec % CHUNK == 0 and CHUNK % dec_seq == 0 and dec_seq <= MOBA_BLOCK
    n_tok = seq + n_dec
    n_pool = cache_k.shape[1]

    bd = _group_mean_matrix()
    kp = _key_offset_bias()
    xp = x_prompt.reshape(seq, D_MODEL)
    xs = x_sample.reshape(n_dec, D_MODEL)
    c_all = jnp.concatenate([c_prompt, c_sample], axis=0)
    c_rows = -(-c_all.shape[0] // 8) * 8
    c_all = jnp.pad(c_all, ((0, c_rows - c_all.shape[0]), (0, 0)))
    k_rows, v_rows, ks_rows, vs_rows, sgu_rows = [], [], [], [], []

    for l in range(depth):
        mod = _ada(c_all, w_ada[l], b_ada[l])
        mod_p = [mod[0:1, t * D_MODEL:(t + 1) * D_MODEL] for t in range(6)]
        mod_s = [jnp.repeat(mod[1:1 + dec_batch, t * D_MODEL:(t + 1) * D_MODEL], dec_seq, axis=0) for t in range(6)]
        w_in_bf = w_in[l].astype(BF16)
        w_out_bf = w_out[l].astype(BF16)
        gs_flat = g_sgu[l].reshape(SGU_WIDTH)

        q, k, v, ug, vsn, kb, vt, kmean = _inproj(xp, g_mix[l], mod_p[1], mod_p[0], w_in_bf, gs_flat, bd, True)
        attn = _moba_prompt(q, kb, vt, kmean.reshape(seq // MOBA_BLOCK, ATT_WIDTH), kp)
        bias_p = jnp.repeat(b_sgu[l].T, HEAD_DIM, axis=1)
        x1_p, h2_p, route_p, cnt_p = _mix(attn, ug, vsn, xp, w_sgu[l], bias_p, g_out_att[l], g_out_sgu[l],
                                          w_out_bf, mod_p[2], mod_p[4], mod_p[3], g_ffn[l], w_router[l],
                                          b_router[l], jnp.zeros((1, N_EXPERTS), F32), MOBA_BLOCK)
        k_rows.append(k)
        v_rows.append(v)

        qs, ks, vs, ugs, vsns = _inproj(xs, g_mix[l], mod_s[1], mod_s[0], w_in_bf, gs_flat, bd, False)
        attn_s = _moba_decode(qs.reshape(dec_batch, dec_seq, ATT_WIDTH), ks.reshape(dec_batch, dec_seq, ATT_WIDTH),
                              vs.reshape(dec_batch, dec_seq, ATT_WIDTH),
                              _pages_keys_minor(cache_k[l]), _pages_keys_minor(cache_v[l]), page_table)
        eye = jnp.eye(CHUNK // dec_seq, dtype=F32)
        w_s = jax.vmap(lambda w: jnp.kron(eye, w))(w_sgu[l][:, :dec_seq, :dec_seq])
        bias_s = jnp.tile(jnp.repeat(b_sgu[l].T[:dec_seq], HEAD_DIM, axis=1), (CHUNK // dec_seq, 1))
        x1_s, h2_s, route_s, cnt = _mix(attn_s.reshape(n_dec, ATT_WIDTH), ugs, vsns, xs, w_s, bias_s, g_out_att[l],
                                        g_out_sgu[l], w_out_bf, mod_s[2], mod_s[4], mod_s[3], g_ffn[l],
                                        w_router[l], b_router[l], cnt_p, CHUNK)
        ks_rows.append(ks)
        vs_rows.append(vs)
        sgu_rows.append(vsns)

        route = jnp.concatenate([route_p, route_s], axis=0)
        e_idx = route[:, :TOP_K].astype(jnp.int32)
        rank = route[:, 2 * TOP_K:3 * TOP_K].astype(jnp.int32)
        cnt_i = cnt.reshape(N_EXPERTS).astype(jnp.int32)
        pcnt = (cnt_i + MOE_TILE - 1) // MOE_TILE * MOE_TILE
        pend = jnp.cumsum(pcnt)
        pstart = pend - pcnt
        dest = (pstart[e_idx] + rank).reshape(-1)
        n_tiles = -(-(n_tok * TOP_K) // MOE_TILE) + N_EXPERTS
        tile_e = jnp.clip(jnp.searchsorted(pend, jnp.arange(n_tiles, dtype=jnp.int32) * MOE_TILE, side='right'),
                          0, N_EXPERTS - 1).astype(jnp.int32)
        n_used = (pend[-1:] // MOE_TILE).astype(jnp.int32)

        h2 = jnp.concatenate([h2_p, h2_s], axis=0)
        slots = _dispatch(h2, dest, cnt_i, pcnt, pstart, n_tiles * MOE_TILE)
        ys = _experts(slots, tile_e, n_used, w_gu[l], b_gu[l], w_down[l], b_down[l])
        xp = _combine(ys, dest[:seq * TOP_K], x1_p, route_p, mod_p[5], g_final)
        xs = _combine(ys, dest[seq * TOP_K:], x1_s, route_s, mod_s[5], g_final)

    hd = (N_ATT_HEADS, HEAD_DIM)
    k_p = jnp.stack(k_rows).reshape(depth, batch, seq // PAGE_SIZE, PAGE_SIZE, *hd)
    v_p = jnp.stack(v_rows).reshape(depth, batch, seq // PAGE_SIZE, PAGE_SIZE, *hd)
    k_s = jnp.stack(ks_rows).reshape(depth, dec_batch, dec_seq, *hd)
    v_s = jnp.stack(vs_rows).reshape(depth, dec_batch, dec_seq, *hd)
    sgu_v = jnp.stack(sgu_rows).reshape(depth, dec_batch, dec_seq, N_SGU_GROUPS, HEAD_DIM)
    return (xp.reshape(batch, seq, D_MODEL), xs.reshape(dec_batch, dec_seq, D_MODEL), k_p, v_p, k_s, v_s, sgu_v)
```

```python
import functools

import numpy as np
import jax
import jax.numpy as jnp
from jax import lax
from jax.experimental import pallas as pl
from jax.experimental.pallas import tpu as pltpu

F32 = jnp.float32
BF16 = jnp.bfloat16

D_MODEL = 1024
HEAD_DIM = 64
N_ATT_HEADS = 8
N_SGU_GROUPS = 8
ATT_WIDTH = N_ATT_HEADS * HEAD_DIM
SGU_WIDTH = N_SGU_GROUPS * HEAD_DIM
IN_WIDTH = 3 * ATT_WIDTH + 2 * SGU_WIDTH
CHUNK = 128
MOBA_BLOCK = 256
MOBA_TOPK = 3
PAGE_SIZE = 128
N_EXPERTS = 32
TOP_K = 4
D_FF = D_MODEL
SWIGLU_LIMIT = 7.0
SWIGLU_ALPHA = 1.702
EPS = 1e-6
MASK_VALUE = -1e30

HEADS_PER_GROUP = 4
GROUP_WIDTH = HEADS_PER_GROUP * HEAD_DIM
ROUTE_LANES = 128
MOE_TILE = 256
DISPATCH_TOKENS = 128
COMBINE_TOKENS = 128
OWN_ROWS = 16
VMEM_LIMIT = 56 * 1024 * 1024

_SLOPES = [2.0 ** (-8.0 * (h + 1) / N_ATT_HEADS) for h in range(N_ATT_HEADS)]


def _params(*sem):
    return pltpu.CompilerParams(dimension_semantics=sem, vmem_limit_bytes=VMEM_LIMIT)


def _split(a):
    hi = a.astype(BF16)
    lo = (a - hi.astype(F32)).astype(BF16)
    return hi, lo


def _dot(a, b):
    return jnp.dot(a, b, preferred_element_type=F32)


def _dot_nt(a, b):
    return lax.dot_general(a, b, (((1,), (1,)), ((), ())), preferred_element_type=F32)


def _dot3(a, b, dot=_dot):
    a_hi, a_lo = _split(a)
    b_hi, b_lo = _split(b)
    return dot(a_hi, b_hi) + (dot(a_hi, b_lo) + dot(a_lo, b_hi))


def _rms(x, g):
    return x * lax.rsqrt(jnp.mean(x * x, axis=-1, keepdims=True) + EPS) * g


def _gelu(x):
    return x * (0.5 * (1.0 + jnp.tanh(0.7978845608028654 * (x + 0.044715 * (x * x * x)))))


def _ada_body(c_ref, w_ref, b_ref, o_ref):
    c = c_ref[...]
    s = c * (1.0 / (1.0 + jnp.exp(-c)))
    o_ref[...] = _dot3(s, w_ref[...]) + b_ref[...]


def _ada(c, w, b):
    rows = c.shape[0]
    n = w.shape[1]
    tn = 1536
    return pl.pallas_call(
        _ada_body,
        grid=(n // tn,),
        in_specs=[pl.BlockSpec((rows, D_MODEL), lambda j: (0, 0)),
                  pl.BlockSpec((D_MODEL, tn), lambda j: (0, j)),
                  pl.BlockSpec((1, tn), lambda j: (0, j))],
        out_specs=pl.BlockSpec((rows, tn), lambda j: (0, j)),
        out_shape=jax.ShapeDtypeStruct((rows, n), F32),
        compiler_params=_params("arbitrary"),
        name="ada",
    )(c, w, b.reshape(1, n))


def _inproj_body(x_ref, g_ref, sc_ref, sh_ref, w_ref, gs_ref, bd_ref, q_ref, k_ref, v_ref, ug_ref, vsn_ref,
                 *prompt_refs):
    h = _rms(x_ref[...], g_ref[...]) * (1.0 + sc_ref[...]) + sh_ref[...]
    z = _dot(h.astype(BF16), w_ref[...])
    a = ATT_WIDTH
    q_ref[...] = z[:, :a]
    k = z[:, a:2 * a]
    v = z[:, 2 * a:3 * a]
    k_ref[...] = k
    v_ref[...] = v
    ug_ref[...] = _gelu(z[:, 3 * a:3 * a + SGU_WIDTH])
    vg = _gelu(z[:, 3 * a + SGU_WIDTH:])
    sq_hi, sq_lo = _split(vg * vg)
    gmean = _dot(sq_hi, bd_ref[...]) + _dot(sq_lo, bd_ref[...])
    vsn_ref[...] = vg * lax.rsqrt(gmean + EPS) * gs_ref[...]
    if prompt_refs:
        kb_ref, vt_ref, km_ref = prompt_refs
        kb_ref[...] = k.astype(BF16)
        vt_ref[...] = v.T.astype(BF16)
        km_ref[0] = jnp.mean(k, axis=0, keepdims=True)


def _inproj(x, g_mix, sc, sh, w_in_bf, g_sgu, bd, prompt):
    n = x.shape[0]
    tm = MOBA_BLOCK if prompt else n
    mod_rows = sc.shape[0]
    mod_spec = (pl.BlockSpec((1, D_MODEL), lambda i: (0, 0)) if mod_rows == 1
                else pl.BlockSpec((tm, D_MODEL), lambda i: (i, 0)))
    row = lambda w: pl.BlockSpec((tm, w), lambda i: (i, 0))
    out_specs = [row(ATT_WIDTH)] * 3 + [row(SGU_WIDTH)] * 2
    out_shape = [jax.ShapeDtypeStruct((n, ATT_WIDTH), F32)] * 3 + [jax.ShapeDtypeStruct((n, SGU_WIDTH), F32)] * 2
    if prompt:
        out_specs += [row(ATT_WIDTH), pl.BlockSpec((ATT_WIDTH, tm), lambda i: (0, i)),
                      pl.BlockSpec((1, 1, ATT_WIDTH), lambda i: (i, 0, 0))]
        out_shape += [jax.ShapeDtypeStruct((n, ATT_WIDTH), BF16), jax.ShapeDtypeStruct((ATT_WIDTH, n), BF16),
                      jax.ShapeDtypeStruct((n // tm, 1, ATT_WIDTH), F32)]
    return pl.pallas_call(
        _inproj_body,
        grid=(n // tm,),
        in_specs=[row(D_MODEL),
                  pl.BlockSpec((1, D_MODEL), lambda i: (0, 0)),
                  mod_spec, mod_spec,
                  pl.BlockSpec((D_MODEL, IN_WIDTH), lambda i: (0, 0)),
                  pl.BlockSpec((1, SGU_WIDTH), lambda i: (0, 0)),
                  pl.BlockSpec((SGU_WIDTH, SGU_WIDTH), lambda i: (0, 0))],
        out_specs=out_specs,
        out_shape=out_shape,
        compiler_params=_params("arbitrary"),
        name="inproj_prompt" if prompt else "inproj_sample",
    )(x, g_mix.reshape(1, D_MODEL), sc, sh, w_in_bf, g_sgu.reshape(1, SGU_WIDTH), bd)


def _select_bias(gate, n_past, axis):
    blk = lax.broadcasted_iota(jnp.int32, gate.shape, axis).astype(F32)
    past = blk < n_past
    gate = jnp.where(past, gate, -jnp.inf)
    bias = jnp.full(gate.shape, MASK_VALUE, F32)
    for _ in range(MOBA_TOPK):
        mx = jnp.max(gate, axis=axis, keepdims=True)
        first = jnp.min(jnp.where(gate == mx, blk, float(gate.shape[axis])), axis=axis, keepdims=True)
        hit = blk == first
        bias = jnp.where(hit, 0.0, bias)
        gate = jnp.where(hit, -jnp.inf, gate)
    return jnp.where(past, bias, MASK_VALUE)


def _moba_body(i_of, j_of, q_ref, kb_ref, vt_ref, km_ref, kp_ref, o_ref, qm_ref, sel_ref, m_ref, l_ref, acc_ref):
    s = pl.program_id(0)
    i = i_of[s]
    j = j_of[s]
    bq = MOBA_BLOCK

    def sweep(diag):
        lane_q = lax.broadcasted_iota(jnp.int32, (1, bq), 1).astype(F32)
        dist0 = ((i - j) * MOBA_BLOCK).astype(F32) + lane_q
        if diag:
            key = lax.broadcasted_iota(jnp.int32, (bq, bq), 0)
            qry = lax.broadcasted_iota(jnp.int32, (bq, bq), 1)
            causal = jnp.where(key <= qry, 0.0, MASK_VALUE)
        for h in range(N_ATT_HEADS):
            g = h // HEADS_PER_GROUP
            k4 = kb_ref[:, g * GROUP_WIDTH:(g + 1) * GROUP_WIDTH]
            s0 = _dot(k4, qm_ref[h]) + kp_ref[h]
            brow = -_SLOPES[h] * dist0
            if diag:
                s0 = s0 + causal
                m_old = jnp.full((1, bq), MASK_VALUE, F32)
                l_old = jnp.zeros((1, bq), F32)
            else:
                brow = brow + sel_ref[h, pl.ds(j, 1), :]
                m_old = m_ref[h:h + 1, :]
                l_old = l_ref[h:h + 1, :]
            m_new = jnp.maximum(m_old, jnp.max(s0, axis=0, keepdims=True) + brow)
            p = jnp.exp(s0 - (m_new - brow))
            alpha = jnp.exp(m_old - m_new)
            l_ref[h:h + 1, :] = alpha * l_old + jnp.sum(p, axis=0, keepdims=True)
            m_ref[h:h + 1, :] = m_new
            rows = slice(h * HEAD_DIM, (h + 1) * HEAD_DIM)
            pv = _dot(vt_ref[rows, :], p.astype(BF16))
            if diag:
                acc_ref[rows, :] = pv
            else:
                acc_ref[rows, :] = alpha * acc_ref[rows, :] + pv

    @pl.when(j == i)
    def _first():
        qt = (q_ref[...] * (HEAD_DIM ** -0.5)).T
        row = lax.broadcasted_iota(jnp.int32, (GROUP_WIDTH, bq), 0) // HEAD_DIM
        for h in range(N_ATT_HEADS):
            g = h // HEADS_PER_GROUP
            qm = jnp.where(row == h % HEADS_PER_GROUP, qt[g * GROUP_WIDTH:(g + 1) * GROUP_WIDTH, :], 0.0)
            qm_ref[h] = qm.astype(BF16)
            gate = _dot3(km_ref[:, g * GROUP_WIDTH:(g + 1) * GROUP_WIDTH], qm)
            sel_ref[h] = _select_bias(gate, i.astype(F32), 0)
        sweep(True)

    @pl.when(j != i)
    def _past():
        sweep(False)

    @pl.when((j == i - 1) | (i == 0))
    def _last():
        for h in range(N_ATT_HEADS):
            rows = slice(h * HEAD_DIM, (h + 1) * HEAD_DIM)
            acc_ref[rows, :] = acc_ref[rows, :] / l_ref[h:h + 1, :]
        o_ref[...] = acc_ref[...].T


def _moba_prompt(q, kb, vt, kmean, kp):
    t = q.shape[0]
    nb = t // MOBA_BLOCK
    i_of = np.concatenate([np.full(i + 1, i, np.int32) for i in range(nb)])
    j_of = np.concatenate([np.concatenate([[i], np.arange(i)]).astype(np.int32) for i in range(nb)])
    grid_spec = pltpu.PrefetchScalarGridSpec(
        num_scalar_prefetch=2,
        grid=(len(i_of),),
        in_specs=[pl.BlockSpec((MOBA_BLOCK, ATT_WIDTH), lambda s, io, jo: (io[s], 0)),
                  pl.BlockSpec((MOBA_BLOCK, ATT_WIDTH), lambda s, io, jo: (jo[s], 0)),
                  pl.BlockSpec((ATT_WIDTH, MOBA_BLOCK), lambda s, io, jo: (0, jo[s])),
                  pl.BlockSpec((nb, ATT_WIDTH), lambda s, io, jo: (0, 0)),
                  pl.BlockSpec((N_ATT_HEADS, MOBA_BLOCK, MOBA_BLOCK), lambda s, io, jo: (0, 0, 0))],
        out_specs=pl.BlockSpec((MOBA_BLOCK, ATT_WIDTH), lambda s, io, jo: (io[s], 0)),
        scratch_shapes=[pltpu.VMEM((N_ATT_HEADS, GROUP_WIDTH, MOBA_BLOCK), BF16),
                        pltpu.VMEM((N_ATT_HEADS, nb, MOBA_BLOCK), F32),
                        pltpu.VMEM((N_ATT_HEADS, MOBA_BLOCK), F32),
                        pltpu.VMEM((N_ATT_HEADS, MOBA_BLOCK), F32),
                        pltpu.VMEM((ATT_WIDTH, MOBA_BLOCK), F32)])
    return pl.pallas_call(
        _moba_body,
        grid_spec=grid_spec,
        out_shape=jax.ShapeDtypeStruct((t, ATT_WIDTH), F32),
        compiler_params=_params("arbitrary"),
        name="moba_prompt",
    )(jnp.asarray(i_of), jnp.asarray(j_of), q, kb, vt, kmean, kp)


def _decode_body(pt_ref, q_ref, kn_ref, vn_ref, *refs, pages, n_blocks, past_len, n_q):
    k_pages = refs[:pages]
    v_pages = refs[pages:2 * pages]
    o_ref = refs[2 * pages]
    km_ref, op_ref, m_ref, l_ref = refs[2 * pages + 1:]
    step = pl.program_id(1)
    rows = N_ATT_HEADS * n_q
    head_of_row = lax.broadcasted_iota(jnp.int32, (rows, ATT_WIDTH), 0) % N_ATT_HEADS
    head_of_lane = lax.broadcasted_iota(jnp.int32, (rows, ATT_WIDTH), 1) // HEAD_DIM
    own_head = head_of_row == head_of_lane
    rid = lax.broadcasted_iota(jnp.int32, (rows, 1), 0)
    slope = jnp.zeros((rows, 1), F32)
    for h in range(N_ATT_HEADS):
        slope = jnp.where(rid % N_ATT_HEADS == h, _SLOPES[h], slope)
    qidx = rid // N_ATT_HEADS
    qpos = (past_len + qidx).astype(F32)
    blk_lane = lax.broadcasted_iota(jnp.int32, (rows, 128), 1)

    @pl.when(step == 0)
    def _init():
        m_ref[...] = jnp.full(m_ref.shape, MASK_VALUE, F32)
        l_ref[...] = jnp.zeros(l_ref.shape, F32)
        km_ref[...] = jnp.zeros(km_ref.shape, F32)

    qm = jnp.where(own_head, q_ref[0] * (HEAD_DIM ** -0.5), 0.0)
    qm_bf = qm.astype(BF16)
    km_lane = lax.broadcasted_iota(jnp.int32, (ATT_WIDTH, 128), 1)
    for b in range(pages // 2):
        blk = step * (pages // 2) + b
        kt = jnp.concatenate([k_pages[2 * b][0], k_pages[2 * b + 1][0]], axis=1)
        vt = jnp.concatenate([v_pages[2 * b][0], v_pages[2 * b + 1][0]], axis=1)
        km_ref[...] = jnp.where(km_lane == blk, jnp.sum(kt, axis=1, keepdims=True) * (1.0 / MOBA_BLOCK), km_ref[...])
        kpos = (blk * MOBA_BLOCK + lax.broadcasted_iota(jnp.int32, (1, MOBA_BLOCK), 1)).astype(F32)
        s = _dot(qm_bf, kt.astype(BF16)) - slope * (qpos - kpos)
        m = jnp.max(s, axis=-1, keepdims=True)
        p = jnp.exp(s - m)
        m_ref[...] = jnp.where(blk_lane == blk, m, m_ref[...])
        l_ref[...] = jnp.where(blk_lane == blk, jnp.sum(p, axis=-1, keepdims=True), l_ref[...])
        op_ref[blk] = jnp.where(own_head, _dot_nt(p.astype(BF16), vt.astype(BF16)), 0.0)

    @pl.when(step == pl.num_programs(1) - 1)
    def _merge():
        gate = _dot3(qm, km_ref[...])
        sel = _select_bias(gate, float(n_blocks), 1)
        kn = kn_ref[0]
        vn = vn_ref[0]
        kidx = lax.broadcasted_iota(jnp.int32, (rows, kn.shape[0]), 1)
        dist = (qidx - kidx).astype(F32)
        s_own = _dot_nt(qm_bf, kn.astype(BF16)) - slope * dist
        s_own = jnp.where(dist >= 0, s_own, MASK_VALUE)
        m_blk = m_ref[...] + sel
        m_all = jnp.maximum(jnp.max(m_blk, axis=-1, keepdims=True), jnp.max(s_own, axis=-1, keepdims=True))
        w = jnp.exp(m_blk - m_all)
        p_own = jnp.exp(s_own - m_all)
        den = jnp.sum(w * l_ref[...], axis=-1, keepdims=True) + jnp.sum(p_own, axis=-1, keepdims=True)
        acc = jnp.where(own_head, _dot(p_own.astype(BF16), vn.astype(BF16)), 0.0)
        for b in range(n_blocks):
            acc = acc + w[:, b:b + 1] * op_ref[b]
        acc = acc / den
        for qi in range(n_q):
            o_ref[0, qi:qi + 1, :] = jnp.sum(acc[qi * N_ATT_HEADS:(qi + 1) * N_ATT_HEADS], axis=0, keepdims=True)


def _moba_decode(q, k_new, v_new, cache_k, cache_v, page_table):
    b, n_q, _ = q.shape
    n_pages = page_table.shape[1]
    past_len = n_pages * PAGE_SIZE
    n_blocks = past_len // MOBA_BLOCK
    assert n_blocks <= 128 and n_pages % 2 == 0
    pages = 8 if n_pages % 8 == 0 else 2
    rows = N_ATT_HEADS * n_q
    q_rows = jnp.repeat(q, N_ATT_HEADS, axis=1)
    pad = ((0, 0), (0, OWN_ROWS - n_q), (0, 0))
    by_seq = lambda r: pl.BlockSpec((1, r, ATT_WIDTH), lambda bi, st, pt: (bi, 0, 0))

    def page_spec(t):
        return pl.BlockSpec((1, ATT_WIDTH, PAGE_SIZE), lambda bi, st, pt: (pt[bi, st * pages + t], 0, 0))

    grid_spec = pltpu.PrefetchScalarGridSpec(
        num_scalar_prefetch=1,
        grid=(b, n_pages // pages),
        in_specs=[by_seq(rows), by_seq(OWN_ROWS), by_seq(OWN_ROWS)] + [page_spec(t) for t in range(pages)] * 2,
        out_specs=by_seq(n_q),
        scratch_shapes=[pltpu.VMEM((ATT_WIDTH, 128), F32),
                        pltpu.VMEM((n_blocks, rows, ATT_WIDTH), F32),
                        pltpu.VMEM((rows, 128), F32),
                        pltpu.VMEM((rows, 128), F32)])
    return pl.pallas_call(
        functools.partial(_decode_body, pages=pages, n_blocks=n_blocks, past_len=past_len, n_q=n_q),
        grid_spec=grid_spec,
        out_shape=jax.ShapeDtypeStruct((b, n_q, ATT_WIDTH), F32),
        compiler_params=_params("arbitrary", "arbitrary"),
        name="moba_decode",
    )(page_table, q_rows, jnp.pad(k_new, pad), jnp.pad(v_new, pad), *([cache_k] * pages), *([cache_v] * pages))


def _mix_body(attn_ref, ug_ref, vsn_ref, x_ref, wsg_ref, bsg_ref, ga_ref, gs_ref, wo_ref, gt_ref, sc_ref, sh_ref,
              gf_ref, wr_ref, br_ref, cin_ref, x1_ref, h2_ref, route_ref, cnt_ref, carry_ref):
    tm = x_ref.shape[0]

    @pl.when(pl.program_id(0) == 0)
    def _():
        carry_ref[...] = cin_ref[...]

    r = lax.broadcasted_iota(jnp.int32, (CHUNK, CHUNK), 0)
    c = lax.broadcasted_iota(jnp.int32, (CHUNK, CHUNK), 1)
    wcat = jnp.concatenate([jnp.where(c <= r, wsg_ref[g], 0.0).astype(BF16) for g in range(N_SGU_GROUPS)], axis=1)
    lane_group = lax.broadcasted_iota(jnp.int32, (CHUNK, SGU_WIDTH), 1) // HEAD_DIM
    mixed = []
    for ch in range(tm // CHUNK):
        vs = vsn_ref[ch * CHUNK:(ch + 1) * CHUNK, :]
        stack = jnp.concatenate([jnp.where(lane_group == g, vs, 0.0).astype(BF16) for g in range(N_SGU_GROUPS)],
                                axis=0)
        mixed.append(_dot(wcat, stack) + bsg_ref[...])
    sgu = ug_ref[...] * (mixed[0] if len(mixed) == 1 else jnp.concatenate(mixed, axis=0))
    merged = jnp.concatenate([_rms(attn_ref[...], ga_ref[...]), _rms(sgu, gs_ref[...])], axis=1)
    x1 = x_ref[...] + gt_ref[...] * _dot(merged.astype(BF16), wo_ref[...])
    x1_ref[...] = x1
    h2 = _rms(x1, gf_ref[...]) * (1.0 + sc_ref[...]) + sh_ref[...]
    h2_ref[...] = h2.reshape(tm, 1, D_MODEL)

    logits = _dot3(h2, wr_ref[...]) + br_ref[...]
    lane_f = lax.broadcasted_iota(jnp.int32, (tm, N_EXPERTS), 1).astype(F32)
    vals, hits = [], []
    work = logits
    for _ in range(TOP_K):
        mx = jnp.max(work, axis=-1, keepdims=True)
        first = jnp.min(jnp.where(work == mx, lane_f, float(N_EXPERTS)), axis=-1, keepdims=True)
        hit = lane_f == first
        vals.append(mx)
        hits.append(hit)
        work = jnp.where(hit, -jnp.inf, work)
    ex = [jnp.exp(v - vals[0]) for v in vals]
    den = ex[0] + ex[1] + ex[2] + ex[3]
    chosen = jnp.zeros((tm, N_EXPERTS), F32)
    for hit in hits:
        chosen = jnp.where(hit, 1.0, chosen)
    rr = lax.broadcasted_iota(jnp.int32, (tm, tm), 0)
    cc = lax.broadcasted_iota(jnp.int32, (tm, tm), 1)
    before = _dot(jnp.where(cc < rr, 1.0, 0.0).astype(BF16), chosen.astype(BF16)) + carry_ref[...]
    rl = lax.broadcasted_iota(jnp.int32, (tm, ROUTE_LANES), 1)
    route = jnp.zeros((tm, ROUTE_LANES), F32)
    for t in range(TOP_K):
        idx = jnp.sum(jnp.where(hits[t], lane_f, 0.0), axis=-1, keepdims=True)
        rank = jnp.sum(jnp.where(hits[t], before, 0.0), axis=-1, keepdims=True)
        route = jnp.where(rl == t, idx, route)
        route = jnp.where(rl == TOP_K + t, ex[t] / den, route)
        route = jnp.where(rl == 2 * TOP_K + t, rank, route)
    route_ref[...] = route
    carry_ref[...] = carry_ref[...] + jnp.sum(chosen, axis=0, keepdims=True)
    cnt_ref[...] = carry_ref[...]


def _mix(attn, ug, vsn, x, wsg, bsg, g_att, g_sgu, w_out_bf, gt, sc, sh, g_ffn, w_router, b_router, cnt_in, tm):
    n = x.shape[0]
    mod_rows = gt.shape[0]
    mod_spec = (pl.BlockSpec((1, D_MODEL), lambda i: (0, 0)) if mod_rows == 1
                else pl.BlockSpec((tm, D_MODEL), lambda i: (i, 0)))
    row = lambda w: pl.BlockSpec((tm, w), lambda i: (i, 0))
    full = lambda *shape: pl.BlockSpec(shape, lambda i: (0,) * len(shape))
    return pl.pallas_call(
        _mix_body,
        grid=(n // tm,),
        in_specs=[row(ATT_WIDTH), row(SGU_WIDTH), row(SGU_WIDTH), row(D_MODEL),
                  full(N_SGU_GROUPS, CHUNK, CHUNK), full(CHUNK, SGU_WIDTH),
                  full(1, ATT_WIDTH), full(1, SGU_WIDTH), full(D_MODEL, D_MODEL),
                  mod_spec, mod_spec, mod_spec,
                  full(1, D_MODEL), full(D_MODEL, N_EXPERTS), full(1, N_EXPERTS), full(1, N_EXPERTS)],
        out_specs=[row(D_MODEL), pl.BlockSpec((tm, 1, D_MODEL), lambda i: (i, 0, 0)), row(ROUTE_LANES),
                   full(1, N_EXPERTS)],
        out_shape=[jax.ShapeDtypeStruct((n, D_MODEL), F32), jax.ShapeDtypeStruct((n, 1, D_MODEL), F32),
                   jax.ShapeDtypeStruct((n, ROUTE_LANES), F32), jax.ShapeDtypeStruct((1, N_EXPERTS), F32)],
        scratch_shapes=[pltpu.VMEM((1, N_EXPERTS), F32)],
        compiler_params=_params("arbitrary"),
        name="mix",
    )(attn, ug, vsn, x, wsg, bsg, g_att.reshape(1, ATT_WIDTH), g_sgu.reshape(1, SGU_WIDTH), w_out_bf,
      gt, sc, sh, g_ffn.reshape(1, D_MODEL), w_router, b_router.reshape(1, N_EXPERTS), cnt_in)


def _dispatch_body(cnt_ref, pcnt_ref, pstart_ref, dest_ref, h2_ref, zero_ref, xs_ref, sem, tile_sem):
    step = pl.program_id(0)
    n_assign = DISPATCH_TOKENS * TOP_K

    def row_copy(a):
        tok = step * DISPATCH_TOKENS + a // TOP_K
        return pltpu.make_async_copy(h2_ref.at[pl.ds(tok, 1)], xs_ref.at[pl.ds(dest_ref[a], 1)], sem)

    def start(a, carry):
        row_copy(a).start()
        return carry

    def wait(a, carry):
        row_copy(a).wait()
        return carry

    lax.fori_loop(0, n_assign, start, 0)
    lax.fori_loop(0, n_assign, wait, 0)

    @pl.when(step == 0)
    def _pad():
        last = N_EXPERTS - 1
        n_used = (pstart_ref[last] + pcnt_ref[last]) // MOE_TILE
        for phase in (True, False):
            def per_expert(e, carry):
                def per_row(rw, c2):
                    cp = pltpu.make_async_copy(zero_ref.at[pl.ds(0, 1)],
                                               xs_ref.at[pl.ds(pstart_ref[e] + rw, 1)], sem)
                    cp.start() if phase else cp.wait()
                    return c2
                return lax.fori_loop(cnt_ref[e], pcnt_ref[e], per_row, carry)
            lax.fori_loop(0, N_EXPERTS, per_expert, 0)

            def per_tile(t, carry):
                cp = pltpu.make_async_copy(zero_ref, xs_ref.at[pl.ds(t * MOE_TILE, MOE_TILE)], tile_sem)
                cp.start() if phase else cp.wait()
                return carry
            lax.fori_loop(n_used, xs_ref.shape[0] // MOE_TILE, per_tile, 0)


def _dispatch(h2, dest_flat, cnt, pcnt, pstart, n_slot):
    n = h2.shape[0]
    n_assign = DISPATCH_TOKENS * TOP_K
    grid_spec = pltpu.PrefetchScalarGridSpec(
        num_scalar_prefetch=3,
        grid=(n // DISPATCH_TOKENS,),
        in_specs=[pl.BlockSpec((n_assign,), lambda s, *_: (s,), memory_space=pltpu.SMEM),
                  pl.BlockSpec(memory_space=pl.ANY),
                  pl.BlockSpec(memory_space=pl.ANY)],
        out_specs=pl.BlockSpec(memory_space=pl.ANY),
        scratch_shapes=[pltpu.SemaphoreType.DMA(()), pltpu.SemaphoreType.DMA(())])
    return pl.pallas_call(
        _dispatch_body,
        grid_spec=grid_spec,
        out_shape=jax.ShapeDtypeStruct((n_slot, 1, D_MODEL), F32),
        compiler_params=_params("arbitrary"),
        name="dispatch",
    )(cnt, pcnt, pstart, dest_flat, h2, jnp.zeros((MOE_TILE, 1, D_MODEL), F32))


def _experts_body(tile_e, n_used, xs_ref, wgu_ref, bgu_ref, wd_ref, bd_ref, o_ref, wgu_bf, wd_bf):
    t = pl.program_id(0)
    e = tile_e[t]
    prev = tile_e[jnp.maximum(t - 1, 0)]

    @pl.when((t == 0) | (e != prev))
    def _():
        wgu_bf[...] = wgu_ref[0].astype(BF16)
        wd_bf[...] = wd_ref[0].astype(BF16)

    @pl.when(t < n_used[0])
    def _():
        x = xs_ref[...].reshape(MOE_TILE, D_MODEL)
        gu = _dot(x.astype(BF16), wgu_bf[...]) + bgu_ref[0]
        g = jnp.minimum(gu[:, :D_FF], SWIGLU_LIMIT)
        lin = jnp.clip(gu[:, D_FF:], -SWIGLU_LIMIT, SWIGLU_LIMIT)
        act = g * (1.0 / (1.0 + jnp.exp(-SWIGLU_ALPHA * g))) * (lin + 1.0)
        out = _dot(act.astype(BF16), wd_bf[...]) + bd_ref[0]
        o_ref[...] = out.reshape(MOE_TILE, 1, D_MODEL)

    @pl.when(t >= n_used[0])
    def _():
        o_ref[...] = jnp.zeros(o_ref.shape, F32)


def _experts(xs, tile_e, n_used, w_gu, b_gu, w_down, b_down):
    n_slot = xs.shape[0]
    n_tiles = n_slot // MOE_TILE
    by_e = lambda *blk: pl.BlockSpec((1,) + blk, lambda t, te, nu: (te[t],) + (0,) * len(blk))
    grid_spec = pltpu.PrefetchScalarGridSpec(
        num_scalar_prefetch=2,
        grid=(n_tiles,),
        in_specs=[pl.BlockSpec((MOE_TILE, 1, D_MODEL), lambda t, te, nu: (jnp.minimum(t, nu[0] - 1), 0, 0)),
                  by_e(D_MODEL, 2 * D_FF), by_e(1, 2 * D_FF), by_e(D_FF, D_MODEL), by_e(1, D_MODEL)],
        out_specs=pl.BlockSpec((MOE_TILE, 1, D_MODEL), lambda t, te, nu: (t, 0, 0)),
        scratch_shapes=[pltpu.VMEM((D_MODEL, 2 * D_FF), BF16), pltpu.VMEM((D_FF, D_MODEL), BF16)])
    return pl.pallas_call(
        _experts_body,
        grid_spec=grid_spec,
        out_shape=jax.ShapeDtypeStruct((n_slot, 1, D_MODEL), F32),
        compiler_params=_params("arbitrary"),
        name="experts",
    )(tile_e, n_used, xs, w_gu, b_gu.reshape(N_EXPERTS, 1, 2 * D_FF), w_down, b_down.reshape(N_EXPERTS, 1, D_MODEL))


def _combine_body(dest_ref, ys_ref, x1_ref, route_ref, gt_ref, gfin_ref, o_ref, buf, sem):
    tc = x1_ref.shape[0]

    def row_copy(a):
        return pltpu.make_async_copy(ys_ref.at[pl.ds(dest_ref[a], 1)],
                                     buf.at[a % TOP_K, pl.ds(a // TOP_K, 1)], sem)

    def start(a, carry):
        row_copy(a).start()
        return carry

    def wait(a, carry):
        row_copy(a).wait()
        return carry

    lax.fori_loop(0, tc * TOP_K, start, 0)
    lax.fori_loop(0, tc * TOP_K, wait, 0)
    route = route_ref[...]
    moe = route[:, TOP_K:TOP_K + 1] * buf[0].reshape(tc, D_MODEL)
    for t in range(1, TOP_K):
        moe = moe + route[:, TOP_K + t:TOP_K + t + 1] * buf[t].reshape(tc, D_MODEL)
    o_ref[...] = _rms(x1_ref[...] + gt_ref[...] * moe, gfin_ref[...])


def _combine(ys, dest_flat, x1, route, gt, g_final):
    n = x1.shape[0]
    tc = COMBINE_TOKENS
    mod_spec = (pl.BlockSpec((1, D_MODEL), lambda i: (0, 0)) if gt.shape[0] == 1
                else pl.BlockSpec((tc, D_MODEL), lambda i: (i, 0)))
    return pl.pallas_call(
        _combine_body,
        grid=(n // tc,),
        in_specs=[pl.BlockSpec((tc * TOP_K,), lambda i: (i,), memory_space=pltpu.SMEM),
                  pl.BlockSpec(memory_space=pl.ANY),
                  pl.BlockSpec((tc, D_MODEL), lambda i: (i, 0)),
                  pl.BlockSpec((tc, ROUTE_LANES), lambda i: (i, 0)),
                  mod_spec,
                  pl.BlockSpec((1, D_MODEL), lambda i: (0, 0))],
        out_specs=pl.BlockSpec((tc, D_MODEL), lambda i: (i, 0)),
        out_shape=jax.ShapeDtypeStruct((n, D_MODEL), F32),
        scratch_shapes=[pltpu.VMEM((TOP_K, tc, 1, D_MODEL), F32), pltpu.SemaphoreType.DMA(())],
        compiler_params=_params("arbitrary"),
        name="combine",
    )(dest_flat, ys, x1, route, gt, g_final.reshape(1, D_MODEL))


def _pages_keys_minor(cache):
    return cache.transpose(0, 2, 3, 1).reshape(cache.shape[0], ATT_WIDTH, PAGE_SIZE)


def _group_mean_matrix():
    g = np.arange(SGU_WIDTH) // HEAD_DIM
    return jnp.asarray((g[:, None] == g[None, :]).astype(np.float32) / HEAD_DIM, BF16)


def _key_offset_bias():
    off = np.arange(MOBA_BLOCK, dtype=np.float32)[None, :, None]
    return jnp.asarray(np.broadcast_to(np.asarray(_SLOPES, np.float32)[:, None, None] * off,
                                       (N_ATT_HEADS, MOBA_BLOCK, MOBA_BLOCK)))


def kernel(x_prompt, x_sample, cache_k, cache_v, page_table, c_prompt, c_sample, g_mix, w_ada, b_ada, w_in,
           g_sgu, w_sgu, b_sgu, g_out_att, g_out_sgu, w_out, g_ffn, w_router, b_router, w_gu, b_gu, w_down,
           b_down, g_final):
    depth = w_in.shape[0]
    batch, seq, _ = x_prompt.shape
    dec_batch, dec_seq, _ = x_sample.shape
    assert depth == 1, "the final norm is fused into the last stage of the single layer"
    assert batch == 1 and seq % MOBA_BLOCK == 0
    n_dec = dec_batch * dec_seq
    assert n_dec % CHUNK == 0 and CHUNK % dec_seq == 0 and dec_seq <= MOBA_BLOCK
    n_tok = seq + n_dec
    n_pool = cache_k.shape[1]

    bd = _group_mean_matrix()
    kp = _key_offset_bias()
    xp = x_prompt.reshape(seq, D_MODEL)
    xs = x_sample.reshape(n_dec, D_MODEL)
    c_all = jnp.concatenate([c_prompt, c_sample], axis=0)
    c_rows = -(-c_all.shape[0] // 8) * 8
    c_all = jnp.pad(c_all, ((0, c_rows - c_all.shape[0]), (0, 0)))
    k_rows, v_rows, ks_rows, vs_rows, sgu_rows = [], [], [], [], []

    for l in range(depth):
        mod = _ada(c_all, w_ada[l], b_ada[l])
        mod_p = [mod[0:1, t * D_MODEL:(t + 1) * D_MODEL] for t in range(6)]
        mod_rows = jnp.broadcast_to(mod[1:1 + dec_batch, None, :], (dec_batch, dec_seq, 6 * D_MODEL))
        mod_rows = mod_rows.reshape(n_dec, 6 * D_MODEL)
        mod_s = [mod_rows[:, t * D_MODEL:(t + 1) * D_MODEL] for t in range(6)]
        w_in_bf = w_in[l].astype(BF16)
        w_out_bf = w_out[l].astype(BF16)
        gs_flat = g_sgu[l].reshape(SGU_WIDTH)

        q, k, v, ug, vsn, kb, vt, kmean = _inproj(xp, g_mix[l], mod_p[1], mod_p[0], w_in_bf, gs_flat, bd, True)
        attn = _moba_prompt(q, kb, vt, kmean.reshape(seq // MOBA_BLOCK, ATT_WIDTH), kp)
        bias_p = jnp.repeat(b_sgu[l].T, HEAD_DIM, axis=1)
        x1_p, h2_p, route_p, cnt_p = _mix(attn, ug, vsn, xp, w_sgu[l], bias_p, g_out_att[l], g_out_sgu[l],
                                          w_out_bf, mod_p[2], mod_p[4], mod_p[3], g_ffn[l], w_router[l],
                                          b_router[l], jnp.zeros((1, N_EXPERTS), F32), MOBA_BLOCK)
        k_rows.append(k)
        v_rows.append(v)

        qs, ks, vs, ugs, vsns = _inproj(xs, g_mix[l], mod_s[1], mod_s[0], w_in_bf, gs_flat, bd, False)
        attn_s = _moba_decode(qs.reshape(dec_batch, dec_seq, ATT_WIDTH), ks.reshape(dec_batch, dec_seq, ATT_WIDTH),
                              vs.reshape(dec_batch, dec_seq, ATT_WIDTH),
                              _pages_keys_minor(cache_k[l]), _pages_keys_minor(cache_v[l]), page_table)
        eye = jnp.eye(CHUNK // dec_seq, dtype=F32)
        w_s = jax.vmap(lambda w: jnp.kron(eye, w))(w_sgu[l][:, :dec_seq, :dec_seq])
        bias_s = jnp.tile(jnp.repeat(b_sgu[l].T[:dec_seq], HEAD_DIM, axis=1), (CHUNK // dec_seq, 1))
        x1_s, h2_s, route_s, cnt = _mix(attn_s.reshape(n_dec, ATT_WIDTH), ugs, vsns, xs, w_s, bias_s, g_out_att[l],
                                        g_out_sgu[l], w_out_bf, mod_s[2], mod_s[4], mod_s[3], g_ffn[l],
                                        w_router[l], b_router[l], cnt_p, CHUNK)
        ks_rows.append(ks)
        vs_rows.append(vs)
        sgu_rows.append(vsns)

        route = jnp.concatenate([route_p, route_s], axis=0)
        e_idx = route[:, :TOP_K].astype(jnp.int32)
        rank = route[:, 2 * TOP_K:3 * TOP_K].astype(jnp.int32)
        cnt_i = cnt.reshape(N_EXPERTS).astype(jnp.int32)
        pcnt = (cnt_i + MOE_TILE - 1) // MOE_TILE * MOE_TILE
        pend = jnp.cumsum(pcnt)
        pstart = pend - pcnt
        dest = (pstart[e_idx] + rank).reshape(-1)
        n_tiles = -(-(n_tok * TOP_K) // MOE_TILE) + N_EXPERTS
        tile_start = jnp.arange(n_tiles, dtype=jnp.int32) * MOE_TILE
        tile_e = jnp.minimum(jnp.sum((pend[None, :] <= tile_start[:, None]).astype(jnp.int32), axis=1), N_EXPERTS - 1)
        n_used = (pend[-1:] // MOE_TILE).astype(jnp.int32)

        h2 = jnp.concatenate([h2_p, h2_s], axis=0)
        slots = _dispatch(h2, dest, cnt_i, pcnt, pstart, n_tiles * MOE_TILE)
        ys = _experts(slots, tile_e, n_used, w_gu[l], b_gu[l], w_down[l], b_down[l])
        xp = _combine(ys, dest[:seq * TOP_K], x1_p, route_p, mod_p[5], g_final)
        xs = _combine(ys, dest[seq * TOP_K:], x1_s, route_s, mod_s[5], g_final)

    hd = (N_ATT_HEADS, HEAD_DIM)
    k_p = jnp.stack(k_rows).reshape(depth, batch, seq // PAGE_SIZE, PAGE_SIZE, *hd)
    v_p = jnp.stack(v_rows).reshape(depth, batch, seq // PAGE_SIZE, PAGE_SIZE, *hd)
    k_s = jnp.stack(ks_rows).reshape(depth, dec_batch, dec_seq, *hd)
    v_s = jnp.stack(vs_rows).reshape(depth, dec_batch, dec_seq, *hd)
    sgu_v = jnp.stack(sgu_rows).reshape(depth, dec_batch, dec_seq, N_SGU_GROUPS, HEAD_DIM)
    return (xp.reshape(batch, seq, D_MODEL), xs.reshape(dec_batch, dec_seq, D_MODEL), k_p, v_p, k_s, v_s, sgu_v)
```

```python
import functools

import numpy as np
import jax
import jax.numpy as jnp
from jax import lax
from jax.experimental import pallas as pl
from jax.experimental.pallas import tpu as pltpu

F32 = jnp.float32
BF16 = jnp.bfloat16

D_MODEL = 1024
HEAD_DIM = 64
N_ATT_HEADS = 8
N_SGU_GROUPS = 8
ATT_WIDTH = N_ATT_HEADS * HEAD_DIM
SGU_WIDTH = N_SGU_GROUPS * HEAD_DIM
IN_WIDTH = 3 * ATT_WIDTH + 2 * SGU_WIDTH
CHUNK = 128
MOBA_BLOCK = 256
MOBA_TOPK = 3
PAGE_SIZE = 128
N_EXPERTS = 32
TOP_K = 4
D_FF = D_MODEL
SWIGLU_LIMIT = 7.0
SWIGLU_ALPHA = 1.702
EPS = 1e-6
MASK_VALUE = -1e30

HEADS_PER_GROUP = 4
GROUP_WIDTH = HEADS_PER_GROUP * HEAD_DIM
ROUTE_LANES = 128
MOE_TILE = 256
DMA_UNROLL = 8
COMBINE_TOKENS = 128
OWN_ROWS = 16
VMEM_LIMIT = 56 * 1024 * 1024

_SLOPES = [2.0 ** (-8.0 * (h + 1) / N_ATT_HEADS) for h in range(N_ATT_HEADS)]


def _params(*sem):
    return pltpu.CompilerParams(dimension_semantics=sem, vmem_limit_bytes=VMEM_LIMIT)


def _split(a):
    hi = a.astype(BF16)
    lo = (a - hi.astype(F32)).astype(BF16)
    return hi, lo


def _dot(a, b):
    return jnp.dot(a, b, preferred_element_type=F32)


def _dot_nt(a, b):
    return lax.dot_general(a, b, (((1,), (1,)), ((), ())), preferred_element_type=F32)


def _dot3(a, b, dot=_dot):
    a_hi, a_lo = _split(a)
    b_hi, b_lo = _split(b)
    return dot(a_hi, b_hi) + (dot(a_hi, b_lo) + dot(a_lo, b_hi))


def _rms(x, g):
    return x * lax.rsqrt(jnp.mean(x * x, axis=-1, keepdims=True) + EPS) * g


def _gelu(x):
    return x * (0.5 * (1.0 + jnp.tanh(0.7978845608028654 * (x + 0.044715 * (x * x * x)))))


def _ada_body(c_ref, w_ref, b_ref, o_ref):
    c = c_ref[...]
    s = c * (1.0 / (1.0 + jnp.exp(-c)))
    o_ref[...] = _dot3(s, w_ref[...]) + b_ref[...]


def _ada(c, w, b):
    rows = c.shape[0]
    n = w.shape[1]
    tn = 1536
    return pl.pallas_call(
        _ada_body,
        grid=(n // tn,),
        in_specs=[pl.BlockSpec((rows, D_MODEL), lambda j: (0, 0)),
                  pl.BlockSpec((D_MODEL, tn), lambda j: (0, j)),
                  pl.BlockSpec((1, tn), lambda j: (0, j))],
        out_specs=pl.BlockSpec((rows, tn), lambda j: (0, j)),
        out_shape=jax.ShapeDtypeStruct((rows, n), F32),
        compiler_params=_params("arbitrary"),
        name="ada",
    )(c, w, b.reshape(1, n))


def _inproj_body(x_ref, g_ref, sc_ref, sh_ref, w_ref, gs_ref, bd_ref, q_ref, k_ref, v_ref, ug_ref, vsn_ref,
                 *prompt_refs):
    h = _rms(x_ref[...], g_ref[...]) * (1.0 + sc_ref[...]) + sh_ref[...]
    z = _dot(h.astype(BF16), w_ref[...])
    a = ATT_WIDTH
    q_ref[...] = z[:, :a]
    k = z[:, a:2 * a]
    v = z[:, 2 * a:3 * a]
    k_ref[...] = k
    v_ref[...] = v
    ug_ref[...] = _gelu(z[:, 3 * a:3 * a + SGU_WIDTH])
    vg = _gelu(z[:, 3 * a + SGU_WIDTH:])
    sq_hi, sq_lo = _split(vg * vg)
    gmean = _dot(sq_hi, bd_ref[...]) + _dot(sq_lo, bd_ref[...])
    vsn_ref[...] = vg * lax.rsqrt(gmean + EPS) * gs_ref[...]
    if prompt_refs:
        kb_ref, vt_ref, km_ref = prompt_refs
        kb_ref[...] = k.astype(BF16)
        vt_ref[...] = v.T.astype(BF16)
        km_ref[0] = jnp.mean(k, axis=0, keepdims=True)


def _inproj(x, g_mix, sc, sh, w_in_bf, g_sgu, bd, prompt):
    n = x.shape[0]
    tm = MOBA_BLOCK if prompt else n
    mod_rows = sc.shape[0]
    mod_spec = (pl.BlockSpec((1, D_MODEL), lambda i: (0, 0)) if mod_rows == 1
                else pl.BlockSpec((tm, D_MODEL), lambda i: (i, 0)))
    row = lambda w: pl.BlockSpec((tm, w), lambda i: (i, 0))
    out_specs = [row(ATT_WIDTH)] * 3 + [row(SGU_WIDTH)] * 2
    out_shape = [jax.ShapeDtypeStruct((n, ATT_WIDTH), F32)] * 3 + [jax.ShapeDtypeStruct((n, SGU_WIDTH), F32)] * 2
    if prompt:
        out_specs += [row(ATT_WIDTH), pl.BlockSpec((ATT_WIDTH, tm), lambda i: (0, i)),
                      pl.BlockSpec((1, 1, ATT_WIDTH), lambda i: (i, 0, 0))]
        out_shape += [jax.ShapeDtypeStruct((n, ATT_WIDTH), BF16), jax.ShapeDtypeStruct((ATT_WIDTH, n), BF16),
                      jax.ShapeDtypeStruct((n // tm, 1, ATT_WIDTH), F32)]
    return pl.pallas_call(
        _inproj_body,
        grid=(n // tm,),
        in_specs=[row(D_MODEL),
                  pl.BlockSpec((1, D_MODEL), lambda i: (0, 0)),
                  mod_spec, mod_spec,
                  pl.BlockSpec((D_MODEL, IN_WIDTH), lambda i: (0, 0)),
                  pl.BlockSpec((1, SGU_WIDTH), lambda i: (0, 0)),
                  pl.BlockSpec((SGU_WIDTH, SGU_WIDTH), lambda i: (0, 0))],
        out_specs=out_specs,
        out_shape=out_shape,
        compiler_params=_params("arbitrary"),
        name="inproj_prompt" if prompt else "inproj_sample",
    )(x, g_mix.reshape(1, D_MODEL), sc, sh, w_in_bf, g_sgu.reshape(1, SGU_WIDTH), bd)


def _select_bias(gate, n_past, axis):
    blk = lax.broadcasted_iota(jnp.int32, gate.shape, axis).astype(F32)
    past = blk < n_past
    gate = jnp.where(past, gate, -jnp.inf)
    bias = jnp.full(gate.shape, MASK_VALUE, F32)
    for _ in range(MOBA_TOPK):
        mx = jnp.max(gate, axis=axis, keepdims=True)
        first = jnp.min(jnp.where(gate == mx, blk, float(gate.shape[axis])), axis=axis, keepdims=True)
        hit = blk == first
        bias = jnp.where(hit, 0.0, bias)
        gate = jnp.where(hit, -jnp.inf, gate)
    return jnp.where(past, bias, MASK_VALUE)


def _moba_body(i_of, j_of, q_ref, kb_ref, vt_ref, km_ref, kp_ref, o_ref, qm_ref, sel_ref, m_ref, l_ref, acc_ref):
    s = pl.program_id(0)
    i = i_of[s]
    j = j_of[s]
    bq = MOBA_BLOCK

    def sweep(diag):
        lane_q = lax.broadcasted_iota(jnp.int32, (1, bq), 1).astype(F32)
        dist0 = ((i - j) * MOBA_BLOCK).astype(F32) + lane_q
        if diag:
            key = lax.broadcasted_iota(jnp.int32, (bq, bq), 0)
            qry = lax.broadcasted_iota(jnp.int32, (bq, bq), 1)
            causal = jnp.where(key <= qry, 0.0, MASK_VALUE)
        for h in range(N_ATT_HEADS):
            g = h // HEADS_PER_GROUP
            k4 = kb_ref[:, g * GROUP_WIDTH:(g + 1) * GROUP_WIDTH]
            s0 = _dot(k4, qm_ref[h]) + kp_ref[h]
            brow = -_SLOPES[h] * dist0
            if diag:
                s0 = s0 + causal
                m_old = jnp.full((1, bq), MASK_VALUE, F32)
                l_old = jnp.zeros((1, bq), F32)
            else:
                brow = brow + sel_ref[h, pl.ds(j, 1), :]
                m_old = m_ref[h:h + 1, :]
                l_old = l_ref[h:h + 1, :]
            m_new = jnp.maximum(m_old, jnp.max(s0, axis=0, keepdims=True) + brow)
            p = jnp.exp(s0 - (m_new - brow))
            alpha = jnp.exp(m_old - m_new)
            l_ref[h:h + 1, :] = alpha * l_old + jnp.sum(p, axis=0, keepdims=True)
            m_ref[h:h + 1, :] = m_new
            rows = slice(h * HEAD_DIM, (h + 1) * HEAD_DIM)
            pv = _dot(vt_ref[rows, :], p.astype(BF16))
            if diag:
                acc_ref[rows, :] = pv
            else:
                acc_ref[rows, :] = alpha * acc_ref[rows, :] + pv

    @pl.when(j == i)
    def _first():
        qt = (q_ref[...] * (HEAD_DIM ** -0.5)).T
        row = lax.broadcasted_iota(jnp.int32, (GROUP_WIDTH, bq), 0) // HEAD_DIM
        for h in range(N_ATT_HEADS):
            g = h // HEADS_PER_GROUP
            qm = jnp.where(row == h % HEADS_PER_GROUP, qt[g * GROUP_WIDTH:(g + 1) * GROUP_WIDTH, :], 0.0)
            qm_ref[h] = qm.astype(BF16)
            gate = _dot3(km_ref[:, g * GROUP_WIDTH:(g + 1) * GROUP_WIDTH], qm)
            sel_ref[h] = _select_bias(gate, i.astype(F32), 0)
        sweep(True)

    @pl.when(j != i)
    def _past():
        sweep(False)

    @pl.when((j == i - 1) | (i == 0))
    def _last():
        for h in range(N_ATT_HEADS):
            rows = slice(h * HEAD_DIM, (h + 1) * HEAD_DIM)
            acc_ref[rows, :] = acc_ref[rows, :] / l_ref[h:h + 1, :]
        o_ref[...] = acc_ref[...].T


def _moba_prompt(q, kb, vt, kmean, kp):
    t = q.shape[0]
    nb = t // MOBA_BLOCK
    i_of = np.concatenate([np.full(i + 1, i, np.int32) for i in range(nb)])
    j_of = np.concatenate([np.concatenate([[i], np.arange(i)]).astype(np.int32) for i in range(nb)])
    grid_spec = pltpu.PrefetchScalarGridSpec(
        num_scalar_prefetch=2,
        grid=(len(i_of),),
        in_specs=[pl.BlockSpec((MOBA_BLOCK, ATT_WIDTH), lambda s, io, jo: (io[s], 0)),
                  pl.BlockSpec((MOBA_BLOCK, ATT_WIDTH), lambda s, io, jo: (jo[s], 0)),
                  pl.BlockSpec((ATT_WIDTH, MOBA_BLOCK), lambda s, io, jo: (0, jo[s])),
                  pl.BlockSpec((nb, ATT_WIDTH), lambda s, io, jo: (0, 0)),
                  pl.BlockSpec((N_ATT_HEADS, MOBA_BLOCK, MOBA_BLOCK), lambda s, io, jo: (0, 0, 0))],
        out_specs=pl.BlockSpec((MOBA_BLOCK, ATT_WIDTH), lambda s, io, jo: (io[s], 0)),
        scratch_shapes=[pltpu.VMEM((N_ATT_HEADS, GROUP_WIDTH, MOBA_BLOCK), BF16),
                        pltpu.VMEM((N_ATT_HEADS, nb, MOBA_BLOCK), F32),
                        pltpu.VMEM((N_ATT_HEADS, MOBA_BLOCK), F32),
                        pltpu.VMEM((N_ATT_HEADS, MOBA_BLOCK), F32),
                        pltpu.VMEM((ATT_WIDTH, MOBA_BLOCK), F32)])
    return pl.pallas_call(
        _moba_body,
        grid_spec=grid_spec,
        out_shape=jax.ShapeDtypeStruct((t, ATT_WIDTH), F32),
        compiler_params=_params("arbitrary"),
        name="moba_prompt",
    )(jnp.asarray(i_of), jnp.asarray(j_of), q, kb, vt, kmean, kp)


def _decode_body(pt_ref, q_ref, kn_ref, vn_ref, *refs, pages, n_blocks, past_len, n_q):
    k_pages = refs[:pages]
    v_pages = refs[pages:2 * pages]
    o_ref = refs[2 * pages]
    km_ref, op_ref, m_ref, l_ref = refs[2 * pages + 1:]
    step = pl.program_id(1)
    rows = N_ATT_HEADS * n_q
    head_of_row = lax.broadcasted_iota(jnp.int32, (rows, ATT_WIDTH), 0) % N_ATT_HEADS
    head_of_lane = lax.broadcasted_iota(jnp.int32, (rows, ATT_WIDTH), 1) // HEAD_DIM
    own_head = head_of_row == head_of_lane
    rid = lax.broadcasted_iota(jnp.int32, (rows, 1), 0)
    slope = jnp.zeros((rows, 1), F32)
    for h in range(N_ATT_HEADS):
        slope = jnp.where(rid % N_ATT_HEADS == h, _SLOPES[h], slope)
    qidx = rid // N_ATT_HEADS
    qpos = (past_len + qidx).astype(F32)
    blk_lane = lax.broadcasted_iota(jnp.int32, (rows, 128), 1)

    @pl.when(step == 0)
    def _init():
        m_ref[...] = jnp.full(m_ref.shape, MASK_VALUE, F32)
        l_ref[...] = jnp.zeros(l_ref.shape, F32)
        km_ref[...] = jnp.zeros(km_ref.shape, F32)

    qm = jnp.where(own_head, q_ref[0] * (HEAD_DIM ** -0.5), 0.0)
    qm_bf = qm.astype(BF16)
    km_lane = lax.broadcasted_iota(jnp.int32, (ATT_WIDTH, 128), 1)
    for b in range(pages // 2):
        blk = step * (pages // 2) + b
        kt = jnp.concatenate([k_pages[2 * b][0], k_pages[2 * b + 1][0]], axis=1)
        vt = jnp.concatenate([v_pages[2 * b][0], v_pages[2 * b + 1][0]], axis=1)
        km_ref[...] = jnp.where(km_lane == blk, jnp.sum(kt, axis=1, keepdims=True) * (1.0 / MOBA_BLOCK), km_ref[...])
        kpos = (blk * MOBA_BLOCK + lax.broadcasted_iota(jnp.int32, (1, MOBA_BLOCK), 1)).astype(F32)
        s = _dot(qm_bf, kt.astype(BF16)) - slope * (qpos - kpos)
        m = jnp.max(s, axis=-1, keepdims=True)
        p = jnp.exp(s - m)
        m_ref[...] = jnp.where(blk_lane == blk, m, m_ref[...])
        l_ref[...] = jnp.where(blk_lane == blk, jnp.sum(p, axis=-1, keepdims=True), l_ref[...])
        op_ref[blk] = jnp.where(own_head, _dot_nt(p.astype(BF16), vt.astype(BF16)), 0.0)

    @pl.when(step == pl.num_programs(1) - 1)
    def _merge():
        gate = _dot3(qm, km_ref[...])
        sel = _select_bias(gate, float(n_blocks), 1)
        kn = kn_ref[0]
        vn = vn_ref[0]
        kidx = lax.broadcasted_iota(jnp.int32, (rows, kn.shape[0]), 1)
        dist = (qidx - kidx).astype(F32)
        s_own = _dot_nt(qm_bf, kn.astype(BF16)) - slope * dist
        s_own = jnp.where(dist >= 0, s_own, MASK_VALUE)
        m_blk = m_ref[...] + sel
        m_all = jnp.maximum(jnp.max(m_blk, axis=-1, keepdims=True), jnp.max(s_own, axis=-1, keepdims=True))
        w = jnp.exp(m_blk - m_all)
        p_own = jnp.exp(s_own - m_all)
        den = jnp.sum(w * l_ref[...], axis=-1, keepdims=True) + jnp.sum(p_own, axis=-1, keepdims=True)
        acc = jnp.where(own_head, _dot(p_own.astype(BF16), vn.astype(BF16)), 0.0)
        for b in range(n_blocks):
            acc = acc + w[:, b:b + 1] * op_ref[b]
        acc = acc / den
        for qi in range(n_q):
            o_ref[0, qi:qi + 1, :] = jnp.sum(acc[qi * N_ATT_HEADS:(qi + 1) * N_ATT_HEADS], axis=0, keepdims=True)


def _moba_decode(q, k_new, v_new, cache_k, cache_v, page_table):
    b, n_q, _ = q.shape
    n_pages = page_table.shape[1]
    past_len = n_pages * PAGE_SIZE
    n_blocks = past_len // MOBA_BLOCK
    assert n_blocks <= 128 and n_pages % 2 == 0
    pages = 8 if n_pages % 8 == 0 else 2
    rows = N_ATT_HEADS * n_q
    q_rows = jnp.repeat(q, N_ATT_HEADS, axis=1)
    pad = ((0, 0), (0, OWN_ROWS - n_q), (0, 0))
    by_seq = lambda r: pl.BlockSpec((1, r, ATT_WIDTH), lambda bi, st, pt: (bi, 0, 0))

    def page_spec(t):
        return pl.BlockSpec((1, ATT_WIDTH, PAGE_SIZE), lambda bi, st, pt: (pt[bi, st * pages + t], 0, 0))

    grid_spec = pltpu.PrefetchScalarGridSpec(
        num_scalar_prefetch=1,
        grid=(b, n_pages // pages),
        in_specs=[by_seq(rows), by_seq(OWN_ROWS), by_seq(OWN_ROWS)] + [page_spec(t) for t in range(pages)] * 2,
        out_specs=by_seq(n_q),
        scratch_shapes=[pltpu.VMEM((ATT_WIDTH, 128), F32),
                        pltpu.VMEM((n_blocks, rows, ATT_WIDTH), F32),
                        pltpu.VMEM((rows, 128), F32),
                        pltpu.VMEM((rows, 128), F32)])
    return pl.pallas_call(
        functools.partial(_decode_body, pages=pages, n_blocks=n_blocks, past_len=past_len, n_q=n_q),
        grid_spec=grid_spec,
        out_shape=jax.ShapeDtypeStruct((b, n_q, ATT_WIDTH), F32),
        compiler_params=_params("arbitrary", "arbitrary"),
        name="moba_decode",
    )(page_table, q_rows, jnp.pad(k_new, pad), jnp.pad(v_new, pad), *([cache_k] * pages), *([cache_v] * pages))


def _mix_body(attn_ref, ug_ref, vsn_ref, x_ref, wsg_ref, bsg_ref, ga_ref, gs_ref, wo_ref, gt_ref, sc_ref, sh_ref,
              gf_ref, wr_ref, br_ref, cin_ref, x1_ref, h2_ref, route_ref, cnt_ref, carry_ref):
    tm = x_ref.shape[0]

    @pl.when(pl.program_id(0) == 0)
    def _():
        carry_ref[...] = cin_ref[...]

    r = lax.broadcasted_iota(jnp.int32, (CHUNK, CHUNK), 0)
    c = lax.broadcasted_iota(jnp.int32, (CHUNK, CHUNK), 1)
    wcat = jnp.concatenate([jnp.where(c <= r, wsg_ref[g], 0.0).astype(BF16) for g in range(N_SGU_GROUPS)], axis=1)
    lane_group = lax.broadcasted_iota(jnp.int32, (CHUNK, SGU_WIDTH), 1) // HEAD_DIM
    mixed = []
    for ch in range(tm // CHUNK):
        vs = vsn_ref[ch * CHUNK:(ch + 1) * CHUNK, :]
        stack = jnp.concatenate([jnp.where(lane_group == g, vs, 0.0).astype(BF16) for g in range(N_SGU_GROUPS)],
                                axis=0)
        mixed.append(_dot(wcat, stack) + bsg_ref[...])
    sgu = ug_ref[...] * (mixed[0] if len(mixed) == 1 else jnp.concatenate(mixed, axis=0))
    merged = jnp.concatenate([_rms(attn_ref[...], ga_ref[...]), _rms(sgu, gs_ref[...])], axis=1)
    x1 = x_ref[...] + gt_ref[...] * _dot(merged.astype(BF16), wo_ref[...])
    x1_ref[...] = x1
    h2 = _rms(x1, gf_ref[...]) * (1.0 + sc_ref[...]) + sh_ref[...]
    h2_ref[...] = h2.reshape(tm, 1, D_MODEL)

    logits = _dot3(h2, wr_ref[...]) + br_ref[...]
    lane_f = lax.broadcasted_iota(jnp.int32, (tm, N_EXPERTS), 1).astype(F32)
    vals, hits = [], []
    work = logits
    for _ in range(TOP_K):
        mx = jnp.max(work, axis=-1, keepdims=True)
        first = jnp.min(jnp.where(work == mx, lane_f, float(N_EXPERTS)), axis=-1, keepdims=True)
        hit = lane_f == first
        vals.append(mx)
        hits.append(hit)
        work = jnp.where(hit, -jnp.inf, work)
    ex = [jnp.exp(v - vals[0]) for v in vals]
    den = ex[0] + ex[1] + ex[2] + ex[3]
    chosen = jnp.zeros((tm, N_EXPERTS), F32)
    for hit in hits:
        chosen = jnp.where(hit, 1.0, chosen)
    rr = lax.broadcasted_iota(jnp.int32, (tm, tm), 0)
    cc = lax.broadcasted_iota(jnp.int32, (tm, tm), 1)
    before = _dot(jnp.where(cc < rr, 1.0, 0.0).astype(BF16), chosen.astype(BF16)) + carry_ref[...]
    rl = lax.broadcasted_iota(jnp.int32, (tm, ROUTE_LANES), 1)
    route = jnp.zeros((tm, ROUTE_LANES), F32)
    for t in range(TOP_K):
        idx = jnp.sum(jnp.where(hits[t], lane_f, 0.0), axis=-1, keepdims=True)
        rank = jnp.sum(jnp.where(hits[t], before, 0.0), axis=-1, keepdims=True)
        route = jnp.where(rl == t, idx, route)
        route = jnp.where(rl == TOP_K + t, ex[t] / den, route)
        route = jnp.where(rl == 2 * TOP_K + t, rank, route)
    route_ref[...] = route
    carry_ref[...] = carry_ref[...] + jnp.sum(chosen, axis=0, keepdims=True)
    cnt_ref[...] = carry_ref[...]


def _mix(attn, ug, vsn, x, wsg, bsg, g_att, g_sgu, w_out_bf, gt, sc, sh, g_ffn, w_router, b_router, cnt_in, tm):
    n = x.shape[0]
    mod_rows = gt.shape[0]
    mod_spec = (pl.BlockSpec((1, D_MODEL), lambda i: (0, 0)) if mod_rows == 1
                else pl.BlockSpec((tm, D_MODEL), lambda i: (i, 0)))
    row = lambda w: pl.BlockSpec((tm, w), lambda i: (i, 0))
    full = lambda *shape: pl.BlockSpec(shape, lambda i: (0,) * len(shape))
    return pl.pallas_call(
        _mix_body,
        grid=(n // tm,),
        in_specs=[row(ATT_WIDTH), row(SGU_WIDTH), row(SGU_WIDTH), row(D_MODEL),
                  full(N_SGU_GROUPS, CHUNK, CHUNK), full(CHUNK, SGU_WIDTH),
                  full(1, ATT_WIDTH), full(1, SGU_WIDTH), full(D_MODEL, D_MODEL),
                  mod_spec, mod_spec, mod_spec,
                  full(1, D_MODEL), full(D_MODEL, N_EXPERTS), full(1, N_EXPERTS), full(1, N_EXPERTS)],
        out_specs=[row(D_MODEL), pl.BlockSpec((tm, 1, D_MODEL), lambda i: (i, 0, 0)), row(ROUTE_LANES),
                   full(1, N_EXPERTS)],
        out_shape=[jax.ShapeDtypeStruct((n, D_MODEL), F32), jax.ShapeDtypeStruct((n, 1, D_MODEL), F32),
                   jax.ShapeDtypeStruct((n, ROUTE_LANES), F32), jax.ShapeDtypeStruct((1, N_EXPERTS), F32)],
        scratch_shapes=[pltpu.VMEM((1, N_EXPERTS), F32)],
        compiler_params=_params("arbitrary"),
        name="mix",
    )(attn, ug, vsn, x, wsg, bsg, g_att.reshape(1, ATT_WIDTH), g_sgu.reshape(1, SGU_WIDTH), w_out_bf,
      gt, sc, sh, g_ffn.reshape(1, D_MODEL), w_router, b_router.reshape(1, N_EXPERTS), cnt_in)


def _experts_body(tile_e, tile_rows, n_used, src_ref, src_next_ref, dst_ref, h2p_ref, h2s_ref, wgu_ref, bgu_ref,
                  wd_ref, bd_ref, y_ref, xbuf, obuf, gsem, ssem, wgu_bf, wd_bf, *, n_prompt):
    t = pl.program_id(0)
    used = n_used[0]
    slot = t % 2

    def start_gather(src, rows, into):
        def body(r, carry):
            tok = src[r]

            @pl.when((r < rows) & (tok < n_prompt))
            def _():
                pltpu.make_async_copy(h2p_ref.at[pl.ds(tok, 1)], xbuf.at[into, pl.ds(r, 1)], gsem.at[into]).start()

            @pl.when((r < rows) & (tok >= n_prompt))
            def _():
                pltpu.make_async_copy(h2s_ref.at[pl.ds(tok - n_prompt, 1)], xbuf.at[into, pl.ds(r, 1)],
                                      gsem.at[into]).start()
            return carry
        lax.fori_loop(0, MOE_TILE, body, 0, unroll=DMA_UNROLL)

    def wait_gather(rows, into):
        pltpu.make_async_copy(h2p_ref.at[pl.ds(0, rows)], xbuf.at[into, pl.ds(0, rows)], gsem.at[into]).wait()

    def start_scatter(rows, frm):
        def body(r, carry):
            @pl.when(r < rows)
            def _():
                pltpu.make_async_copy(obuf.at[frm, pl.ds(r, 1)], y_ref.at[pl.ds(dst_ref[r], 1)], ssem.at[frm]).start()
            return carry
        lax.fori_loop(0, MOE_TILE, body, 0, unroll=DMA_UNROLL)

    def wait_scatter(rows, frm):
        pltpu.make_async_copy(obuf.at[frm, pl.ds(0, rows)], y_ref.at[pl.ds(0, rows)], ssem.at[frm]).wait()

    @pl.when(t == 0)
    def _():
        xbuf[...] = jnp.zeros(xbuf.shape, F32)
        start_gather(src_ref, tile_rows[0], 0)

    @pl.when(t + 1 < used)
    def _():
        start_gather(src_next_ref, tile_rows[t + 1], 1 - slot)

    e = tile_e[t]
    prev = tile_e[jnp.maximum(t - 1, 0)]

    @pl.when((t == 0) | (e != prev))
    def _():
        wgu_bf[...] = wgu_ref[0].astype(BF16)
        wd_bf[...] = wd_ref[0].astype(BF16)

    @pl.when(t < used)
    def _():
        wait_gather(tile_rows[t], slot)

        @pl.when(t >= 2)
        def _():
            wait_scatter(tile_rows[jnp.maximum(t - 2, 0)], slot)

        x = xbuf[slot].reshape(MOE_TILE, D_MODEL)
        gu = _dot(x.astype(BF16), wgu_bf[...]) + bgu_ref[0]
        g = jnp.minimum(gu[:, :D_FF], SWIGLU_LIMIT)
        lin = jnp.clip(gu[:, D_FF:], -SWIGLU_LIMIT, SWIGLU_LIMIT)
        act = g * (1.0 / (1.0 + jnp.exp(-SWIGLU_ALPHA * g))) * (lin + 1.0)
        out = _dot(act.astype(BF16), wd_bf[...]) + bd_ref[0]
        obuf[slot] = out.reshape(MOE_TILE, 1, D_MODEL)
        start_scatter(tile_rows[t], slot)

        @pl.when(t == used - 1)
        def _():
            @pl.when(t >= 1)
            def _():
                wait_scatter(tile_rows[jnp.maximum(t - 1, 0)], 1 - slot)
            wait_scatter(tile_rows[t], slot)


def _experts(h2_p, h2_s, slot_src, slot_dst, tile_e, tile_rows, n_used, w_gu, b_gu, w_down, b_down):
    n_prompt = h2_p.shape[0]
    n_tok = n_prompt + h2_s.shape[0]
    n_tiles = slot_src.shape[0] // MOE_TILE
    by_e = lambda *blk: pl.BlockSpec((1,) + blk, lambda t, te, tr, nu: (te[t],) + (0,) * len(blk))
    hbm = pl.BlockSpec(memory_space=pl.ANY)
    this_tile = pl.BlockSpec((MOE_TILE,), lambda t, *_: (t,), memory_space=pltpu.SMEM)
    grid_spec = pltpu.PrefetchScalarGridSpec(
        num_scalar_prefetch=3,
        grid=(n_tiles,),
        in_specs=[this_tile,
                  pl.BlockSpec((MOE_TILE,), lambda t, *_: (jnp.minimum(t + 1, n_tiles - 1),), memory_space=pltpu.SMEM),
                  this_tile,
                  hbm, hbm,
                  by_e(D_MODEL, 2 * D_FF), by_e(1, 2 * D_FF), by_e(D_FF, D_MODEL), by_e(1, D_MODEL)],
        out_specs=hbm,
        scratch_shapes=[pltpu.VMEM((2, MOE_TILE, 1, D_MODEL), F32), pltpu.VMEM((2, MOE_TILE, 1, D_MODEL), F32),
                        pltpu.SemaphoreType.DMA((2,)), pltpu.SemaphoreType.DMA((2,)),
                        pltpu.VMEM((D_MODEL, 2 * D_FF), BF16), pltpu.VMEM((D_FF, D_MODEL), BF16)])
    return pl.pallas_call(
        functools.partial(_experts_body, n_prompt=n_prompt),
        grid_spec=grid_spec,
        out_shape=jax.ShapeDtypeStruct((TOP_K * n_tok, 1, D_MODEL), F32),
        compiler_params=_params("arbitrary"),
        name="experts",
    )(tile_e, tile_rows, n_used, slot_src, slot_src, slot_dst, h2_p, h2_s, w_gu,
      b_gu.reshape(N_EXPERTS, 1, 2 * D_FF), w_down, b_down.reshape(N_EXPERTS, 1, D_MODEL))


def _combine_body(*refs):
    y_refs = refs[:TOP_K]
    x1_ref, route_ref, gt_ref, gfin_ref, o_ref = refs[TOP_K:]
    tc = x1_ref.shape[0]
    route = route_ref[...]
    moe = route[:, TOP_K:TOP_K + 1] * y_refs[0][...].reshape(tc, D_MODEL)
    for t in range(1, TOP_K):
        moe = moe + route[:, TOP_K + t:TOP_K + t + 1] * y_refs[t][...].reshape(tc, D_MODEL)
    o_ref[...] = _rms(x1_ref[...] + gt_ref[...] * moe, gfin_ref[...])


def _combine(ys, first_tok, x1, route, gt, g_final):
    n = x1.shape[0]
    tc = COMBINE_TOKENS
    n_tok = ys.shape[0] // TOP_K
    assert n_tok % tc == 0 and first_tok % tc == 0
    mod_spec = (pl.BlockSpec((1, D_MODEL), lambda i: (0, 0)) if gt.shape[0] == 1
                else pl.BlockSpec((tc, D_MODEL), lambda i: (i, 0)))

    def y_spec(k):
        return pl.BlockSpec((tc, 1, D_MODEL), lambda i: ((k * n_tok + first_tok) // tc + i, 0, 0))

    return pl.pallas_call(
        _combine_body,
        grid=(n // tc,),
        in_specs=[y_spec(k) for k in range(TOP_K)] + [
            pl.BlockSpec((tc, D_MODEL), lambda i: (i, 0)),
            pl.BlockSpec((tc, ROUTE_LANES), lambda i: (i, 0)),
            mod_spec,
            pl.BlockSpec((1, D_MODEL), lambda i: (0, 0))],
        out_specs=pl.BlockSpec((tc, D_MODEL), lambda i: (i, 0)),
        out_shape=jax.ShapeDtypeStruct((n, D_MODEL), F32),
        compiler_params=_params("arbitrary"),
        name="combine",
    )(*([ys] * TOP_K), x1, route, gt, g_final.reshape(1, D_MODEL))


def _pages_keys_minor(cache):
    return cache.transpose(0, 2, 3, 1).reshape(cache.shape[0], ATT_WIDTH, PAGE_SIZE)


def _group_mean_matrix():
    g = np.arange(SGU_WIDTH) // HEAD_DIM
    return jnp.asarray((g[:, None] == g[None, :]).astype(np.float32) / HEAD_DIM, BF16)


def _key_offset_bias():
    off = np.arange(MOBA_BLOCK, dtype=np.float32)[None, :, None]
    return jnp.asarray(np.broadcast_to(np.asarray(_SLOPES, np.float32)[:, None, None] * off,
                                       (N_ATT_HEADS, MOBA_BLOCK, MOBA_BLOCK)))


def kernel(x_prompt, x_sample, cache_k, cache_v, page_table, c_prompt, c_sample, g_mix, w_ada, b_ada, w_in,
           g_sgu, w_sgu, b_sgu, g_out_att, g_out_sgu, w_out, g_ffn, w_router, b_router, w_gu, b_gu, w_down,
           b_down, g_final):
    depth = w_in.shape[0]
    batch, seq, _ = x_prompt.shape
    dec_batch, dec_seq, _ = x_sample.shape
    assert depth == 1, "the final norm is fused into the last stage of the single layer"
    assert batch == 1 and seq % MOBA_BLOCK == 0
    n_dec = dec_batch * dec_seq
    assert n_dec % CHUNK == 0 and CHUNK % dec_seq == 0 and dec_seq <= MOBA_BLOCK
    n_tok = seq + n_dec
    n_pool = cache_k.shape[1]

    bd = _group_mean_matrix()
    kp = _key_offset_bias()
    xp = x_prompt.reshape(seq, D_MODEL)
    xs = x_sample.reshape(n_dec, D_MODEL)
    c_all = jnp.concatenate([c_prompt, c_sample], axis=0)
    c_rows = -(-c_all.shape[0] // 8) * 8
    c_all = jnp.pad(c_all, ((0, c_rows - c_all.shape[0]), (0, 0)))
    k_rows, v_rows, ks_rows, vs_rows, sgu_rows = [], [], [], [], []

    for l in range(depth):
        mod = _ada(c_all, w_ada[l], b_ada[l])
        mod_p = [mod[0:1, t * D_MODEL:(t + 1) * D_MODEL] for t in range(6)]
        mod_rows = jnp.broadcast_to(mod[1:1 + dec_batch, None, :], (dec_batch, dec_seq, 6 * D_MODEL))
        mod_rows = mod_rows.reshape(n_dec, 6 * D_MODEL)
        mod_s = [mod_rows[:, t * D_MODEL:(t + 1) * D_MODEL] for t in range(6)]
        w_in_bf = w_in[l].astype(BF16)
        w_out_bf = w_out[l].astype(BF16)
        gs_flat = g_sgu[l].reshape(SGU_WIDTH)

        q, k, v, ug, vsn, kb, vt, kmean = _inproj(xp, g_mix[l], mod_p[1], mod_p[0], w_in_bf, gs_flat, bd, True)
        attn = _moba_prompt(q, kb, vt, kmean.reshape(seq // MOBA_BLOCK, ATT_WIDTH), kp)
        bias_p = jnp.repeat(b_sgu[l].T, HEAD_DIM, axis=1)
        x1_p, h2_p, route_p, cnt_p = _mix(attn, ug, vsn, xp, w_sgu[l], bias_p, g_out_att[l], g_out_sgu[l],
                                          w_out_bf, mod_p[2], mod_p[4], mod_p[3], g_ffn[l], w_router[l],
                                          b_router[l], jnp.zeros((1, N_EXPERTS), F32), MOBA_BLOCK)
        k_rows.append(k)
        v_rows.append(v)

        qs, ks, vs, ugs, vsns = _inproj(xs, g_mix[l], mod_s[1], mod_s[0], w_in_bf, gs_flat, bd, False)
        attn_s = _moba_decode(qs.reshape(dec_batch, dec_seq, ATT_WIDTH), ks.reshape(dec_batch, dec_seq, ATT_WIDTH),
                              vs.reshape(dec_batch, dec_seq, ATT_WIDTH),
                              _pages_keys_minor(cache_k[l]), _pages_keys_minor(cache_v[l]), page_table)
        eye = jnp.eye(CHUNK // dec_seq, dtype=F32)
        w_s = jax.vmap(lambda w: jnp.kron(eye, w))(w_sgu[l][:, :dec_seq, :dec_seq])
        bias_s = jnp.tile(jnp.repeat(b_sgu[l].T[:dec_seq], HEAD_DIM, axis=1), (CHUNK // dec_seq, 1))
        x1_s, h2_s, route_s, cnt = _mix(attn_s.reshape(n_dec, ATT_WIDTH), ugs, vsns, xs, w_s, bias_s, g_out_att[l],
                                        g_out_sgu[l], w_out_bf, mod_s[2], mod_s[4], mod_s[3], g_ffn[l],
                                        w_router[l], b_router[l], cnt_p, CHUNK)
        ks_rows.append(ks)
        vs_rows.append(vs)
        sgu_rows.append(vsns)

        route = jnp.concatenate([route_p, route_s], axis=0)
        e_idx = route[:, :TOP_K].astype(jnp.int32)
        rank = route[:, 2 * TOP_K:3 * TOP_K].astype(jnp.int32)
        cnt_i = cnt.reshape(N_EXPERTS).astype(jnp.int32)
        pcnt = (cnt_i + MOE_TILE - 1) // MOE_TILE * MOE_TILE
        pend = jnp.cumsum(pcnt)
        pstart = pend - pcnt
        dest = (pstart[e_idx] + rank).reshape(-1)
        n_tiles = -(-(n_tok * TOP_K) // MOE_TILE) + N_EXPERTS
        tile_start = jnp.arange(n_tiles, dtype=jnp.int32) * MOE_TILE
        tile_e = jnp.minimum(jnp.sum((pend[None, :] <= tile_start[:, None]).astype(jnp.int32), axis=1), N_EXPERTS - 1)
        n_used = (pend[-1:] // MOE_TILE).astype(jnp.int32)
        tile_rows = jnp.clip((pstart + cnt_i)[tile_e] - tile_start, 0, MOE_TILE).astype(jnp.int32)
        slot_assign = jnp.zeros((n_tiles * MOE_TILE,), jnp.int32).at[dest].set(
            jnp.arange(n_tok * TOP_K, dtype=jnp.int32))
        slot_src = slot_assign // TOP_K
        slot_dst = (slot_assign % TOP_K) * n_tok + slot_src

        ys = _experts(h2_p, h2_s, slot_src, slot_dst, tile_e, tile_rows, n_used, w_gu[l], b_gu[l], w_down[l],
                      b_down[l])
        xp = _combine(ys, 0, x1_p, route_p, mod_p[5], g_final)
        xs = _combine(ys, seq, x1_s, route_s, mod_s[5], g_final)

    hd = (N_ATT_HEADS, HEAD_DIM)
    k_p = jnp.stack(k_rows).reshape(depth, batch, seq // PAGE_SIZE, PAGE_SIZE, *hd)
    v_p = jnp.stack(v_rows).reshape(depth, batch, seq // PAGE_SIZE, PAGE_SIZE, *hd)
    k_s = jnp.stack(ks_rows).reshape(depth, dec_batch, dec_seq, *hd)
    v_s = jnp.stack(vs_rows).reshape(depth, dec_batch, dec_seq, *hd)
    sgu_v = jnp.stack(sgu_rows).reshape(depth, dec_batch, dec_seq, N_SGU_GROUPS, HEAD_DIM)
    return (xp.reshape(batch, seq, D_MODEL), xs.reshape(dec_batch, dec_seq, D_MODEL), k_p, v_p, k_s, v_s, sgu_v)
```

```python
import functools

import numpy as np
import jax
import jax.numpy as jnp
from jax import lax
from jax.experimental import pallas as pl
from jax.experimental.pallas import tpu as pltpu

F32 = jnp.float32
BF16 = jnp.bfloat16

D_MODEL = 1024
HEAD_DIM = 64
N_ATT_HEADS = 8
N_SGU_GROUPS = 8
ATT_WIDTH = N_ATT_HEADS * HEAD_DIM
SGU_WIDTH = N_SGU_GROUPS * HEAD_DIM
IN_WIDTH = 3 * ATT_WIDTH + 2 * SGU_WIDTH
CHUNK = 128
MOBA_BLOCK = 256
MOBA_TOPK = 3
PAGE_SIZE = 128
N_EXPERTS = 32
TOP_K = 4
D_FF = D_MODEL
SWIGLU_LIMIT = 7.0
SWIGLU_ALPHA = 1.702
EPS = 1e-6
MASK_VALUE = -1e30

HEADS_PER_GROUP = 4
GROUP_WIDTH = HEADS_PER_GROUP * HEAD_DIM
ROUTE_LANES = 128
MOE_TILE = 256
DMA_UNROLL = 8
COMBINE_TOKENS = 128
OWN_ROWS = 16
VMEM_LIMIT = 56 * 1024 * 1024

_SLOPES = [2.0 ** (-8.0 * (h + 1) / N_ATT_HEADS) for h in range(N_ATT_HEADS)]


def _params(*sem, flags=None):
    return pltpu.CompilerParams(dimension_semantics=sem, vmem_limit_bytes=VMEM_LIMIT, flags=flags)


def _split(a):
    hi = a.astype(BF16)
    lo = (a - hi.astype(F32)).astype(BF16)
    return hi, lo


def _dot(a, b):
    return jnp.dot(a, b, preferred_element_type=F32)


def _dot_nt(a, b):
    return lax.dot_general(a, b, (((1,), (1,)), ((), ())), preferred_element_type=F32)


def _dot3(a, b, dot=_dot):
    a_hi, a_lo = _split(a)
    b_hi, b_lo = _split(b)
    return dot(a_hi, b_hi) + (dot(a_hi, b_lo) + dot(a_lo, b_hi))


def _rms(x, g):
    return x * lax.rsqrt(jnp.mean(x * x, axis=-1, keepdims=True) + EPS) * g


def _gelu(x):
    return x * (0.5 * (1.0 + jnp.tanh(0.7978845608028654 * (x + 0.044715 * (x * x * x)))))


def _ada_body(c_ref, w_ref, b_ref, o_ref):
    c = c_ref[...]
    s = c * (1.0 / (1.0 + jnp.exp(-c)))
    o_ref[...] = _dot3(s, w_ref[...]) + b_ref[...]


def _ada(c, w, b):
    rows = c.shape[0]
    n = w.shape[1]
    tn = 1536
    return pl.pallas_call(
        _ada_body,
        grid=(n // tn,),
        in_specs=[pl.BlockSpec((rows, D_MODEL), lambda j: (0, 0)),
                  pl.BlockSpec((D_MODEL, tn), lambda j: (0, j)),
                  pl.BlockSpec((1, tn), lambda j: (0, j))],
        out_specs=pl.BlockSpec((rows, tn), lambda j: (0, j)),
        out_shape=jax.ShapeDtypeStruct((rows, n), F32),
        compiler_params=_params("arbitrary"),
        name="ada",
    )(c, w, b.reshape(1, n))


def _inproj_body(x_ref, g_ref, sc_ref, sh_ref, w_ref, gs_ref, bd_ref, q_ref, k_ref, v_ref, ug_ref, vsn_ref,
                 *prompt_refs):
    h = _rms(x_ref[...], g_ref[...]) * (1.0 + sc_ref[...]) + sh_ref[...]
    z = _dot(h.astype(BF16), w_ref[...])
    a = ATT_WIDTH
    q_ref[...] = z[:, :a]
    k = z[:, a:2 * a]
    v = z[:, 2 * a:3 * a]
    k_ref[...] = k
    v_ref[...] = v
    ug_ref[...] = _gelu(z[:, 3 * a:3 * a + SGU_WIDTH])
    vg = _gelu(z[:, 3 * a + SGU_WIDTH:])
    sq_hi, sq_lo = _split(vg * vg)
    gmean = _dot(sq_hi, bd_ref[...]) + _dot(sq_lo, bd_ref[...])
    vsn_ref[...] = vg * lax.rsqrt(gmean + EPS) * gs_ref[...]
    if prompt_refs:
        kb_ref, vt_ref, km_ref = prompt_refs
        kb_ref[...] = k.astype(BF16)
        vt_ref[...] = v.T.astype(BF16)
        km_ref[0] = jnp.mean(k, axis=0, keepdims=True)


def _inproj(x, g_mix, sc, sh, w_in_bf, g_sgu, bd, prompt):
    n = x.shape[0]
    tm = MOBA_BLOCK if prompt else n
    mod_rows = sc.shape[0]
    mod_spec = (pl.BlockSpec((1, D_MODEL), lambda i: (0, 0)) if mod_rows == 1
                else pl.BlockSpec((tm, D_MODEL), lambda i: (i, 0)))
    row = lambda w: pl.BlockSpec((tm, w), lambda i: (i, 0))
    out_specs = [row(ATT_WIDTH)] * 3 + [row(SGU_WIDTH)] * 2
    out_shape = [jax.ShapeDtypeStruct((n, ATT_WIDTH), F32)] * 3 + [jax.ShapeDtypeStruct((n, SGU_WIDTH), F32)] * 2
    if prompt:
        out_specs += [row(ATT_WIDTH), pl.BlockSpec((ATT_WIDTH, tm), lambda i: (0, i)),
                      pl.BlockSpec((1, 1, ATT_WIDTH), lambda i: (i, 0, 0))]
        out_shape += [jax.ShapeDtypeStruct((n, ATT_WIDTH), BF16), jax.ShapeDtypeStruct((ATT_WIDTH, n), BF16),
                      jax.ShapeDtypeStruct((n // tm, 1, ATT_WIDTH), F32)]
    return pl.pallas_call(
        _inproj_body,
        grid=(n // tm,),
        in_specs=[row(D_MODEL),
                  pl.BlockSpec((1, D_MODEL), lambda i: (0, 0)),
                  mod_spec, mod_spec,
                  pl.BlockSpec((D_MODEL, IN_WIDTH), lambda i: (0, 0)),
                  pl.BlockSpec((1, SGU_WIDTH), lambda i: (0, 0)),
                  pl.BlockSpec((SGU_WIDTH, SGU_WIDTH), lambda i: (0, 0))],
        out_specs=out_specs,
        out_shape=out_shape,
        compiler_params=_params("arbitrary"),
        name="inproj_prompt" if prompt else "inproj_sample",
    )(x, g_mix.reshape(1, D_MODEL), sc, sh, w_in_bf, g_sgu.reshape(1, SGU_WIDTH), bd)


def _select_bias(gate, n_past, axis):
    blk = lax.broadcasted_iota(jnp.int32, gate.shape, axis).astype(F32)
    past = blk < n_past
    gate = jnp.where(past, gate, -jnp.inf)
    bias = jnp.full(gate.shape, MASK_VALUE, F32)
    for _ in range(MOBA_TOPK):
        mx = jnp.max(gate, axis=axis, keepdims=True)
        first = jnp.min(jnp.where(gate == mx, blk, float(gate.shape[axis])), axis=axis, keepdims=True)
        hit = blk == first
        bias = jnp.where(hit, 0.0, bias)
        gate = jnp.where(hit, -jnp.inf, gate)
    return jnp.where(past, bias, MASK_VALUE)


def _moba_body(i_of, j_of, q_ref, kb_ref, vt_ref, km_ref, kp_ref, o_ref, qm_ref, sel_ref, m_ref, l_ref, acc_ref):
    s = pl.program_id(0)
    i = i_of[s]
    j = j_of[s]
    bq = MOBA_BLOCK
    wide = HEADS_PER_GROUP * bq
    n_groups = N_ATT_HEADS // HEADS_PER_GROUP
    head_of_row = lax.broadcasted_iota(jnp.int32, (GROUP_WIDTH, bq), 0) // HEAD_DIM

    def sweep(diag):
        col = lax.broadcasted_iota(jnp.int32, (1, wide), 1)
        dist0 = ((i - j) * MOBA_BLOCK + col % bq).astype(F32)
        if diag:
            key = lax.broadcasted_iota(jnp.int32, (bq, wide), 0)
            qry = lax.broadcasted_iota(jnp.int32, (bq, wide), 1) % bq
            causal = jnp.where(key <= qry, 0.0, MASK_VALUE)
        for g in range(n_groups):
            slope = jnp.zeros((1, wide), F32)
            for hl in range(HEADS_PER_GROUP):
                slope = jnp.where(col // bq == hl, _SLOPES[g * HEADS_PER_GROUP + hl], slope)
            lanes = slice(g * GROUP_WIDTH, (g + 1) * GROUP_WIDTH)
            s0 = _dot(kb_ref[:, lanes], qm_ref[g]) + kp_ref[g]
            brow = -slope * dist0
            if diag:
                s0 = s0 + causal
                m_old = jnp.full((1, wide), MASK_VALUE, F32)
                l_old = jnp.zeros((1, wide), F32)
            else:
                brow = brow + sel_ref[g, pl.ds(j, 1), :]
                m_old = m_ref[g:g + 1, :]
                l_old = l_ref[g:g + 1, :]
            m_new = jnp.maximum(m_old, jnp.max(s0, axis=0, keepdims=True) + brow)
            p32 = jnp.exp(s0 - (m_new - brow))
            alpha = jnp.exp(m_old - m_new)
            l_ref[g:g + 1, :] = alpha * l_old + jnp.sum(p32, axis=0, keepdims=True)
            m_ref[g:g + 1, :] = m_new
            p = p32.astype(BF16)
            vt = vt_ref[lanes, :]
            vt_heads = jnp.concatenate([jnp.where(head_of_row == hl, vt, jnp.zeros_like(vt))
                                        for hl in range(HEADS_PER_GROUP)], axis=1)
            p_heads = jnp.concatenate([p[:, hl * bq:(hl + 1) * bq] for hl in range(HEADS_PER_GROUP)], axis=0)
            pv = _dot(vt_heads, p_heads)
            if diag:
                acc_ref[g] = pv
            else:
                alpha_rows = jnp.concatenate(
                    [jnp.broadcast_to(alpha[:, hl * bq:(hl + 1) * bq], (HEAD_DIM, bq)) for hl in range(HEADS_PER_GROUP)],
                    axis=0)
                acc_ref[g] = alpha_rows * acc_ref[g] + pv

    @pl.when(j == i)
    def _first():
        qt = (q_ref[...] * (HEAD_DIM ** -0.5)).T
        for g in range(n_groups):
            lanes = slice(g * GROUP_WIDTH, (g + 1) * GROUP_WIDTH)
            for hl in range(HEADS_PER_GROUP):
                qm = jnp.where(head_of_row == hl, qt[lanes, :], 0.0)
                qm_ref[g, :, hl * bq:(hl + 1) * bq] = qm.astype(BF16)
                gate = _dot3(km_ref[:, lanes], qm)
                sel_ref[g, :, hl * bq:(hl + 1) * bq] = _select_bias(gate, i.astype(F32), 0)
        sweep(True)

    @pl.when(j != i)
    def _past():
        sweep(False)

    @pl.when((j == i - 1) | (i == 0))
    def _last():
        outs = []
        for g in range(n_groups):
            inv = jnp.concatenate(
                [jnp.broadcast_to(l_ref[g:g + 1, hl * bq:(hl + 1) * bq], (HEAD_DIM, bq)) for hl in range(HEADS_PER_GROUP)],
                axis=0)
            outs.append(acc_ref[g] / inv)
        o_ref[...] = jnp.concatenate(outs, axis=0).T


def _moba_prompt(q, kb, vt, kmean, kp):
    t = q.shape[0]
    nb = t // MOBA_BLOCK
    i_of = np.concatenate([np.full(i + 1, i, np.int32) for i in range(nb)])
    j_of = np.concatenate([np.concatenate([[i], np.arange(i)]).astype(np.int32) for i in range(nb)])
    n_groups = N_ATT_HEADS // HEADS_PER_GROUP
    wide = HEADS_PER_GROUP * MOBA_BLOCK
    grid_spec = pltpu.PrefetchScalarGridSpec(
        num_scalar_prefetch=2,
        grid=(len(i_of),),
        in_specs=[pl.BlockSpec((MOBA_BLOCK, ATT_WIDTH), lambda s, io, jo: (io[s], 0)),
                  pl.BlockSpec((MOBA_BLOCK, ATT_WIDTH), lambda s, io, jo: (jo[s], 0)),
                  pl.BlockSpec((ATT_WIDTH, MOBA_BLOCK), lambda s, io, jo: (0, jo[s])),
                  pl.BlockSpec((nb, ATT_WIDTH), lambda s, io, jo: (0, 0)),
                  pl.BlockSpec((n_groups, MOBA_BLOCK, wide), lambda s, io, jo: (0, 0, 0))],
        out_specs=pl.BlockSpec((MOBA_BLOCK, ATT_WIDTH), lambda s, io, jo: (io[s], 0)),
        scratch_shapes=[pltpu.VMEM((n_groups, GROUP_WIDTH, wide), BF16),
                        pltpu.VMEM((n_groups, nb, wide), F32),
                        pltpu.VMEM((n_groups, wide), F32),
                        pltpu.VMEM((n_groups, wide), F32),
                        pltpu.VMEM((n_groups, GROUP_WIDTH, MOBA_BLOCK), F32)])
    return pl.pallas_call(
        _moba_body,
        grid_spec=grid_spec,
        out_shape=jax.ShapeDtypeStruct((t, ATT_WIDTH), F32),
        compiler_params=_params("arbitrary"),
        name="moba_prompt",
    )(jnp.asarray(i_of), jnp.asarray(j_of), q, kb, vt, kmean, kp)


def _decode_body(pt_ref, q_ref, kn_ref, vn_ref, *refs, pages, n_blocks, past_len, n_q):
    k_pages = refs[:pages]
    v_pages = refs[pages:2 * pages]
    o_ref = refs[2 * pages]
    gate_ref, op_ref, m_ref, l_ref = refs[2 * pages + 1:]
    step = pl.program_id(1)
    rows = N_ATT_HEADS * n_q
    head_of_row = lax.broadcasted_iota(jnp.int32, (rows, ATT_WIDTH), 0) % N_ATT_HEADS
    head_of_lane = lax.broadcasted_iota(jnp.int32, (rows, ATT_WIDTH), 1) // HEAD_DIM
    own_head = head_of_row == head_of_lane
    rid = lax.broadcasted_iota(jnp.int32, (rows, 1), 0)
    slope = jnp.zeros((rows, 1), F32)
    for h in range(N_ATT_HEADS):
        slope = jnp.where(rid % N_ATT_HEADS == h, _SLOPES[h], slope)
    qidx = rid // N_ATT_HEADS
    qpos = (past_len + qidx).astype(F32)
    blk_lane = lax.broadcasted_iota(jnp.int32, (rows, 128), 1)

    @pl.when(step == 0)
    def _init():
        m_ref[...] = jnp.full(m_ref.shape, MASK_VALUE, F32)
        l_ref[...] = jnp.zeros(l_ref.shape, F32)
        gate_ref[...] = jnp.zeros(gate_ref.shape, F32)

    qm_bf = jnp.where(own_head, q_ref[0] * (HEAD_DIM ** -0.5), 0.0).astype(BF16)
    for b in range(pages // 2):
        blk = step * (pages // 2) + b
        kt = jnp.concatenate([k_pages[2 * b][0], k_pages[2 * b + 1][0]], axis=1)
        vt = jnp.concatenate([v_pages[2 * b][0], v_pages[2 * b + 1][0]], axis=1)
        qk = _dot(qm_bf, kt.astype(BF16))
        gate_ref[...] = jnp.where(blk_lane == blk, jnp.sum(qk, axis=-1, keepdims=True) * (1.0 / MOBA_BLOCK),
                                  gate_ref[...])
        kpos = (blk * MOBA_BLOCK + lax.broadcasted_iota(jnp.int32, (1, MOBA_BLOCK), 1)).astype(F32)
        s = qk - slope * (qpos - kpos)
        m = jnp.max(s, axis=-1, keepdims=True)
        p = jnp.exp(s - m)
        m_ref[...] = jnp.where(blk_lane == blk, m, m_ref[...])
        l_ref[...] = jnp.where(blk_lane == blk, jnp.sum(p, axis=-1, keepdims=True), l_ref[...])
        op_ref[blk] = jnp.where(own_head, _dot_nt(p.astype(BF16), vt.astype(BF16)), 0.0)

    @pl.when(step == pl.num_programs(1) - 1)
    def _merge():
        sel = _select_bias(gate_ref[...], float(n_blocks), 1)
        kn = kn_ref[0]
        vn = vn_ref[0]
        kidx = lax.broadcasted_iota(jnp.int32, (rows, kn.shape[0]), 1)
        dist = (qidx - kidx).astype(F32)
        s_own = _dot_nt(qm_bf, kn.astype(BF16)) - slope * dist
        s_own = jnp.where(dist >= 0, s_own, MASK_VALUE)
        m_blk = m_ref[...] + sel
        m_all = jnp.maximum(jnp.max(m_blk, axis=-1, keepdims=True), jnp.max(s_own, axis=-1, keepdims=True))
        w = jnp.exp(m_blk - m_all)
        p_own = jnp.exp(s_own - m_all)
        den = jnp.sum(w * l_ref[...], axis=-1, keepdims=True) + jnp.sum(p_own, axis=-1, keepdims=True)
        acc = jnp.where(own_head, _dot(p_own.astype(BF16), vn.astype(BF16)), 0.0)
        for b in range(n_blocks):
            acc = acc + w[:, b:b + 1] * op_ref[b]
        acc = acc / den
        for qi in range(n_q):
            o_ref[0, qi:qi + 1, :] = jnp.sum(acc[qi * N_ATT_HEADS:(qi + 1) * N_ATT_HEADS], axis=0, keepdims=True)


def _moba_decode(q, k_new, v_new, cache_k, cache_v, page_table):
    b, n_q, _ = q.shape
    n_pages = page_table.shape[1]
    past_len = n_pages * PAGE_SIZE
    n_blocks = past_len // MOBA_BLOCK
    assert n_blocks <= 128 and n_pages % 2 == 0
    pages = 8 if n_pages % 8 == 0 else 2
    rows = N_ATT_HEADS * n_q
    q_rows = jnp.repeat(q, N_ATT_HEADS, axis=1)
    pad = ((0, 0), (0, OWN_ROWS - n_q), (0, 0))
    by_seq = lambda r: pl.BlockSpec((1, r, ATT_WIDTH), lambda bi, st, pt: (bi, 0, 0))

    def page_spec(t):
        return pl.BlockSpec((1, ATT_WIDTH, PAGE_SIZE), lambda bi, st, pt: (pt[bi, st * pages + t], 0, 0))

    grid_spec = pltpu.PrefetchScalarGridSpec(
        num_scalar_prefetch=1,
        grid=(b, n_pages // pages),
        in_specs=[by_seq(rows), by_seq(OWN_ROWS), by_seq(OWN_ROWS)] + [page_spec(t) for t in range(pages)] * 2,
        out_specs=by_seq(n_q),
        scratch_shapes=[pltpu.VMEM((rows, 128), F32),
                        pltpu.VMEM((n_blocks, rows, ATT_WIDTH), F32),
                        pltpu.VMEM((rows, 128), F32),
                        pltpu.VMEM((rows, 128), F32)])
    return pl.pallas_call(
        functools.partial(_decode_body, pages=pages, n_blocks=n_blocks, past_len=past_len, n_q=n_q),
        grid_spec=grid_spec,
        out_shape=jax.ShapeDtypeStruct((b, n_q, ATT_WIDTH), F32),
        compiler_params=_params("arbitrary", "arbitrary"),
        name="moba_decode",
    )(page_table, q_rows, jnp.pad(k_new, pad), jnp.pad(v_new, pad), *([cache_k] * pages), *([cache_v] * pages))


def _mix_body(attn_ref, ug_ref, vsn_ref, x_ref, wsg_ref, bsg_ref, ga_ref, gs_ref, wo_ref, gt_ref, sc_ref, sh_ref,
              gf_ref, wr_ref, br_ref, cin_ref, x1_ref, h2_ref, route_ref, cnt_ref, carry_ref):
    tm = x_ref.shape[0]

    @pl.when(pl.program_id(0) == 0)
    def _():
        carry_ref[...] = cin_ref[...]

    r = lax.broadcasted_iota(jnp.int32, (CHUNK, CHUNK), 0)
    c = lax.broadcasted_iota(jnp.int32, (CHUNK, CHUNK), 1)
    wcat = jnp.concatenate([jnp.where(c <= r, wsg_ref[g], 0.0).astype(BF16) for g in range(N_SGU_GROUPS)], axis=1)
    lane_group = lax.broadcasted_iota(jnp.int32, (CHUNK, SGU_WIDTH), 1) // HEAD_DIM
    mixed = []
    for ch in range(tm // CHUNK):
        vs = vsn_ref[ch * CHUNK:(ch + 1) * CHUNK, :]
        stack = jnp.concatenate([jnp.where(lane_group == g, vs, 0.0).astype(BF16) for g in range(N_SGU_GROUPS)],
                                axis=0)
        mixed.append(_dot(wcat, stack) + bsg_ref[...])
    sgu = ug_ref[...] * (mixed[0] if len(mixed) == 1 else jnp.concatenate(mixed, axis=0))
    merged = jnp.concatenate([_rms(attn_ref[...], ga_ref[...]), _rms(sgu, gs_ref[...])], axis=1)
    x1 = x_ref[...] + gt_ref[...] * _dot(merged.astype(BF16), wo_ref[...])
    x1_ref[...] = x1
    h2 = _rms(x1, gf_ref[...]) * (1.0 + sc_ref[...]) + sh_ref[...]
    h2_ref[...] = h2.reshape(tm, 1, D_MODEL)

    logits = _dot3(h2, wr_ref[...]) + br_ref[...]
    lane_f = lax.broadcasted_iota(jnp.int32, (tm, N_EXPERTS), 1).astype(F32)
    vals, hits = [], []
    work = logits
    for _ in range(TOP_K):
        mx = jnp.max(work, axis=-1, keepdims=True)
        first = jnp.min(jnp.where(work == mx, lane_f, float(N_EXPERTS)), axis=-1, keepdims=True)
        hit = lane_f == first
        vals.append(mx)
        hits.append(hit)
        work = jnp.where(hit, -jnp.inf, work)
    ex = [jnp.exp(v - vals[0]) for v in vals]
    den = ex[0] + ex[1] + ex[2] + ex[3]
    chosen = jnp.zeros((tm, N_EXPERTS), F32)
    for hit in hits:
        chosen = jnp.where(hit, 1.0, chosen)
    rr = lax.broadcasted_iota(jnp.int32, (tm, tm), 0)
    cc = lax.broadcasted_iota(jnp.int32, (tm, tm), 1)
    before = _dot(jnp.where(cc < rr, 1.0, 0.0).astype(BF16), chosen.astype(BF16)) + carry_ref[...]
    rl = lax.broadcasted_iota(jnp.int32, (tm, ROUTE_LANES), 1)
    route = jnp.zeros((tm, ROUTE_LANES), F32)
    for t in range(TOP_K):
        idx = jnp.sum(jnp.where(hits[t], lane_f, 0.0), axis=-1, keepdims=True)
        rank = jnp.sum(jnp.where(hits[t], before, 0.0), axis=-1, keepdims=True)
        route = jnp.where(rl == t, idx, route)
        route = jnp.where(rl == TOP_K + t, ex[t] / den, route)
        route = jnp.where(rl == 2 * TOP_K + t, rank, route)
    route_ref[...] = route
    carry_ref[...] = carry_ref[...] + jnp.sum(chosen, axis=0, keepdims=True)
    cnt_ref[...] = carry_ref[...]


def _mix(attn, ug, vsn, x, wsg, bsg, g_att, g_sgu, w_out_bf, gt, sc, sh, g_ffn, w_router, b_router, cnt_in, tm):
    n = x.shape[0]
    mod_rows = gt.shape[0]
    mod_spec = (pl.BlockSpec((1, D_MODEL), lambda i: (0, 0)) if mod_rows == 1
                else pl.BlockSpec((tm, D_MODEL), lambda i: (i, 0)))
    row = lambda w: pl.BlockSpec((tm, w), lambda i: (i, 0))
    full = lambda *shape: pl.BlockSpec(shape, lambda i: (0,) * len(shape))
    return pl.pallas_call(
        _mix_body,
        grid=(n // tm,),
        in_specs=[row(ATT_WIDTH), row(SGU_WIDTH), row(SGU_WIDTH), row(D_MODEL),
                  full(N_SGU_GROUPS, CHUNK, CHUNK), full(CHUNK, SGU_WIDTH),
                  full(1, ATT_WIDTH), full(1, SGU_WIDTH), full(D_MODEL, D_MODEL),
                  mod_spec, mod_spec, mod_spec,
                  full(1, D_MODEL), full(D_MODEL, N_EXPERTS), full(1, N_EXPERTS), full(1, N_EXPERTS)],
        out_specs=[row(D_MODEL), pl.BlockSpec((tm, 1, D_MODEL), lambda i: (i, 0, 0)), row(ROUTE_LANES),
                   full(1, N_EXPERTS)],
        out_shape=[jax.ShapeDtypeStruct((n, D_MODEL), F32), jax.ShapeDtypeStruct((n, 1, D_MODEL), F32),
                   jax.ShapeDtypeStruct((n, ROUTE_LANES), F32), jax.ShapeDtypeStruct((1, N_EXPERTS), F32)],
        scratch_shapes=[pltpu.VMEM((1, N_EXPERTS), F32)],
        compiler_params=_params("arbitrary"),
        name="mix",
    )(attn, ug, vsn, x, wsg, bsg, g_att.reshape(1, ATT_WIDTH), g_sgu.reshape(1, SGU_WIDTH), w_out_bf,
      gt, sc, sh, g_ffn.reshape(1, D_MODEL), w_router, b_router.reshape(1, N_EXPERTS), cnt_in)


def _experts_body(tile_e, tile_rows, n_used, src_ref, src_next_ref, dst_ref, h2p_ref, h2s_ref, wgu_ref, bgu_ref,
                  wd_ref, bd_ref, y_ref, xbuf, obuf, gsem, ssem, wgu_bf, wd_bf, *, n_prompt):
    t = pl.program_id(0)
    used = n_used[0]
    slot = t % 2

    def start_gather(src, rows, into):
        def body(r, carry):
            tok = src[r]

            @pl.when((r < rows) & (tok < n_prompt))
            def _():
                pltpu.make_async_copy(h2p_ref.at[pl.ds(tok, 1)], xbuf.at[into, pl.ds(r, 1)], gsem.at[into]).start()

            @pl.when((r < rows) & (tok >= n_prompt))
            def _():
                pltpu.make_async_copy(h2s_ref.at[pl.ds(tok - n_prompt, 1)], xbuf.at[into, pl.ds(r, 1)],
                                      gsem.at[into]).start()
            return carry
        lax.fori_loop(0, MOE_TILE, body, 0, unroll=DMA_UNROLL)

    def wait_gather(rows, into):
        pltpu.make_async_copy(h2p_ref.at[pl.ds(0, rows)], xbuf.at[into, pl.ds(0, rows)], gsem.at[into]).wait()

    def start_scatter(rows, frm):
        def body(r, carry):
            @pl.when(r < rows)
            def _():
                pltpu.make_async_copy(obuf.at[frm, pl.ds(r, 1)], y_ref.at[pl.ds(dst_ref[r], 1)], ssem.at[frm]).start()
            return carry
        lax.fori_loop(0, MOE_TILE, body, 0, unroll=DMA_UNROLL)

    def wait_scatter(rows, frm):
        pltpu.make_async_copy(obuf.at[frm, pl.ds(0, rows)], y_ref.at[pl.ds(0, rows)], ssem.at[frm]).wait()

    @pl.when(t == 0)
    def _():
        xbuf[...] = jnp.zeros(xbuf.shape, F32)
        start_gather(src_ref, tile_rows[0], 0)

    @pl.when(t + 1 < used)
    def _():
        start_gather(src_next_ref, tile_rows[t + 1], 1 - slot)

    e = tile_e[t]
    prev = tile_e[jnp.maximum(t - 1, 0)]

    @pl.when((t == 0) | (e != prev))
    def _():
        wgu_bf[...] = wgu_ref[0].astype(BF16)
        wd_bf[...] = wd_ref[0].astype(BF16)

    @pl.when(t < used)
    def _():
        wait_gather(tile_rows[t], slot)

        @pl.when(t >= 2)
        def _():
            wait_scatter(tile_rows[jnp.maximum(t - 2, 0)], slot)

        x = xbuf[slot].reshape(MOE_TILE, D_MODEL)
        gu = _dot(x.astype(BF16), wgu_bf[...]) + bgu_ref[0]
        g = jnp.minimum(gu[:, :D_FF], SWIGLU_LIMIT)
        lin = jnp.clip(gu[:, D_FF:], -SWIGLU_LIMIT, SWIGLU_LIMIT)
        act = g * (1.0 / (1.0 + jnp.exp(-SWIGLU_ALPHA * g))) * (lin + 1.0)
        out = _dot(act.astype(BF16), wd_bf[...]) + bd_ref[0]
        obuf[slot] = out.reshape(MOE_TILE, 1, D_MODEL)
        start_scatter(tile_rows[t], slot)

        @pl.when(t == used - 1)
        def _():
            @pl.when(t >= 1)
            def _():
                wait_scatter(tile_rows[jnp.maximum(t - 1, 0)], 1 - slot)
            wait_scatter(tile_rows[t], slot)


def _experts(h2_p, h2_s, slot_src, slot_dst, tile_e, tile_rows, n_used, w_gu, b_gu, w_down, b_down):
    n_prompt = h2_p.shape[0]
    n_tok = n_prompt + h2_s.shape[0]
    n_tiles = slot_src.shape[0] // MOE_TILE
    by_e = lambda *blk: pl.BlockSpec((1,) + blk, lambda t, te, tr, nu: (te[t],) + (0,) * len(blk))
    hbm = pl.BlockSpec(memory_space=pl.ANY)
    this_tile = pl.BlockSpec((MOE_TILE,), lambda t, *_: (t,), memory_space=pltpu.SMEM)
    grid_spec = pltpu.PrefetchScalarGridSpec(
        num_scalar_prefetch=3,
        grid=(n_tiles,),
        in_specs=[this_tile,
                  pl.BlockSpec((MOE_TILE,), lambda t, *_: (jnp.minimum(t + 1, n_tiles - 1),), memory_space=pltpu.SMEM),
                  this_tile,
                  hbm, hbm,
                  by_e(D_MODEL, 2 * D_FF), by_e(1, 2 * D_FF), by_e(D_FF, D_MODEL), by_e(1, D_MODEL)],
        out_specs=hbm,
        scratch_shapes=[pltpu.VMEM((2, MOE_TILE, 1, D_MODEL), F32), pltpu.VMEM((2, MOE_TILE, 1, D_MODEL), F32),
                        pltpu.SemaphoreType.DMA((2,)), pltpu.SemaphoreType.DMA((2,)),
                        pltpu.VMEM((D_MODEL, 2 * D_FF), BF16), pltpu.VMEM((D_FF, D_MODEL), BF16)])
    return pl.pallas_call(
        functools.partial(_experts_body, n_prompt=n_prompt),
        grid_spec=grid_spec,
        out_shape=jax.ShapeDtypeStruct((TOP_K * n_tok, 1, D_MODEL), F32),
        compiler_params=_params("arbitrary"),
        name="experts",
    )(tile_e, tile_rows, n_used, slot_src, slot_src, slot_dst, h2_p, h2_s, w_gu,
      b_gu.reshape(N_EXPERTS, 1, 2 * D_FF), w_down, b_down.reshape(N_EXPERTS, 1, D_MODEL))


def _combine_body(*refs):
    y_refs = refs[:TOP_K]
    x1_ref, route_ref, gt_ref, gfin_ref, o_ref = refs[TOP_K:]
    tc = x1_ref.shape[0]
    route = route_ref[...]
    moe = route[:, TOP_K:TOP_K + 1] * y_refs[0][...].reshape(tc, D_MODEL)
    for t in range(1, TOP_K):
        moe = moe + route[:, TOP_K + t:TOP_K + t + 1] * y_refs[t][...].reshape(tc, D_MODEL)
    o_ref[...] = _rms(x1_ref[...] + gt_ref[...] * moe, gfin_ref[...])


def _combine(ys, first_tok, x1, route, gt, g_final):
    n = x1.shape[0]
    tc = COMBINE_TOKENS
    n_tok = ys.shape[0] // TOP_K
    assert n_tok % tc == 0 and first_tok % tc == 0
    mod_spec = (pl.BlockSpec((1, D_MODEL), lambda i: (0, 0)) if gt.shape[0] == 1
                else pl.BlockSpec((tc, D_MODEL), lambda i: (i, 0)))

    def y_spec(k):
        return pl.BlockSpec((tc, 1, D_MODEL), lambda i: ((k * n_tok + first_tok) // tc + i, 0, 0))

    return pl.pallas_call(
        _combine_body,
        grid=(n // tc,),
        in_specs=[y_spec(k) for k in range(TOP_K)] + [
            pl.BlockSpec((tc, D_MODEL), lambda i: (i, 0)),
            pl.BlockSpec((tc, ROUTE_LANES), lambda i: (i, 0)),
            mod_spec,
            pl.BlockSpec((1, D_MODEL), lambda i: (0, 0))],
        out_specs=pl.BlockSpec((tc, D_MODEL), lambda i: (i, 0)),
        out_shape=jax.ShapeDtypeStruct((n, D_MODEL), F32),
        compiler_params=_params("arbitrary"),
        name="combine",
    )(*([ys] * TOP_K), x1, route, gt, g_final.reshape(1, D_MODEL))


def _pages_keys_minor(cache):
    return cache.transpose(0, 2, 3, 1).reshape(cache.shape[0], ATT_WIDTH, PAGE_SIZE)


def _group_mean_matrix():
    g = np.arange(SGU_WIDTH) // HEAD_DIM
    return jnp.asarray((g[:, None] == g[None, :]).astype(np.float32) / HEAD_DIM, BF16)


def _key_offset_bias():
    off = np.arange(MOBA_BLOCK, dtype=np.float32)[None, :, None]
    per_head = np.broadcast_to(np.asarray(_SLOPES, np.float32)[:, None, None] * off,
                               (N_ATT_HEADS, MOBA_BLOCK, MOBA_BLOCK))
    n_groups = N_ATT_HEADS // HEADS_PER_GROUP
    grouped = per_head.reshape(n_groups, HEADS_PER_GROUP, MOBA_BLOCK, MOBA_BLOCK).transpose(0, 2, 1, 3)
    return jnp.asarray(grouped.reshape(n_groups, MOBA_BLOCK, HEADS_PER_GROUP * MOBA_BLOCK))


def kernel(x_prompt, x_sample, cache_k, cache_v, page_table, c_prompt, c_sample, g_mix, w_ada, b_ada, w_in,
           g_sgu, w_sgu, b_sgu, g_out_att, g_out_sgu, w_out, g_ffn, w_router, b_router, w_gu, b_gu, w_down,
           b_down, g_final):
    depth = w_in.shape[0]
    batch, seq, _ = x_prompt.shape
    dec_batch, dec_seq, _ = x_sample.shape
    assert depth == 1, "the final norm is fused into the last stage of the single layer"
    assert batch == 1 and seq % MOBA_BLOCK == 0
    n_dec = dec_batch * dec_seq
    assert n_dec % CHUNK == 0 and CHUNK % dec_seq == 0 and dec_seq <= MOBA_BLOCK
    n_tok = seq + n_dec
    n_pool = cache_k.shape[1]

    bd = _group_mean_matrix()
    kp = _key_offset_bias()
    xp = x_prompt.reshape(seq, D_MODEL)
    xs = x_sample.reshape(n_dec, D_MODEL)
    c_all = jnp.concatenate([c_prompt, c_sample], axis=0)
    c_rows = -(-c_all.shape[0] // 8) * 8
    c_all = jnp.pad(c_all, ((0, c_rows - c_all.shape[0]), (0, 0)))
    k_rows, v_rows, ks_rows, vs_rows, sgu_rows = [], [], [], [], []

    for l in range(depth):
        mod = _ada(c_all, w_ada[l], b_ada[l])
        mod_p = [mod[0:1, t * D_MODEL:(t + 1) * D_MODEL] for t in range(6)]
        mod_rows = jnp.broadcast_to(mod[1:1 + dec_batch, None, :], (dec_batch, dec_seq, 6 * D_MODEL))
        mod_rows = mod_rows.reshape(n_dec, 6 * D_MODEL)
        mod_s = [mod_rows[:, t * D_MODEL:(t + 1) * D_MODEL] for t in range(6)]
        w_in_bf = w_in[l].astype(BF16)
        w_out_bf = w_out[l].astype(BF16)
        gs_flat = g_sgu[l].reshape(SGU_WIDTH)

        q, k, v, ug, vsn, kb, vt, kmean = _inproj(xp, g_mix[l], mod_p[1], mod_p[0], w_in_bf, gs_flat, bd, True)
        attn = _moba_prompt(q, kb, vt, kmean.reshape(seq // MOBA_BLOCK, ATT_WIDTH), kp)
        bias_p = jnp.repeat(b_sgu[l].T, HEAD_DIM, axis=1)
        x1_p, h2_p, route_p, cnt_p = _mix(attn, ug, vsn, xp, w_sgu[l], bias_p, g_out_att[l], g_out_sgu[l],
                                          w_out_bf, mod_p[2], mod_p[4], mod_p[3], g_ffn[l], w_router[l],
                                          b_router[l], jnp.zeros((1, N_EXPERTS), F32), MOBA_BLOCK)
        k_rows.append(k)
        v_rows.append(v)

        qs, ks, vs, ugs, vsns = _inproj(xs, g_mix[l], mod_s[1], mod_s[0], w_in_bf, gs_flat, bd, False)
        attn_s = _moba_decode(qs.reshape(dec_batch, dec_seq, ATT_WIDTH), ks.reshape(dec_batch, dec_seq, ATT_WIDTH),
                              vs.reshape(dec_batch, dec_seq, ATT_WIDTH),
                              _pages_keys_minor(cache_k[l]), _pages_keys_minor(cache_v[l]), page_table)
        eye = jnp.eye(CHUNK // dec_seq, dtype=F32)
        w_s = jax.vmap(lambda w: jnp.kron(eye, w))(w_sgu[l][:, :dec_seq, :dec_seq])
        bias_s = jnp.tile(jnp.repeat(b_sgu[l].T[:dec_seq], HEAD_DIM, axis=1), (CHUNK // dec_seq, 1))
        x1_s, h2_s, route_s, cnt = _mix(attn_s.reshape(n_dec, ATT_WIDTH), ugs, vsns, xs, w_s, bias_s, g_out_att[l],
                                        g_out_sgu[l], w_out_bf, mod_s[2], mod_s[4], mod_s[3], g_ffn[l],
                                        w_router[l], b_router[l], cnt_p, CHUNK)
        ks_rows.append(ks)
        vs_rows.append(vs)
        sgu_rows.append(vsns)

        route = jnp.concatenate([route_p, route_s], axis=0)
        e_idx = route[:, :TOP_K].astype(jnp.int32)
        rank = route[:, 2 * TOP_K:3 * TOP_K].astype(jnp.int32)
        cnt_i = cnt.reshape(N_EXPERTS).astype(jnp.int32)
        pcnt = (cnt_i + MOE_TILE - 1) // MOE_TILE * MOE_TILE
        pend = jnp.cumsum(pcnt)
        pstart = pend - pcnt
        dest = (pstart[e_idx] + rank).reshape(-1)
        n_tiles = -(-(n_tok * TOP_K) // MOE_TILE) + N_EXPERTS
        tile_start = jnp.arange(n_tiles, dtype=jnp.int32) * MOE_TILE
        tile_e = jnp.minimum(jnp.sum((pend[None, :] <= tile_start[:, None]).astype(jnp.int32), axis=1), N_EXPERTS - 1)
        n_used = (pend[-1:] // MOE_TILE).astype(jnp.int32)
        tile_rows = jnp.clip((pstart + cnt_i)[tile_e] - tile_start, 0, MOE_TILE).astype(jnp.int32)
        slot_assign = jnp.zeros((n_tiles * MOE_TILE,), jnp.int32).at[dest].set(
            jnp.arange(n_tok * TOP_K, dtype=jnp.int32))
        slot_src = slot_assign // TOP_K
        slot_dst = (slot_assign % TOP_K) * n_tok + slot_src

        ys = _experts(h2_p, h2_s, slot_src, slot_dst, tile_e, tile_rows, n_used, w_gu[l], b_gu[l], w_down[l],
                      b_down[l])
        xp = _combine(ys, 0, x1_p, route_p, mod_p[5], g_final)
        xs = _combine(ys, seq, x1_s, route_s, mod_s[5], g_final)

    hd = (N_ATT_HEADS, HEAD_DIM)
    k_p = jnp.stack(k_rows).reshape(depth, batch, seq // PAGE_SIZE, PAGE_SIZE, *hd)
    v_p = jnp.stack(v_rows).reshape(depth, batch, seq // PAGE_SIZE, PAGE_SIZE, *hd)
    k_s = jnp.stack(ks_rows).reshape(depth, dec_batch, dec_seq, *hd)
    v_s = jnp.stack(vs_rows).reshape(depth, dec_batch, dec_seq, *hd)
    sgu_v = jnp.stack(sgu_rows).reshape(depth, dec_batch, dec_seq, N_SGU_GROUPS, HEAD_DIM)
    return (xp.reshape(batch, seq, D_MODEL), xs.reshape(dec_batch, dec_seq, D_MODEL), k_p, v_p, k_s, v_s, sgu_v)
```

```python
import functools

import numpy as np
import jax
import jax.numpy as jnp
from jax import lax
from jax.experimental import pallas as pl
from jax.experimental.pallas import tpu as pltpu

F32 = jnp.float32
BF16 = jnp.bfloat16

D_MODEL = 1024
HEAD_DIM = 64
N_ATT_HEADS = 8
N_SGU_GROUPS = 8
ATT_WIDTH = N_ATT_HEADS * HEAD_DIM
SGU_WIDTH = N_SGU_GROUPS * HEAD_DIM
IN_WIDTH = 3 * ATT_WIDTH + 2 * SGU_WIDTH
CHUNK = 128
MOBA_BLOCK = 256
MOBA_TOPK = 3
PAGE_SIZE = 128
N_EXPERTS = 32
TOP_K = 4
D_FF = D_MODEL
SWIGLU_LIMIT = 7.0
SWIGLU_ALPHA = 1.702
EPS = 1e-6
MASK_VALUE = -1e30

HEADS_PER_GROUP = 4
GROUP_WIDTH = HEADS_PER_GROUP * HEAD_DIM
ROUTE_LANES = 128
MOE_TILE = 256
DMA_UNROLL = 8
COMBINE_TOKENS = 128
OWN_ROWS = 16
VMEM_LIMIT = 56 * 1024 * 1024

_SLOPES = [2.0 ** (-8.0 * (h + 1) / N_ATT_HEADS) for h in range(N_ATT_HEADS)]


def _params(*sem, flags=None):
    return pltpu.CompilerParams(dimension_semantics=sem, vmem_limit_bytes=VMEM_LIMIT, flags=flags)


def _split(a):
    hi = a.astype(BF16)
    lo = (a - hi.astype(F32)).astype(BF16)
    return hi, lo


def _dot(a, b):
    return jnp.dot(a, b, preferred_element_type=F32)


def _dot_nt(a, b):
    return lax.dot_general(a, b, (((1,), (1,)), ((), ())), preferred_element_type=F32)


def _dot3(a, b, dot=_dot):
    a_hi, a_lo = _split(a)
    b_hi, b_lo = _split(b)
    return dot(a_hi, b_hi) + (dot(a_hi, b_lo) + dot(a_lo, b_hi))


def _rms(x, g):
    return x * lax.rsqrt(jnp.mean(x * x, axis=-1, keepdims=True) + EPS) * g


def _gelu(x):
    return x * (0.5 * (1.0 + jnp.tanh(0.7978845608028654 * (x + 0.044715 * (x * x * x)))))


def _ada_body(c_ref, w_ref, b_ref, o_ref):
    c = c_ref[...]
    s = c * (1.0 / (1.0 + jnp.exp(-c)))
    o_ref[...] = _dot3(s, w_ref[...]) + b_ref[...]


def _ada(c, w, b):
    rows = c.shape[0]
    n = w.shape[1]
    tn = 1536
    return pl.pallas_call(
        _ada_body,
        grid=(n // tn,),
        in_specs=[pl.BlockSpec((rows, D_MODEL), lambda j: (0, 0)),
                  pl.BlockSpec((D_MODEL, tn), lambda j: (0, j)),
                  pl.BlockSpec((1, tn), lambda j: (0, j))],
        out_specs=pl.BlockSpec((rows, tn), lambda j: (0, j)),
        out_shape=jax.ShapeDtypeStruct((rows, n), F32),
        compiler_params=_params("arbitrary"),
        name="ada",
    )(c, w, b.reshape(1, n))


def _inproj_body(x_ref, g_ref, sc_ref, sh_ref, w_ref, gs_ref, bd_ref, q_ref, k_ref, v_ref, ug_ref, vsn_ref,
                 *prompt_refs):
    h = _rms(x_ref[...], g_ref[...]) * (1.0 + sc_ref[...]) + sh_ref[...]
    z = _dot(h.astype(BF16), w_ref[...])
    a = ATT_WIDTH
    q_ref[...] = z[:, :a]
    k = z[:, a:2 * a]
    v = z[:, 2 * a:3 * a]
    k_ref[...] = k
    v_ref[...] = v
    ug_ref[...] = _gelu(z[:, 3 * a:3 * a + SGU_WIDTH])
    vg = _gelu(z[:, 3 * a + SGU_WIDTH:])
    sq_hi, sq_lo = _split(vg * vg)
    gmean = _dot(sq_hi, bd_ref[...]) + _dot(sq_lo, bd_ref[...])
    vsn_ref[...] = vg * lax.rsqrt(gmean + EPS) * gs_ref[...]
    if prompt_refs:
        kb_ref, vt_ref, km_ref = prompt_refs
        kb_ref[...] = k.astype(BF16)
        vt_ref[...] = v.T.astype(BF16)
        km_ref[0] = jnp.mean(k, axis=0, keepdims=True)


def _inproj(x, g_mix, sc, sh, w_in_bf, g_sgu, bd, prompt):
    n = x.shape[0]
    tm = MOBA_BLOCK if prompt else n
    mod_rows = sc.shape[0]
    mod_spec = (pl.BlockSpec((1, D_MODEL), lambda i: (0, 0)) if mod_rows == 1
                else pl.BlockSpec((tm, D_MODEL), lambda i: (i, 0)))
    row = lambda w: pl.BlockSpec((tm, w), lambda i: (i, 0))
    out_specs = [row(ATT_WIDTH)] * 3 + [row(SGU_WIDTH)] * 2
    out_shape = [jax.ShapeDtypeStruct((n, ATT_WIDTH), F32)] * 3 + [jax.ShapeDtypeStruct((n, SGU_WIDTH), F32)] * 2
    if prompt:
        out_specs += [row(ATT_WIDTH), pl.BlockSpec((ATT_WIDTH, tm), lambda i: (0, i)),
                      pl.BlockSpec((1, 1, ATT_WIDTH), lambda i: (i, 0, 0))]
        out_shape += [jax.ShapeDtypeStruct((n, ATT_WIDTH), BF16), jax.ShapeDtypeStruct((ATT_WIDTH, n), BF16),
                      jax.ShapeDtypeStruct((n // tm, 1, ATT_WIDTH), F32)]
    return pl.pallas_call(
        _inproj_body,
        grid=(n // tm,),
        in_specs=[row(D_MODEL),
                  pl.BlockSpec((1, D_MODEL), lambda i: (0, 0)),
                  mod_spec, mod_spec,
                  pl.BlockSpec((D_MODEL, IN_WIDTH), lambda i: (0, 0)),
                  pl.BlockSpec((1, SGU_WIDTH), lambda i: (0, 0)),
                  pl.BlockSpec((SGU_WIDTH, SGU_WIDTH), lambda i: (0, 0))],
        out_specs=out_specs,
        out_shape=out_shape,
        compiler_params=_params("arbitrary"),
        name="inproj_prompt" if prompt else "inproj_sample",
    )(x, g_mix.reshape(1, D_MODEL), sc, sh, w_in_bf, g_sgu.reshape(1, SGU_WIDTH), bd)


def _select_bias(gate, n_past, axis):
    blk = lax.broadcasted_iota(jnp.int32, gate.shape, axis).astype(F32)
    past = blk < n_past
    gate = jnp.where(past, gate, -jnp.inf)
    bias = jnp.full(gate.shape, MASK_VALUE, F32)
    for _ in range(MOBA_TOPK):
        mx = jnp.max(gate, axis=axis, keepdims=True)
        first = jnp.min(jnp.where(gate == mx, blk, float(gate.shape[axis])), axis=axis, keepdims=True)
        hit = blk == first
        bias = jnp.where(hit, 0.0, bias)
        gate = jnp.where(hit, -jnp.inf, gate)
    return jnp.where(past, bias, MASK_VALUE)


def _moba_body(i_of, p_of, q_ref, kb_ref, vt_ref, km_ref, kp_ref, o_ref, qm_ref, sel_ref, m_ref, l_ref, acc_ref,
               s_ref):
    s = pl.program_id(0)
    i = i_of[s]
    pair = p_of[s]
    own_pair = i // 2
    bq = MOBA_BLOCK
    wide = HEADS_PER_GROUP * bq
    n_groups = N_ATT_HEADS // HEADS_PER_GROUP
    head_of_row = lax.broadcasted_iota(jnp.int32, (GROUP_WIDTH, bq), 0) // HEAD_DIM

    def sweep(diag, half):
        j = 2 * pair + half
        keys = slice(half * bq, (half + 1) * bq)
        col = lax.broadcasted_iota(jnp.int32, (1, wide), 1)
        dist0 = ((i - j) * MOBA_BLOCK + col % bq).astype(F32)
        if diag:
            key = lax.broadcasted_iota(jnp.int32, (bq, wide), 0)
            qry = lax.broadcasted_iota(jnp.int32, (bq, wide), 1) % bq
            causal = jnp.where(key <= qry, 0.0, MASK_VALUE)
        for g in range(n_groups):
            slope = jnp.zeros((1, wide), F32)
            for hl in range(HEADS_PER_GROUP):
                slope = jnp.where(col // bq == hl, _SLOPES[g * HEADS_PER_GROUP + hl], slope)
            lanes = slice(g * GROUP_WIDTH, (g + 1) * GROUP_WIDTH)
            s0 = _dot(kb_ref[keys, lanes], qm_ref[g]) + kp_ref[g]
            brow = -slope * dist0
            if diag:
                s0 = s0 + causal
                m_old = jnp.full((1, wide), MASK_VALUE, F32)
                l_old = jnp.zeros((1, wide), F32)
            else:
                brow = brow + sel_ref[g, pl.ds(j, 1), :]
                m_old = m_ref[g:g + 1, :]
                l_old = l_ref[g:g + 1, :]
            s_ref[g] = s0
            m_new = jnp.maximum(m_old, jnp.max(s_ref[g], axis=0, keepdims=True) + brow)
            p32 = jnp.exp(s_ref[g] - (m_new - brow))
            alpha = jnp.exp(m_old - m_new)
            l_ref[g:g + 1, :] = alpha * l_old + jnp.sum(p32, axis=0, keepdims=True)
            m_ref[g:g + 1, :] = m_new
            p = p32.astype(BF16)
            vt = vt_ref[lanes, keys]
            vt_heads = jnp.concatenate([jnp.where(head_of_row == hl, vt, jnp.zeros_like(vt))
                                        for hl in range(HEADS_PER_GROUP)], axis=1)
            p_heads = jnp.concatenate([p[:, hl * bq:(hl + 1) * bq] for hl in range(HEADS_PER_GROUP)], axis=0)
            pv = _dot(vt_heads, p_heads)
            if diag:
                acc_ref[g] = pv
            else:
                alpha_rows = jnp.concatenate(
                    [jnp.broadcast_to(alpha[:, hl * bq:(hl + 1) * bq], (HEAD_DIM, bq)) for hl in range(HEADS_PER_GROUP)],
                    axis=0)
                acc_ref[g] = alpha_rows * acc_ref[g] + pv

    @pl.when(pair == own_pair)
    def _first():
        qt = (q_ref[...] * (HEAD_DIM ** -0.5)).T
        for g in range(n_groups):
            lanes = slice(g * GROUP_WIDTH, (g + 1) * GROUP_WIDTH)
            for hl in range(HEADS_PER_GROUP):
                qm = jnp.where(head_of_row == hl, qt[lanes, :], 0.0)
                qm_ref[g, :, hl * bq:(hl + 1) * bq] = qm.astype(BF16)
                gate = _dot3(km_ref[:, lanes], qm)
                sel_ref[g, :, hl * bq:(hl + 1) * bq] = _select_bias(gate, i.astype(F32), 0)

    @pl.when((pair == own_pair) & (i % 2 == 0))
    def _own_then_future():
        sweep(True, 0)

    @pl.when((pair == own_pair) & (i % 2 == 1))
    def _own_then_past():
        sweep(True, 1)
        sweep(False, 0)

    @pl.when(pair != own_pair)
    def _past():
        sweep(False, 0)
        sweep(False, 1)

    @pl.when((pair == own_pair - 1) | (own_pair == 0))
    def _last():
        outs = []
        for g in range(n_groups):
            inv = jnp.concatenate(
                [jnp.broadcast_to(l_ref[g:g + 1, hl * bq:(hl + 1) * bq], (HEAD_DIM, bq)) for hl in range(HEADS_PER_GROUP)],
                axis=0)
            outs.append(acc_ref[g] / inv)
        o_ref[...] = jnp.concatenate(outs, axis=0).T


def _moba_prompt(q, kb, vt, kmean, kp):
    t = q.shape[0]
    nb = t // MOBA_BLOCK
    assert nb % 2 == 0
    i_of = np.concatenate([np.full(i // 2 + 1, i, np.int32) for i in range(nb)])
    p_of = np.concatenate([np.concatenate([[i // 2], np.arange(i // 2)]).astype(np.int32) for i in range(nb)])
    n_groups = N_ATT_HEADS // HEADS_PER_GROUP
    wide = HEADS_PER_GROUP * MOBA_BLOCK
    grid_spec = pltpu.PrefetchScalarGridSpec(
        num_scalar_prefetch=2,
        grid=(len(i_of),),
        in_specs=[pl.BlockSpec((MOBA_BLOCK, ATT_WIDTH), lambda s, io, po: (io[s], 0)),
                  pl.BlockSpec((2 * MOBA_BLOCK, ATT_WIDTH), lambda s, io, po: (po[s], 0)),
                  pl.BlockSpec((ATT_WIDTH, 2 * MOBA_BLOCK), lambda s, io, po: (0, po[s])),
                  pl.BlockSpec((nb, ATT_WIDTH), lambda s, io, jo: (0, 0)),
                  pl.BlockSpec((n_groups, MOBA_BLOCK, wide), lambda s, io, jo: (0, 0, 0))],
        out_specs=pl.BlockSpec((MOBA_BLOCK, ATT_WIDTH), lambda s, io, jo: (io[s], 0)),
        scratch_shapes=[pltpu.VMEM((n_groups, GROUP_WIDTH, wide), BF16),
                        pltpu.VMEM((n_groups, nb, wide), F32),
                        pltpu.VMEM((n_groups, wide), F32),
                        pltpu.VMEM((n_groups, wide), F32),
                        pltpu.VMEM((n_groups, GROUP_WIDTH, MOBA_BLOCK), F32),
                        pltpu.VMEM((n_groups, MOBA_BLOCK, wide), F32)])
    return pl.pallas_call(
        _moba_body,
        grid_spec=grid_spec,
        out_shape=jax.ShapeDtypeStruct((t, ATT_WIDTH), F32),
        compiler_params=_params("arbitrary"),
        name="moba_prompt",
    )(jnp.asarray(i_of), jnp.asarray(p_of), q, kb, vt, kmean, kp)


def _decode_body(pt_ref, q_ref, kn_ref, vn_ref, *refs, pages, n_blocks, past_len, n_q):
    k_pages = refs[:pages]
    v_pages = refs[pages:2 * pages]
    o_ref = refs[2 * pages]
    gate_ref, op_ref, m_ref, l_ref = refs[2 * pages + 1:]
    step = pl.program_id(1)
    rows = N_ATT_HEADS * n_q
    head_of_row = lax.broadcasted_iota(jnp.int32, (rows, ATT_WIDTH), 0) % N_ATT_HEADS
    head_of_lane = lax.broadcasted_iota(jnp.int32, (rows, ATT_WIDTH), 1) // HEAD_DIM
    own_head = head_of_row == head_of_lane
    rid = lax.broadcasted_iota(jnp.int32, (rows, 1), 0)
    slope = jnp.zeros((rows, 1), F32)
    for h in range(N_ATT_HEADS):
        slope = jnp.where(rid % N_ATT_HEADS == h, _SLOPES[h], slope)
    qidx = rid // N_ATT_HEADS
    qpos = (past_len + qidx).astype(F32)
    blk_lane = lax.broadcasted_iota(jnp.int32, (rows, 128), 1)

    @pl.when(step == 0)
    def _init():
        m_ref[...] = jnp.full(m_ref.shape, MASK_VALUE, F32)
        l_ref[...] = jnp.zeros(l_ref.shape, F32)
        gate_ref[...] = jnp.zeros(gate_ref.shape, F32)

    qm_bf = jnp.where(own_head, q_ref[0] * (HEAD_DIM ** -0.5), 0.0).astype(BF16)
    for b in range(pages // 2):
        blk = step * (pages // 2) + b
        kt = jnp.concatenate([k_pages[2 * b][0], k_pages[2 * b + 1][0]], axis=1)
        vt = jnp.concatenate([v_pages[2 * b][0], v_pages[2 * b + 1][0]], axis=1)
        qk = _dot(qm_bf, kt.astype(BF16))
        gate_ref[...] = jnp.where(blk_lane == blk, jnp.sum(qk, axis=-1, keepdims=True) * (1.0 / MOBA_BLOCK),
                                  gate_ref[...])
        kpos = (blk * MOBA_BLOCK + lax.broadcasted_iota(jnp.int32, (1, MOBA_BLOCK), 1)).astype(F32)
        s = qk - slope * (qpos - kpos)
        m = jnp.max(s, axis=-1, keepdims=True)
        p = jnp.exp(s - m)
        m_ref[...] = jnp.where(blk_lane == blk, m, m_ref[...])
        l_ref[...] = jnp.where(blk_lane == blk, jnp.sum(p, axis=-1, keepdims=True), l_ref[...])
        op_ref[blk] = jnp.where(own_head, _dot_nt(p.astype(BF16), vt.astype(BF16)), 0.0)

    @pl.when(step == pl.num_programs(1) - 1)
    def _merge():
        sel = _select_bias(gate_ref[...], float(n_blocks), 1)
        kn = kn_ref[0]
        vn = vn_ref[0]
        kidx = lax.broadcasted_iota(jnp.int32, (rows, kn.shape[0]), 1)
        dist = (qidx - kidx).astype(F32)
        s_own = _dot_nt(qm_bf, kn.astype(BF16)) - slope * dist
        s_own = jnp.where(dist >= 0, s_own, MASK_VALUE)
        m_blk = m_ref[...] + sel
        m_all = jnp.maximum(jnp.max(m_blk, axis=-1, keepdims=True), jnp.max(s_own, axis=-1, keepdims=True))
        w = jnp.exp(m_blk - m_all)
        p_own = jnp.exp(s_own - m_all)
        den = jnp.sum(w * l_ref[...], axis=-1, keepdims=True) + jnp.sum(p_own, axis=-1, keepdims=True)
        acc = jnp.where(own_head, _dot(p_own.astype(BF16), vn.astype(BF16)), 0.0)
        for b in range(n_blocks):
            acc = acc + w[:, b:b + 1] * op_ref[b]
        acc = acc / den
        for qi in range(n_q):
            o_ref[0, qi:qi + 1, :] = jnp.sum(acc[qi * N_ATT_HEADS:(qi + 1) * N_ATT_HEADS], axis=0, keepdims=True)


def _moba_decode(q, k_new, v_new, cache_k, cache_v, page_table):
    b, n_q, _ = q.shape
    n_pages = page_table.shape[1]
    past_len = n_pages * PAGE_SIZE
    n_blocks = past_len // MOBA_BLOCK
    assert n_blocks <= 128 and n_pages % 2 == 0
    pages = next(p for p in (16, 8, 4, 2) if n_pages % p == 0)
    rows = N_ATT_HEADS * n_q
    q_rows = jnp.repeat(q, N_ATT_HEADS, axis=1)
    pad = ((0, 0), (0, OWN_ROWS - n_q), (0, 0))
    by_seq = lambda r: pl.BlockSpec((1, r, ATT_WIDTH), lambda bi, st, pt: (bi, 0, 0))

    def page_spec(t):
        return pl.BlockSpec((1, ATT_WIDTH, PAGE_SIZE), lambda bi, st, pt: (pt[bi, st * pages + t], 0, 0))

    grid_spec = pltpu.PrefetchScalarGridSpec(
        num_scalar_prefetch=1,
        grid=(b, n_pages // pages),
        in_specs=[by_seq(rows), by_seq(OWN_ROWS), by_seq(OWN_ROWS)] + [page_spec(t) for t in range(pages)] * 2,
        out_specs=by_seq(n_q),
        scratch_shapes=[pltpu.VMEM((rows, 128), F32),
                        pltpu.VMEM((n_blocks, rows, ATT_WIDTH), F32),
                        pltpu.VMEM((rows, 128), F32),
                        pltpu.VMEM((rows, 128), F32)])
    return pl.pallas_call(
        functools.partial(_decode_body, pages=pages, n_blocks=n_blocks, past_len=past_len, n_q=n_q),
        grid_spec=grid_spec,
        out_shape=jax.ShapeDtypeStruct((b, n_q, ATT_WIDTH), F32),
        compiler_params=_params("arbitrary", "arbitrary"),
        name="moba_decode",
    )(page_table, q_rows, jnp.pad(k_new, pad), jnp.pad(v_new, pad), *([cache_k] * pages), *([cache_v] * pages))


def _mix_body(attn_ref, ug_ref, vsn_ref, x_ref, wsg_ref, bsg_ref, ga_ref, gs_ref, wo_ref, gt_ref, sc_ref, sh_ref,
              gf_ref, wr_ref, br_ref, cin_ref, x1_ref, h2_ref, route_ref, cnt_ref, carry_ref):
    tm = x_ref.shape[0]

    @pl.when(pl.program_id(0) == 0)
    def _():
        carry_ref[...] = cin_ref[...]

    r = lax.broadcasted_iota(jnp.int32, (CHUNK, CHUNK), 0)
    c = lax.broadcasted_iota(jnp.int32, (CHUNK, CHUNK), 1)
    wcat = jnp.concatenate([jnp.where(c <= r, wsg_ref[g], 0.0).astype(BF16) for g in range(N_SGU_GROUPS)], axis=1)
    lane_group = lax.broadcasted_iota(jnp.int32, (CHUNK, SGU_WIDTH), 1) // HEAD_DIM
    mixed = []
    for ch in range(tm // CHUNK):
        vs = vsn_ref[ch * CHUNK:(ch + 1) * CHUNK, :]
        stack = jnp.concatenate([jnp.where(lane_group == g, vs, 0.0).astype(BF16) for g in range(N_SGU_GROUPS)],
                                axis=0)
        mixed.append(_dot(wcat, stack) + bsg_ref[...])
    sgu = ug_ref[...] * (mixed[0] if len(mixed) == 1 else jnp.concatenate(mixed, axis=0))
    merged = jnp.concatenate([_rms(attn_ref[...], ga_ref[...]), _rms(sgu, gs_ref[...])], axis=1)
    x1 = x_ref[...] + gt_ref[...] * _dot(merged.astype(BF16), wo_ref[...])
    x1_ref[...] = x1
    h2 = _rms(x1, gf_ref[...]) * (1.0 + sc_ref[...]) + sh_ref[...]
    h2_ref[...] = h2.reshape(tm, 1, D_MODEL)

    logits = _dot3(h2, wr_ref[...]) + br_ref[...]
    lane_f = lax.broadcasted_iota(jnp.int32, (tm, N_EXPERTS), 1).astype(F32)
    vals, hits = [], []
    work = logits
    for _ in range(TOP_K):
        mx = jnp.max(work, axis=-1, keepdims=True)
        first = jnp.min(jnp.where(work == mx, lane_f, float(N_EXPERTS)), axis=-1, keepdims=True)
        hit = lane_f == first
        vals.append(mx)
        hits.append(hit)
        work = jnp.where(hit, -jnp.inf, work)
    ex = [jnp.exp(v - vals[0]) for v in vals]
    den = ex[0] + ex[1] + ex[2] + ex[3]
    chosen = jnp.zeros((tm, N_EXPERTS), F32)
    for hit in hits:
        chosen = jnp.where(hit, 1.0, chosen)
    rr = lax.broadcasted_iota(jnp.int32, (tm, tm), 0)
    cc = lax.broadcasted_iota(jnp.int32, (tm, tm), 1)
    before = _dot(jnp.where(cc < rr, 1.0, 0.0).astype(BF16), chosen.astype(BF16)) + carry_ref[...]
    rl = lax.broadcasted_iota(jnp.int32, (tm, ROUTE_LANES), 1)
    route = jnp.zeros((tm, ROUTE_LANES), F32)
    for t in range(TOP_K):
        idx = jnp.sum(jnp.where(hits[t], lane_f, 0.0), axis=-1, keepdims=True)
        rank = jnp.sum(jnp.where(hits[t], before, 0.0), axis=-1, keepdims=True)
        route = jnp.where(rl == t, idx, route)
        route = jnp.where(rl == TOP_K + t, ex[t] / den, route)
        route = jnp.where(rl == 2 * TOP_K + t, rank, route)
    route_ref[...] = route
    carry_ref[...] = carry_ref[...] + jnp.sum(chosen, axis=0, keepdims=True)
    cnt_ref[...] = carry_ref[...]


def _mix(attn, ug, vsn, x, wsg, bsg, g_att, g_sgu, w_out_bf, gt, sc, sh, g_ffn, w_router, b_router, cnt_in, tm):
    n = x.shape[0]
    mod_rows = gt.shape[0]
    mod_spec = (pl.BlockSpec((1, D_MODEL), lambda i: (0, 0)) if mod_rows == 1
                else pl.BlockSpec((tm, D_MODEL), lambda i: (i, 0)))
    row = lambda w: pl.BlockSpec((tm, w), lambda i: (i, 0))
    full = lambda *shape: pl.BlockSpec(shape, lambda i: (0,) * len(shape))
    return pl.pallas_call(
        _mix_body,
        grid=(n // tm,),
        in_specs=[row(ATT_WIDTH), row(SGU_WIDTH), row(SGU_WIDTH), row(D_MODEL),
                  full(N_SGU_GROUPS, CHUNK, CHUNK), full(CHUNK, SGU_WIDTH),
                  full(1, ATT_WIDTH), full(1, SGU_WIDTH), full(D_MODEL, D_MODEL),
                  mod_spec, mod_spec, mod_spec,
                  full(1, D_MODEL), full(D_MODEL, N_EXPERTS), full(1, N_EXPERTS), full(1, N_EXPERTS)],
        out_specs=[row(D_MODEL), pl.BlockSpec((tm, 1, D_MODEL), lambda i: (i, 0, 0)), row(ROUTE_LANES),
                   full(1, N_EXPERTS)],
        out_shape=[jax.ShapeDtypeStruct((n, D_MODEL), F32), jax.ShapeDtypeStruct((n, 1, D_MODEL), F32),
                   jax.ShapeDtypeStruct((n, ROUTE_LANES), F32), jax.ShapeDtypeStruct((1, N_EXPERTS), F32)],
        scratch_shapes=[pltpu.VMEM((1, N_EXPERTS), F32)],
        compiler_params=_params("arbitrary"),
        name="mix",
    )(attn, ug, vsn, x, wsg, bsg, g_att.reshape(1, ATT_WIDTH), g_sgu.reshape(1, SGU_WIDTH), w_out_bf,
      gt, sc, sh, g_ffn.reshape(1, D_MODEL), w_router, b_router.reshape(1, N_EXPERTS), cnt_in)


def _experts_body(tile_e, tile_rows, n_used, src_ref, src_next_ref, dst_ref, h2p_ref, h2s_ref, wgu_ref, bgu_ref,
                  wd_ref, bd_ref, y_ref, xbuf, obuf, gsem, ssem, wgu_bf, wd_bf, *, n_prompt):
    t = pl.program_id(0)
    used = n_used[0]
    slot = t % 2

    def start_gather(src, rows, into):
        def body(r, carry):
            tok = src[r]

            @pl.when((r < rows) & (tok < n_prompt))
            def _():
                pltpu.make_async_copy(h2p_ref.at[pl.ds(tok, 1)], xbuf.at[into, pl.ds(r, 1)], gsem.at[into]).start()

            @pl.when((r < rows) & (tok >= n_prompt))
            def _():
                pltpu.make_async_copy(h2s_ref.at[pl.ds(tok - n_prompt, 1)], xbuf.at[into, pl.ds(r, 1)],
                                      gsem.at[into]).start()
            return carry
        lax.fori_loop(0, MOE_TILE, body, 0, unroll=DMA_UNROLL)

    def wait_gather(rows, into):
        pltpu.make_async_copy(h2p_ref.at[pl.ds(0, rows)], xbuf.at[into, pl.ds(0, rows)], gsem.at[into]).wait()

    def start_scatter(rows, frm):
        def body(r, carry):
            @pl.when(r < rows)
            def _():
                pltpu.make_async_copy(obuf.at[frm, pl.ds(r, 1)], y_ref.at[pl.ds(dst_ref[r], 1)], ssem.at[frm]).start()
            return carry
        lax.fori_loop(0, MOE_TILE, body, 0, unroll=DMA_UNROLL)

    def wait_scatter(rows, frm):
        pltpu.make_async_copy(obuf.at[frm, pl.ds(0, rows)], y_ref.at[pl.ds(0, rows)], ssem.at[frm]).wait()

    @pl.when(t == 0)
    def _():
        xbuf[...] = jnp.zeros(xbuf.shape, F32)
        start_gather(src_ref, tile_rows[0], 0)

    @pl.when(t + 1 < used)
    def _():
        start_gather(src_next_ref, tile_rows[t + 1], 1 - slot)

    e = tile_e[t]
    prev = tile_e[jnp.maximum(t - 1, 0)]

    @pl.when((t == 0) | (e != prev))
    def _():
        wgu_bf[...] = wgu_ref[0].astype(BF16)
        wd_bf[...] = wd_ref[0].astype(BF16)

    @pl.when(t < used)
    def _():
        wait_gather(tile_rows[t], slot)

        @pl.when(t >= 2)
        def _():
            wait_scatter(tile_rows[jnp.maximum(t - 2, 0)], slot)

        x = xbuf[slot].reshape(MOE_TILE, D_MODEL)
        gu = _dot(x.astype(BF16), wgu_bf[...]) + bgu_ref[0]
        g = jnp.minimum(gu[:, :D_FF], SWIGLU_LIMIT)
        lin = jnp.clip(gu[:, D_FF:], -SWIGLU_LIMIT, SWIGLU_LIMIT)
        act = g * (1.0 / (1.0 + jnp.exp(-SWIGLU_ALPHA * g))) * (lin + 1.0)
        out = _dot(act.astype(BF16), wd_bf[...]) + bd_ref[0]
        obuf[slot] = out.reshape(MOE_TILE, 1, D_MODEL)
        start_scatter(tile_rows[t], slot)

        @pl.when(t == used - 1)
        def _():
            @pl.when(t >= 1)
            def _():
                wait_scatter(tile_rows[jnp.maximum(t - 1, 0)], 1 - slot)
            wait_scatter(tile_rows[t], slot)


def _experts(h2_p, h2_s, slot_src, slot_dst, tile_e, tile_rows, n_used, w_gu, b_gu, w_down, b_down):
    n_prompt = h2_p.shape[0]
    n_tok = n_prompt + h2_s.shape[0]
    n_tiles = slot_src.shape[0] // MOE_TILE
    by_e = lambda *blk: pl.BlockSpec((1,) + blk, lambda t, te, tr, nu: (te[t],) + (0,) * len(blk))
    hbm = pl.BlockSpec(memory_space=pl.ANY)
    this_tile = pl.BlockSpec((MOE_TILE,), lambda t, *_: (t,), memory_space=pltpu.SMEM)
    grid_spec = pltpu.PrefetchScalarGridSpec(
        num_scalar_prefetch=3,
        grid=(n_tiles,),
        in_specs=[this_tile,
                  pl.BlockSpec((MOE_TILE,), lambda t, *_: (jnp.minimum(t + 1, n_tiles - 1),), memory_space=pltpu.SMEM),
                  this_tile,
                  hbm, hbm,
                  by_e(D_MODEL, 2 * D_FF), by_e(1, 2 * D_FF), by_e(D_FF, D_MODEL), by_e(1, D_MODEL)],
        out_specs=hbm,
        scratch_shapes=[pltpu.VMEM((2, MOE_TILE, 1, D_MODEL), F32), pltpu.VMEM((2, MOE_TILE, 1, D_MODEL), F32),
                        pltpu.SemaphoreType.DMA((2,)), pltpu.SemaphoreType.DMA((2,)),
                        pltpu.VMEM((D_MODEL, 2 * D_FF), BF16), pltpu.VMEM((D_FF, D_MODEL), BF16)])
    return pl.pallas_call(
        functools.partial(_experts_body, n_prompt=n_prompt),
        grid_spec=grid_spec,
        out_shape=jax.ShapeDtypeStruct((TOP_K * n_tok, 1, D_MODEL), F32),
        compiler_params=_params("arbitrary"),
        name="experts",
    )(tile_e, tile_rows, n_used, slot_src, slot_src, slot_dst, h2_p, h2_s, w_gu,
      b_gu.reshape(N_EXPERTS, 1, 2 * D_FF), w_down, b_down.reshape(N_EXPERTS, 1, D_MODEL))


def _combine_body(*refs):
    y_refs = refs[:TOP_K]
    x1_ref, route_ref, gt_ref, gfin_ref, o_ref = refs[TOP_K:]
    tc = x1_ref.shape[0]
    route = route_ref[...]
    moe = route[:, TOP_K:TOP_K + 1] * y_refs[0][...].reshape(tc, D_MODEL)
    for t in range(1, TOP_K):
        moe = moe + route[:, TOP_K + t:TOP_K + t + 1] * y_refs[t][...].reshape(tc, D_MODEL)
    o_ref[...] = _rms(x1_ref[...] + gt_ref[...] * moe, gfin_ref[...])


def _combine(ys, first_tok, x1, route, gt, g_final):
    n = x1.shape[0]
    tc = COMBINE_TOKENS
    n_tok = ys.shape[0] // TOP_K
    assert n_tok % tc == 0 and first_tok % tc == 0
    mod_spec = (pl.BlockSpec((1, D_MODEL), lambda i: (0, 0)) if gt.shape[0] == 1
                else pl.BlockSpec((tc, D_MODEL), lambda i: (i, 0)))

    def y_spec(k):
        return pl.BlockSpec((tc, 1, D_MODEL), lambda i: ((k * n_tok + first_tok) // tc + i, 0, 0))

    return pl.pallas_call(
        _combine_body,
        grid=(n // tc,),
        in_specs=[y_spec(k) for k in range(TOP_K)] + [
            pl.BlockSpec((tc, D_MODEL), lambda i: (i, 0)),
            pl.BlockSpec((tc, ROUTE_LANES), lambda i: (i, 0)),
            mod_spec,
            pl.BlockSpec((1, D_MODEL), lambda i: (0, 0))],
        out_specs=pl.BlockSpec((tc, D_MODEL), lambda i: (i, 0)),
        out_shape=jax.ShapeDtypeStruct((n, D_MODEL), F32),
        compiler_params=_params("arbitrary"),
        name="combine",
    )(*([ys] * TOP_K), x1, route, gt, g_final.reshape(1, D_MODEL))


def _pages_keys_minor(cache):
    return cache.transpose(0, 2, 3, 1).reshape(cache.shape[0], ATT_WIDTH, PAGE_SIZE)


def _group_mean_matrix():
    g = np.arange(SGU_WIDTH) // HEAD_DIM
    return jnp.asarray((g[:, None] == g[None, :]).astype(np.float32) / HEAD_DIM, BF16)


def _key_offset_bias():
    off = np.arange(MOBA_BLOCK, dtype=np.float32)[None, :, None]
    per_head = np.broadcast_to(np.asarray(_SLOPES, np.float32)[:, None, None] * off,
                               (N_ATT_HEADS, MOBA_BLOCK, MOBA_BLOCK))
    n_groups = N_ATT_HEADS // HEADS_PER_GROUP
    grouped = per_head.reshape(n_groups, HEADS_PER_GROUP, MOBA_BLOCK, MOBA_BLOCK).transpose(0, 2, 1, 3)
    return jnp.asarray(grouped.reshape(n_groups, MOBA_BLOCK, HEADS_PER_GROUP * MOBA_BLOCK))


def kernel(x_prompt, x_sample, cache_k, cache_v, page_table, c_prompt, c_sample, g_mix, w_ada, b_ada, w_in,
           g_sgu, w_sgu, b_sgu, g_out_att, g_out_sgu, w_out, g_ffn, w_router, b_router, w_gu, b_gu, w_down,
           b_down, g_final):
    depth = w_in.shape[0]
    batch, seq, _ = x_prompt.shape
    dec_batch, dec_seq, _ = x_sample.shape
    assert depth == 1, "the final norm is fused into the last stage of the single layer"
    assert batch == 1 and seq % MOBA_BLOCK == 0
    n_dec = dec_batch * dec_seq
    assert n_dec % CHUNK == 0 and CHUNK % dec_seq == 0 and dec_seq <= MOBA_BLOCK
    n_tok = seq + n_dec
    n_pool = cache_k.shape[1]

    bd = _group_mean_matrix()
    kp = _key_offset_bias()
    xp = x_prompt.reshape(seq, D_MODEL)
    xs = x_sample.reshape(n_dec, D_MODEL)
    c_all = jnp.concatenate([c_prompt, c_sample], axis=0)
    c_rows = -(-c_all.shape[0] // 8) * 8
    c_all = jnp.pad(c_all, ((0, c_rows - c_all.shape[0]), (0, 0)))
    k_rows, v_rows, ks_rows, vs_rows, sgu_rows = [], [], [], [], []

    for l in range(depth):
        mod = _ada(c_all, w_ada[l], b_ada[l])
        mod_p = [mod[0:1, t * D_MODEL:(t + 1) * D_MODEL] for t in range(6)]
        mod_rows = jnp.broadcast_to(mod[1:1 + dec_batch, None, :], (dec_batch, dec_seq, 6 * D_MODEL))
        mod_rows = mod_rows.reshape(n_dec, 6 * D_MODEL)
        mod_s = [mod_rows[:, t * D_MODEL:(t + 1) * D_MODEL] for t in range(6)]
        w_in_bf = w_in[l].astype(BF16)
        w_out_bf = w_out[l].astype(BF16)
        gs_flat = g_sgu[l].reshape(SGU_WIDTH)

        q, k, v, ug, vsn, kb, vt, kmean = _inproj(xp, g_mix[l], mod_p[1], mod_p[0], w_in_bf, gs_flat, bd, True)
        attn = _moba_prompt(q, kb, vt, kmean.reshape(seq // MOBA_BLOCK, ATT_WIDTH), kp)
        bias_p = jnp.repeat(b_sgu[l].T, HEAD_DIM, axis=1)
        x1_p, h2_p, route_p, cnt_p = _mix(attn, ug, vsn, xp, w_sgu[l], bias_p, g_out_att[l], g_out_sgu[l],
                                          w_out_bf, mod_p[2], mod_p[4], mod_p[3], g_ffn[l], w_router[l],
                                          b_router[l], jnp.zeros((1, N_EXPERTS), F32), MOBA_BLOCK)
        k_rows.append(k)
        v_rows.append(v)

        qs, ks, vs, ugs, vsns = _inproj(xs, g_mix[l], mod_s[1], mod_s[0], w_in_bf, gs_flat, bd, False)
        attn_s = _moba_decode(qs.reshape(dec_batch, dec_seq, ATT_WIDTH), ks.reshape(dec_batch, dec_seq, ATT_WIDTH),
                              vs.reshape(dec_batch, dec_seq, ATT_WIDTH),
                              _pages_keys_minor(cache_k[l]), _pages_keys_minor(cache_v[l]), page_table)
        eye = jnp.eye(CHUNK // dec_seq, dtype=F32)
        w_s = jax.vmap(lambda w: jnp.kron(eye, w))(w_sgu[l][:, :dec_seq, :dec_seq])
        bias_s = jnp.tile(jnp.repeat(b_sgu[l].T[:dec_seq], HEAD_DIM, axis=1), (CHUNK // dec_seq, 1))
        x1_s, h2_s, route_s, cnt = _mix(attn_s.reshape(n_dec, ATT_WIDTH), ugs, vsns, xs, w_s, bias_s, g_out_att[l],
                                        g_out_sgu[l], w_out_bf, mod_s[2], mod_s[4], mod_s[3], g_ffn[l],
                                        w_router[l], b_router[l], cnt_p, CHUNK)
        ks_rows.append(ks)
        vs_rows.append(vs)
        sgu_rows.append(vsns)

        route = jnp.concatenate([route_p, route_s], axis=0)
        e_idx = route[:, :TOP_K].astype(jnp.int32)
        rank = route[:, 2 * TOP_K:3 * TOP_K].astype(jnp.int32)
        cnt_i = cnt.reshape(N_EXPERTS).astype(jnp.int32)
        pcnt = (cnt_i + MOE_TILE - 1) // MOE_TILE * MOE_TILE
        pend = jnp.cumsum(pcnt)
        pstart = pend - pcnt
        dest = (pstart[e_idx] + rank).reshape(-1)
        n_tiles = -(-(n_tok * TOP_K) // MOE_TILE) + N_EXPERTS
        tile_start = jnp.arange(n_tiles, dtype=jnp.int32) * MOE_TILE
        tile_e = jnp.minimum(jnp.sum((pend[None, :] <= tile_start[:, None]).astype(jnp.int32), axis=1), N_EXPERTS - 1)
        n_used = (pend[-1:] // MOE_TILE).astype(jnp.int32)
        tile_rows = jnp.clip((pstart + cnt_i)[tile_e] - tile_start, 0, MOE_TILE).astype(jnp.int32)
        slot_assign = jnp.zeros((n_tiles * MOE_TILE,), jnp.int32).at[dest].set(
            jnp.arange(n_tok * TOP_K, dtype=jnp.int32))
        slot_src = slot_assign // TOP_K
        slot_dst = (slot_assign % TOP_K) * n_tok + slot_src

        ys = _experts(h2_p, h2_s, slot_src, slot_dst, tile_e, tile_rows, n_used, w_gu[l], b_gu[l], w_down[l],
                      b_down[l])
        xp = _combine(ys, 0, x1_p, route_p, mod_p[5], g_final)
        xs = _combine(ys, seq, x1_s, route_s, mod_s[5], g_final)

    hd = (N_ATT_HEADS, HEAD_DIM)
    k_p = jnp.stack(k_rows).reshape(depth, batch, seq // PAGE_SIZE, PAGE_SIZE, *hd)
    v_p = jnp.stack(v_rows).reshape(depth, batch, seq // PAGE_SIZE, PAGE_SIZE, *hd)
    k_s = jnp.stack(ks_rows).reshape(depth, dec_batch, dec_seq, *hd)
    v_s = jnp.stack(vs_rows).reshape(depth, dec_batch, dec_seq, *hd)
    sgu_v = jnp.stack(sgu_rows).reshape(depth, dec_batch, dec_seq, N_SGU_GROUPS, HEAD_DIM)
    return (xp.reshape(batch, seq, D_MODEL), xs.reshape(dec_batch, dec_seq, D_MODEL), k_p, v_p, k_s, v_s, sgu_v)
```

```python
import functools

import numpy as np
import jax
import jax.numpy as jnp
from jax import lax
from jax.experimental import pallas as pl
from jax.experimental.pallas import tpu as pltpu

F32 = jnp.float32
BF16 = jnp.bfloat16

D_MODEL = 1024
HEAD_DIM = 64
N_ATT_HEADS = 8
N_SGU_GROUPS = 8
ATT_WIDTH = N_ATT_HEADS * HEAD_DIM
SGU_WIDTH = N_SGU_GROUPS * HEAD_DIM
IN_WIDTH = 3 * ATT_WIDTH + 2 * SGU_WIDTH
CHUNK = 128
MOBA_BLOCK = 256
MOBA_TOPK = 3
PAGE_SIZE = 128
N_EXPERTS = 32
TOP_K = 4
D_FF = D_MODEL
SWIGLU_LIMIT = 7.0
SWIGLU_ALPHA = 1.702
EPS = 1e-6
MASK_VALUE = -1e30

HEADS_PER_GROUP = 4
GROUP_WIDTH = HEADS_PER_GROUP * HEAD_DIM
ROUTE_LANES = 128
MOE_TILE = 256
DMA_UNROLL = 8
COMBINE_TOKENS = 128
OWN_ROWS = 16
VMEM_LIMIT = 56 * 1024 * 1024

_SLOPES = [2.0 ** (-8.0 * (h + 1) / N_ATT_HEADS) for h in range(N_ATT_HEADS)]


def _params(*sem, flags=None):
    return pltpu.CompilerParams(dimension_semantics=sem, vmem_limit_bytes=VMEM_LIMIT, flags=flags)


def _split(a):
    hi = a.astype(BF16)
    lo = (a - hi.astype(F32)).astype(BF16)
    return hi, lo


def _dot(a, b):
    return jnp.dot(a, b, preferred_element_type=F32)


def _dot_nt(a, b):
    return lax.dot_general(a, b, (((1,), (1,)), ((), ())), preferred_element_type=F32)


def _dot3(a, b, dot=_dot):
    a_hi, a_lo = _split(a)
    b_hi, b_lo = _split(b)
    return dot(a_hi, b_hi) + (dot(a_hi, b_lo) + dot(a_lo, b_hi))


def _rms(x, g):
    return x * lax.rsqrt(jnp.mean(x * x, axis=-1, keepdims=True) + EPS) * g


def _gelu(x):
    return x * (0.5 * (1.0 + jnp.tanh(0.7978845608028654 * (x + 0.044715 * (x * x * x)))))


ROW_TILE = D_MODEL // 128


def _store_token_tiles(ref, x):
    for c in range(ROW_TILE):
        ref[pl.ds(c, x.shape[0], stride=ROW_TILE), :] = x[:, c * 128:(c + 1) * 128]


def _load_token_tiles(ref, rows):
    return jnp.concatenate([ref[pl.ds(c, rows, stride=ROW_TILE), :] for c in range(ROW_TILE)], axis=1)


def _tile_of(ref, token):
    return ref.at[pl.ds(pl.multiple_of(token * ROW_TILE, ROW_TILE), ROW_TILE)]


def _tiles_of(ref, n_tokens):
    return ref.at[pl.ds(0, pl.multiple_of(n_tokens * ROW_TILE, ROW_TILE))]


def _ada_body(c_ref, w_ref, b_ref, o_ref):
    c = c_ref[...]
    s = c * (1.0 / (1.0 + jnp.exp(-c)))
    o_ref[...] = _dot3(s, w_ref[...]) + b_ref[...]


def _ada(c, w, b):
    rows = c.shape[0]
    n = w.shape[1]
    tn = 1536
    return pl.pallas_call(
        _ada_body,
        grid=(n // tn,),
        in_specs=[pl.BlockSpec((rows, D_MODEL), lambda j: (0, 0)),
                  pl.BlockSpec((D_MODEL, tn), lambda j: (0, j)),
                  pl.BlockSpec((1, tn), lambda j: (0, j))],
        out_specs=pl.BlockSpec((rows, tn), lambda j: (0, j)),
        out_shape=jax.ShapeDtypeStruct((rows, n), F32),
        compiler_params=_params("arbitrary"),
        name="ada",
    )(c, w, b.reshape(1, n))


def _inproj_body(x_ref, g_ref, sc_ref, sh_ref, w_ref, gs_ref, bd_ref, q_ref, k_ref, v_ref, ug_ref, vsn_ref,
                 *prompt_refs):
    h = _rms(x_ref[...], g_ref[...]) * (1.0 + sc_ref[...]) + sh_ref[...]
    z = _dot(h.astype(BF16), w_ref[...])
    a = ATT_WIDTH
    q_ref[...] = z[:, :a]
    k = z[:, a:2 * a]
    v = z[:, 2 * a:3 * a]
    k_ref[...] = k
    v_ref[...] = v
    ug_ref[...] = _gelu(z[:, 3 * a:3 * a + SGU_WIDTH])
    vg = _gelu(z[:, 3 * a + SGU_WIDTH:])
    sq_hi, sq_lo = _split(vg * vg)
    gmean = _dot(sq_hi, bd_ref[...]) + _dot(sq_lo, bd_ref[...])
    vsn_ref[...] = vg * lax.rsqrt(gmean + EPS) * gs_ref[...]
    if prompt_refs:
        kb_ref, vt_ref, km_ref = prompt_refs
        kb_ref[...] = k.astype(BF16)
        vt_ref[...] = v.T.astype(BF16)
        km_ref[0] = jnp.mean(k, axis=0, keepdims=True)


def _inproj(x, g_mix, sc, sh, w_in_bf, g_sgu, bd, prompt):
    n = x.shape[0]
    tm = MOBA_BLOCK if prompt else n
    mod_rows = sc.shape[0]
    mod_spec = (pl.BlockSpec((1, D_MODEL), lambda i: (0, 0)) if mod_rows == 1
                else pl.BlockSpec((tm, D_MODEL), lambda i: (i, 0)))
    row = lambda w: pl.BlockSpec((tm, w), lambda i: (i, 0))
    out_specs = [row(ATT_WIDTH)] * 3 + [row(SGU_WIDTH)] * 2
    out_shape = [jax.ShapeDtypeStruct((n, ATT_WIDTH), F32)] * 3 + [jax.ShapeDtypeStruct((n, SGU_WIDTH), F32)] * 2
    if prompt:
        out_specs += [row(ATT_WIDTH), pl.BlockSpec((ATT_WIDTH, tm), lambda i: (0, i)),
                      pl.BlockSpec((1, 1, ATT_WIDTH), lambda i: (i, 0, 0))]
        out_shape += [jax.ShapeDtypeStruct((n, ATT_WIDTH), BF16), jax.ShapeDtypeStruct((ATT_WIDTH, n), BF16),
                      jax.ShapeDtypeStruct((n // tm, 1, ATT_WIDTH), F32)]
    return pl.pallas_call(
        _inproj_body,
        grid=(n // tm,),
        in_specs=[row(D_MODEL),
                  pl.BlockSpec((1, D_MODEL), lambda i: (0, 0)),
                  mod_spec, mod_spec,
                  pl.BlockSpec((D_MODEL, IN_WIDTH), lambda i: (0, 0)),
                  pl.BlockSpec((1, SGU_WIDTH), lambda i: (0, 0)),
                  pl.BlockSpec((SGU_WIDTH, SGU_WIDTH), lambda i: (0, 0))],
        out_specs=out_specs,
        out_shape=out_shape,
        compiler_params=_params("arbitrary"),
        name="inproj_prompt" if prompt else "inproj_sample",
    )(x, g_mix.reshape(1, D_MODEL), sc, sh, w_in_bf, g_sgu.reshape(1, SGU_WIDTH), bd)


def _select_bias(gate, n_past, axis):
    blk = lax.broadcasted_iota(jnp.int32, gate.shape, axis).astype(F32)
    past = blk < n_past
    gate = jnp.where(past, gate, -jnp.inf)
    bias = jnp.full(gate.shape, MASK_VALUE, F32)
    for _ in range(MOBA_TOPK):
        mx = jnp.max(gate, axis=axis, keepdims=True)
        first = jnp.min(jnp.where(gate == mx, blk, float(gate.shape[axis])), axis=axis, keepdims=True)
        hit = blk == first
        bias = jnp.where(hit, 0.0, bias)
        gate = jnp.where(hit, -jnp.inf, gate)
    return jnp.where(past, bias, MASK_VALUE)


def _moba_body(i_of, p_of, q_ref, kb_ref, vt_ref, km_ref, kp_ref, o_ref, qm_ref, sel_ref, m_ref, l_ref, acc_ref,
               s_ref):
    s = pl.program_id(0)
    i = i_of[s]
    pair = p_of[s]
    own_pair = i // 2
    bq = MOBA_BLOCK
    wide = HEADS_PER_GROUP * bq
    n_groups = N_ATT_HEADS // HEADS_PER_GROUP
    head_of_row = lax.broadcasted_iota(jnp.int32, (GROUP_WIDTH, bq), 0) // HEAD_DIM

    def sweep(diag, half):
        j = 2 * pair + half
        keys = slice(half * bq, (half + 1) * bq)
        col = lax.broadcasted_iota(jnp.int32, (1, wide), 1)
        dist0 = ((i - j) * MOBA_BLOCK + col % bq).astype(F32)
        if diag:
            key = lax.broadcasted_iota(jnp.int32, (bq, wide), 0)
            qry = lax.broadcasted_iota(jnp.int32, (bq, wide), 1) % bq
            causal = jnp.where(key <= qry, 0.0, MASK_VALUE)
        for g in range(n_groups):
            slope = jnp.zeros((1, wide), F32)
            for hl in range(HEADS_PER_GROUP):
                slope = jnp.where(col // bq == hl, _SLOPES[g * HEADS_PER_GROUP + hl], slope)
            lanes = slice(g * GROUP_WIDTH, (g + 1) * GROUP_WIDTH)
            s0 = _dot(kb_ref[keys, lanes], qm_ref[g]) + kp_ref[g]
            brow = -slope * dist0
            if diag:
                s0 = s0 + causal
                m_old = jnp.full((1, wide), MASK_VALUE, F32)
                l_old = jnp.zeros((1, wide), F32)
            else:
                brow = brow + sel_ref[g, pl.ds(j, 1), :]
                m_old = m_ref[g:g + 1, :]
                l_old = l_ref[g:g + 1, :]
            s_ref[g] = s0
            m_new = jnp.maximum(m_old, jnp.max(s_ref[g], axis=0, keepdims=True) + brow)
            p32 = jnp.exp(s_ref[g] - (m_new - brow))
            alpha = jnp.exp(m_old - m_new)
            l_ref[g:g + 1, :] = alpha * l_old + jnp.sum(p32, axis=0, keepdims=True)
            m_ref[g:g + 1, :] = m_new
            p = p32.astype(BF16)
            vt = vt_ref[lanes, keys]
            vt_heads = jnp.concatenate([jnp.where(head_of_row == hl, vt, jnp.zeros_like(vt))
                                        for hl in range(HEADS_PER_GROUP)], axis=1)
            p_heads = jnp.concatenate([p[:, hl * bq:(hl + 1) * bq] for hl in range(HEADS_PER_GROUP)], axis=0)
            pv = _dot(vt_heads, p_heads)
            if diag:
                acc_ref[g] = pv
            else:
                alpha_rows = jnp.concatenate(
                    [jnp.broadcast_to(alpha[:, hl * bq:(hl + 1) * bq], (HEAD_DIM, bq)) for hl in range(HEADS_PER_GROUP)],
                    axis=0)
                acc_ref[g] = alpha_rows * acc_ref[g] + pv

    @pl.when(pair == own_pair)
    def _first():
        qt = (q_ref[...] * (HEAD_DIM ** -0.5)).T
        for g in range(n_groups):
            lanes = slice(g * GROUP_WIDTH, (g + 1) * GROUP_WIDTH)
            for hl in range(HEADS_PER_GROUP):
                qm = jnp.where(head_of_row == hl, qt[lanes, :], 0.0)
                qm_ref[g, :, hl * bq:(hl + 1) * bq] = qm.astype(BF16)
                gate = _dot3(km_ref[:, lanes], qm)
                sel_ref[g, :, hl * bq:(hl + 1) * bq] = _select_bias(gate, i.astype(F32), 0)

    @pl.when((pair == own_pair) & (i % 2 == 0))
    def _own_then_future():
        sweep(True, 0)

    @pl.when((pair == own_pair) & (i % 2 == 1))
    def _own_then_past():
        sweep(True, 1)
        sweep(False, 0)

    @pl.when(pair != own_pair)
    def _past():
        sweep(False, 0)
        sweep(False, 1)

    @pl.when((pair == own_pair - 1) | (own_pair == 0))
    def _last():
        outs = []
        for g in range(n_groups):
            inv = jnp.concatenate(
                [jnp.broadcast_to(l_ref[g:g + 1, hl * bq:(hl + 1) * bq], (HEAD_DIM, bq)) for hl in range(HEADS_PER_GROUP)],
                axis=0)
            outs.append(acc_ref[g] / inv)
        o_ref[...] = jnp.concatenate(outs, axis=0).T


def _moba_prompt(q, kb, vt, kmean, kp):
    t = q.shape[0]
    nb = t // MOBA_BLOCK
    assert nb % 2 == 0
    i_of = np.concatenate([np.full(i // 2 + 1, i, np.int32) for i in range(nb)])
    p_of = np.concatenate([np.concatenate([[i // 2], np.arange(i // 2)]).astype(np.int32) for i in range(nb)])
    n_groups = N_ATT_HEADS // HEADS_PER_GROUP
    wide = HEADS_PER_GROUP * MOBA_BLOCK
    grid_spec = pltpu.PrefetchScalarGridSpec(
        num_scalar_prefetch=2,
        grid=(len(i_of),),
        in_specs=[pl.BlockSpec((MOBA_BLOCK, ATT_WIDTH), lambda s, io, po: (io[s], 0)),
                  pl.BlockSpec((2 * MOBA_BLOCK, ATT_WIDTH), lambda s, io, po: (po[s], 0)),
                  pl.BlockSpec((ATT_WIDTH, 2 * MOBA_BLOCK), lambda s, io, po: (0, po[s])),
                  pl.BlockSpec((nb, ATT_WIDTH), lambda s, io, jo: (0, 0)),
                  pl.BlockSpec((n_groups, MOBA_BLOCK, wide), lambda s, io, jo: (0, 0, 0))],
        out_specs=pl.BlockSpec((MOBA_BLOCK, ATT_WIDTH), lambda s, io, jo: (io[s], 0)),
        scratch_shapes=[pltpu.VMEM((n_groups, GROUP_WIDTH, wide), BF16),
                        pltpu.VMEM((n_groups, nb, wide), F32),
                        pltpu.VMEM((n_groups, wide), F32),
                        pltpu.VMEM((n_groups, wide), F32),
                        pltpu.VMEM((n_groups, GROUP_WIDTH, MOBA_BLOCK), F32),
                        pltpu.VMEM((n_groups, MOBA_BLOCK, wide), F32)])
    return pl.pallas_call(
        _moba_body,
        grid_spec=grid_spec,
        out_shape=jax.ShapeDtypeStruct((t, ATT_WIDTH), F32),
        compiler_params=_params("arbitrary"),
        name="moba_prompt",
    )(jnp.asarray(i_of), jnp.asarray(p_of), q, kb, vt, kmean, kp)


def _decode_body(pt_ref, q_ref, kn_ref, vn_ref, *refs, pages, n_blocks, past_len, n_q):
    k_pages = refs[:pages]
    v_pages = refs[pages:2 * pages]
    o_ref = refs[2 * pages]
    gate_ref, op_ref, m_ref, l_ref = refs[2 * pages + 1:]
    step = pl.program_id(1)
    rows = N_ATT_HEADS * n_q
    head_of_row = lax.broadcasted_iota(jnp.int32, (rows, ATT_WIDTH), 0) % N_ATT_HEADS
    head_of_lane = lax.broadcasted_iota(jnp.int32, (rows, ATT_WIDTH), 1) // HEAD_DIM
    own_head = head_of_row == head_of_lane
    rid = lax.broadcasted_iota(jnp.int32, (rows, 1), 0)
    slope = jnp.zeros((rows, 1), F32)
    for h in range(N_ATT_HEADS):
        slope = jnp.where(rid % N_ATT_HEADS == h, _SLOPES[h], slope)
    qidx = rid // N_ATT_HEADS
    qpos = (past_len + qidx).astype(F32)
    blk_lane = lax.broadcasted_iota(jnp.int32, (rows, 128), 1)

    @pl.when(step == 0)
    def _init():
        m_ref[...] = jnp.full(m_ref.shape, MASK_VALUE, F32)
        l_ref[...] = jnp.zeros(l_ref.shape, F32)
        gate_ref[...] = jnp.zeros(gate_ref.shape, F32)

    qm_bf = jnp.where(own_head, q_ref[0] * (HEAD_DIM ** -0.5), 0.0).astype(BF16)
    for b in range(pages // 2):
        blk = step * (pages // 2) + b
        kt = jnp.concatenate([k_pages[2 * b][0], k_pages[2 * b + 1][0]], axis=1)
        vt = jnp.concatenate([v_pages[2 * b][0], v_pages[2 * b + 1][0]], axis=1)
        qk = _dot(qm_bf, kt.astype(BF16))
        gate_ref[...] = jnp.where(blk_lane == blk, jnp.sum(qk, axis=-1, keepdims=True) * (1.0 / MOBA_BLOCK),
                                  gate_ref[...])
        kpos = (blk * MOBA_BLOCK + lax.broadcasted_iota(jnp.int32, (1, MOBA_BLOCK), 1)).astype(F32)
        s = qk - slope * (qpos - kpos)
        m = jnp.max(s, axis=-1, keepdims=True)
        p = jnp.exp(s - m)
        m_ref[...] = jnp.where(blk_lane == blk, m, m_ref[...])
        l_ref[...] = jnp.where(blk_lane == blk, jnp.sum(p, axis=-1, keepdims=True), l_ref[...])
        op_ref[blk] = jnp.where(own_head, _dot_nt(p.astype(BF16), vt.astype(BF16)), 0.0)

    @pl.when(step == pl.num_programs(1) - 1)
    def _merge():
        sel = _select_bias(gate_ref[...], float(n_blocks), 1)
        kn = kn_ref[0]
        vn = vn_ref[0]
        kidx = lax.broadcasted_iota(jnp.int32, (rows, kn.shape[0]), 1)
        dist = (qidx - kidx).astype(F32)
        s_own = _dot_nt(qm_bf, kn.astype(BF16)) - slope * dist
        s_own = jnp.where(dist >= 0, s_own, MASK_VALUE)
        m_blk = m_ref[...] + sel
        m_all = jnp.maximum(jnp.max(m_blk, axis=-1, keepdims=True), jnp.max(s_own, axis=-1, keepdims=True))
        w = jnp.exp(m_blk - m_all)
        p_own = jnp.exp(s_own - m_all)
        den = jnp.sum(w * l_ref[...], axis=-1, keepdims=True) + jnp.sum(p_own, axis=-1, keepdims=True)
        acc = jnp.where(own_head, _dot(p_own.astype(BF16), vn.astype(BF16)), 0.0)
        for b in range(n_blocks):
            acc = acc + w[:, b:b + 1] * op_ref[b]
        acc = acc / den
        for qi in range(n_q):
            o_ref[0, qi:qi + 1, :] = jnp.sum(acc[qi * N_ATT_HEADS:(qi + 1) * N_ATT_HEADS], axis=0, keepdims=True)


def _moba_decode(q, k_new, v_new, cache_k, cache_v, page_table):
    b, n_q, _ = q.shape
    n_pages = page_table.shape[1]
    past_len = n_pages * PAGE_SIZE
    n_blocks = past_len // MOBA_BLOCK
    assert n_blocks <= 128 and n_pages % 2 == 0
    pages = next(p for p in (16, 8, 4, 2) if n_pages % p == 0)
    rows = N_ATT_HEADS * n_q
    q_rows = jnp.repeat(q, N_ATT_HEADS, axis=1)
    pad = ((0, 0), (0, OWN_ROWS - n_q), (0, 0))
    by_seq = lambda r: pl.BlockSpec((1, r, ATT_WIDTH), lambda bi, st, pt: (bi, 0, 0))

    def page_spec(t):
        return pl.BlockSpec((1, ATT_WIDTH, PAGE_SIZE), lambda bi, st, pt: (pt[bi, st * pages + t], 0, 0))

    grid_spec = pltpu.PrefetchScalarGridSpec(
        num_scalar_prefetch=1,
        grid=(b, n_pages // pages),
        in_specs=[by_seq(rows), by_seq(OWN_ROWS), by_seq(OWN_ROWS)] + [page_spec(t) for t in range(pages)] * 2,
        out_specs=by_seq(n_q),
        scratch_shapes=[pltpu.VMEM((rows, 128), F32),
                        pltpu.VMEM((n_blocks, rows, ATT_WIDTH), F32),
                        pltpu.VMEM((rows, 128), F32),
                        pltpu.VMEM((rows, 128), F32)])
    return pl.pallas_call(
        functools.partial(_decode_body, pages=pages, n_blocks=n_blocks, past_len=past_len, n_q=n_q),
        grid_spec=grid_spec,
        out_shape=jax.ShapeDtypeStruct((b, n_q, ATT_WIDTH), F32),
        compiler_params=_params("arbitrary", "arbitrary"),
        name="moba_decode",
    )(page_table, q_rows, jnp.pad(k_new, pad), jnp.pad(v_new, pad), *([cache_k] * pages), *([cache_v] * pages))


def _mix_body(attn_ref, ug_ref, vsn_ref, x_ref, wsg_ref, bsg_ref, ga_ref, gs_ref, wo_ref, gt_ref, sc_ref, sh_ref,
              gf_ref, wr_ref, br_ref, cin_ref, h2_all_ref, x1_ref, h2_ref, route_ref, cnt_ref, carry_ref):
    del h2_all_ref
    tm = x_ref.shape[0]

    @pl.when(pl.program_id(0) == 0)
    def _():
        carry_ref[...] = cin_ref[...]

    r = lax.broadcasted_iota(jnp.int32, (CHUNK, CHUNK), 0)
    c = lax.broadcasted_iota(jnp.int32, (CHUNK, CHUNK), 1)
    wcat = jnp.concatenate([jnp.where(c <= r, wsg_ref[g], 0.0).astype(BF16) for g in range(N_SGU_GROUPS)], axis=1)
    lane_group = lax.broadcasted_iota(jnp.int32, (CHUNK, SGU_WIDTH), 1) // HEAD_DIM
    mixed = []
    for ch in range(tm // CHUNK):
        vs = vsn_ref[ch * CHUNK:(ch + 1) * CHUNK, :]
        stack = jnp.concatenate([jnp.where(lane_group == g, vs, 0.0).astype(BF16) for g in range(N_SGU_GROUPS)],
                                axis=0)
        mixed.append(_dot(wcat, stack) + bsg_ref[...])
    sgu = ug_ref[...] * (mixed[0] if len(mixed) == 1 else jnp.concatenate(mixed, axis=0))
    merged = jnp.concatenate([_rms(attn_ref[...], ga_ref[...]), _rms(sgu, gs_ref[...])], axis=1)
    x1 = x_ref[...] + gt_ref[...] * _dot(merged.astype(BF16), wo_ref[...])
    x1_ref[...] = x1
    h2 = _rms(x1, gf_ref[...]) * (1.0 + sc_ref[...]) + sh_ref[...]
    _store_token_tiles(h2_ref, h2)

    logits = _dot3(h2, wr_ref[...]) + br_ref[...]
    lane_f = lax.broadcasted_iota(jnp.int32, (tm, N_EXPERTS), 1).astype(F32)
    vals, hits = [], []
    work = logits
    for _ in range(TOP_K):
        mx = jnp.max(work, axis=-1, keepdims=True)
        first = jnp.min(jnp.where(work == mx, lane_f, float(N_EXPERTS)), axis=-1, keepdims=True)
        hit = lane_f == first
        vals.append(mx)
        hits.append(hit)
        work = jnp.where(hit, -jnp.inf, work)
    ex = [jnp.exp(v - vals[0]) for v in vals]
    den = ex[0] + ex[1] + ex[2] + ex[3]
    chosen = jnp.zeros((tm, N_EXPERTS), F32)
    for hit in hits:
        chosen = jnp.where(hit, 1.0, chosen)
    rr = lax.broadcasted_iota(jnp.int32, (tm, tm), 0)
    cc = lax.broadcasted_iota(jnp.int32, (tm, tm), 1)
    before = _dot(jnp.where(cc < rr, 1.0, 0.0).astype(BF16), chosen.astype(BF16)) + carry_ref[...]
    rl = lax.broadcasted_iota(jnp.int32, (tm, ROUTE_LANES), 1)
    route = jnp.zeros((tm, ROUTE_LANES), F32)
    for t in range(TOP_K):
        idx = jnp.sum(jnp.where(hits[t], lane_f, 0.0), axis=-1, keepdims=True)
        rank = jnp.sum(jnp.where(hits[t], before, 0.0), axis=-1, keepdims=True)
        route = jnp.where(rl == t, idx, route)
        route = jnp.where(rl == TOP_K + t, ex[t] / den, route)
        route = jnp.where(rl == 2 * TOP_K + t, rank, route)
    route_ref[...] = route
    carry_ref[...] = carry_ref[...] + jnp.sum(chosen, axis=0, keepdims=True)
    cnt_ref[...] = carry_ref[...]


def _mix(attn, ug, vsn, x, wsg, bsg, g_att, g_sgu, w_out_bf, gt, sc, sh, g_ffn, w_router, b_router, cnt_in, tm,
         h2_all, first_tok):
    n = x.shape[0]
    assert first_tok % tm == 0
    first_blk = first_tok // tm
    mod_rows = gt.shape[0]
    mod_spec = (pl.BlockSpec((1, D_MODEL), lambda i: (0, 0)) if mod_rows == 1
                else pl.BlockSpec((tm, D_MODEL), lambda i: (i, 0)))
    row = lambda w: pl.BlockSpec((tm, w), lambda i: (i, 0))
    full = lambda *shape: pl.BlockSpec(shape, lambda i: (0,) * len(shape))
    return pl.pallas_call(
        _mix_body,
        grid=(n // tm,),
        in_specs=[row(ATT_WIDTH), row(SGU_WIDTH), row(SGU_WIDTH), row(D_MODEL),
                  full(N_SGU_GROUPS, CHUNK, CHUNK), full(CHUNK, SGU_WIDTH),
                  full(1, ATT_WIDTH), full(1, SGU_WIDTH), full(D_MODEL, D_MODEL),
                  mod_spec, mod_spec, mod_spec,
                  full(1, D_MODEL), full(D_MODEL, N_EXPERTS), full(1, N_EXPERTS), full(1, N_EXPERTS),
                  pl.BlockSpec(memory_space=pl.ANY)],
        out_specs=[row(D_MODEL), pl.BlockSpec((tm * ROW_TILE, 128), lambda i: (first_blk + i, 0)), row(ROUTE_LANES),
                   full(1, N_EXPERTS)],
        out_shape=[jax.ShapeDtypeStruct((n, D_MODEL), F32), jax.ShapeDtypeStruct(h2_all.shape, F32),
                   jax.ShapeDtypeStruct((n, ROUTE_LANES), F32), jax.ShapeDtypeStruct((1, N_EXPERTS), F32)],
        scratch_shapes=[pltpu.VMEM((1, N_EXPERTS), F32)],
        input_output_aliases={16: 1},
        compiler_params=_params("arbitrary"),
        name="mix",
    )(attn, ug, vsn, x, wsg, bsg, g_att.reshape(1, ATT_WIDTH), g_sgu.reshape(1, SGU_WIDTH), w_out_bf,
      gt, sc, sh, g_ffn.reshape(1, D_MODEL), w_router, b_router.reshape(1, N_EXPERTS), cnt_in, h2_all)


def _experts_body(tile_e, tile_rows, n_used, src_ref, src_next_ref, dst_ref, h2_ref, wgu_ref, bgu_ref,
                  wd_ref, bd_ref, y_ref, xbuf, obuf, gsem, ssem, wgu_bf, wd_bf):
    t = pl.program_id(0)
    used = n_used[0]
    slot = t % 2

    def start_gather(src, rows, into):
        def body(r, carry):
            @pl.when(r < rows)
            def _():
                pltpu.make_async_copy(_tile_of(h2_ref, src[r]), _tile_of(xbuf.at[into], r), gsem.at[into]).start()
            return carry
        lax.fori_loop(0, MOE_TILE, body, 0, unroll=DMA_UNROLL)

    def wait_gather(rows, into):
        pltpu.make_async_copy(_tiles_of(h2_ref, rows), _tiles_of(y_ref, rows), gsem.at[into]).wait()

    def start_scatter(rows, frm):
        def body(r, carry):
            @pl.when(r < rows)
            def _():
                pltpu.make_async_copy(_tile_of(obuf.at[frm], r), _tile_of(y_ref, dst_ref[r]), ssem.at[frm]).start()
            return carry
        lax.fori_loop(0, MOE_TILE, body, 0, unroll=DMA_UNROLL)

    def wait_scatter(rows, frm):
        pltpu.make_async_copy(_tiles_of(h2_ref, rows), _tiles_of(y_ref, rows), ssem.at[frm]).wait()

    @pl.when(t == 0)
    def _():
        xbuf[...] = jnp.zeros(xbuf.shape, F32)
        start_gather(src_ref, tile_rows[0], 0)

    @pl.when(t + 1 < used)
    def _():
        start_gather(src_next_ref, tile_rows[t + 1], 1 - slot)

    e = tile_e[t]
    prev = tile_e[jnp.maximum(t - 1, 0)]

    @pl.when((t == 0) | (e != prev))
    def _():
        wgu_bf[...] = wgu_ref[0].astype(BF16)
        wd_bf[...] = wd_ref[0].astype(BF16)

    @pl.when(t < used)
    def _():
        wait_gather(tile_rows[t], slot)

        @pl.when(t >= 2)
        def _():
            wait_scatter(tile_rows[jnp.maximum(t - 2, 0)], slot)

        x = _load_token_tiles(xbuf.at[slot], MOE_TILE)
        gu = _dot(x.astype(BF16), wgu_bf[...]) + bgu_ref[0]
        g = jnp.minimum(gu[:, :D_FF], SWIGLU_LIMIT)
        lin = jnp.clip(gu[:, D_FF:], -SWIGLU_LIMIT, SWIGLU_LIMIT)
        act = g * (1.0 / (1.0 + jnp.exp(-SWIGLU_ALPHA * g))) * (lin + 1.0)
        out = _dot(act.astype(BF16), wd_bf[...]) + bd_ref[0]
        _store_token_tiles(obuf.at[slot], out)
        start_scatter(tile_rows[t], slot)

        @pl.when(t == used - 1)
        def _():
            @pl.when(t >= 1)
            def _():
                wait_scatter(tile_rows[jnp.maximum(t - 1, 0)], 1 - slot)
            wait_scatter(tile_rows[t], slot)


def _experts(h2, slot_src, slot_dst, tile_e, tile_rows, n_used, w_gu, b_gu, w_down, b_down):
    n_tok = h2.shape[0] // ROW_TILE
    n_tiles = slot_src.shape[0] // MOE_TILE
    by_e = lambda *blk: pl.BlockSpec((1,) + blk, lambda t, te, tr, nu: (te[t],) + (0,) * len(blk))
    hbm = pl.BlockSpec(memory_space=pl.ANY)
    this_tile = pl.BlockSpec((MOE_TILE,), lambda t, *_: (t,), memory_space=pltpu.SMEM)
    grid_spec = pltpu.PrefetchScalarGridSpec(
        num_scalar_prefetch=3,
        grid=(n_tiles,),
        in_specs=[this_tile,
                  pl.BlockSpec((MOE_TILE,), lambda t, *_: (jnp.minimum(t + 1, n_tiles - 1),), memory_space=pltpu.SMEM),
                  this_tile,
                  hbm,
                  by_e(D_MODEL, 2 * D_FF), by_e(1, 2 * D_FF), by_e(D_FF, D_MODEL), by_e(1, D_MODEL)],
        out_specs=hbm,
        scratch_shapes=[pltpu.VMEM((2, MOE_TILE * ROW_TILE, 128), F32), pltpu.VMEM((2, MOE_TILE * ROW_TILE, 128), F32),
                        pltpu.SemaphoreType.DMA((2,)), pltpu.SemaphoreType.DMA((2,)),
                        pltpu.VMEM((D_MODEL, 2 * D_FF), BF16), pltpu.VMEM((D_FF, D_MODEL), BF16)])
    return pl.pallas_call(
        _experts_body,
        grid_spec=grid_spec,
        out_shape=jax.ShapeDtypeStruct((TOP_K * n_tok * ROW_TILE, 128), F32),
        compiler_params=_params("arbitrary"),
        name="experts",
    )(tile_e, tile_rows, n_used, slot_src, slot_src, slot_dst, h2, w_gu,
      b_gu.reshape(N_EXPERTS, 1, 2 * D_FF), w_down, b_down.reshape(N_EXPERTS, 1, D_MODEL))


def _combine_body(*refs):
    y_refs = refs[:TOP_K]
    x1_ref, route_ref, gt_ref, gfin_ref, o_ref = refs[TOP_K:]
    tc = x1_ref.shape[0]
    route = route_ref[...]
    moe = route[:, TOP_K:TOP_K + 1] * _load_token_tiles(y_refs[0], tc)
    for t in range(1, TOP_K):
        moe = moe + route[:, TOP_K + t:TOP_K + t + 1] * _load_token_tiles(y_refs[t], tc)
    o_ref[...] = _rms(x1_ref[...] + gt_ref[...] * moe, gfin_ref[...])


def _combine(ys, first_tok, x1, route, gt, g_final):
    n = x1.shape[0]
    tc = COMBINE_TOKENS
    n_tok = ys.shape[0] // (TOP_K * ROW_TILE)
    assert n_tok % tc == 0 and first_tok % tc == 0
    mod_spec = (pl.BlockSpec((1, D_MODEL), lambda i: (0, 0)) if gt.shape[0] == 1
                else pl.BlockSpec((tc, D_MODEL), lambda i: (i, 0)))

    def y_spec(k):
        return pl.BlockSpec((tc * ROW_TILE, 128), lambda i: ((k * n_tok + first_tok) // tc + i, 0))

    return pl.pallas_call(
        _combine_body,
        grid=(n // tc,),
        in_specs=[y_spec(k) for k in range(TOP_K)] + [
            pl.BlockSpec((tc, D_MODEL), lambda i: (i, 0)),
            pl.BlockSpec((tc, ROUTE_LANES), lambda i: (i, 0)),
            mod_spec,
            pl.BlockSpec((1, D_MODEL), lambda i: (0, 0))],
        out_specs=pl.BlockSpec((tc, D_MODEL), lambda i: (i, 0)),
        out_shape=jax.ShapeDtypeStruct((n, D_MODEL), F32),
        compiler_params=_params("arbitrary"),
        name="combine",
    )(*([ys] * TOP_K), x1, route, gt, g_final.reshape(1, D_MODEL))


def _pages_keys_minor(cache):
    return cache.transpose(0, 2, 3, 1).reshape(cache.shape[0], ATT_WIDTH, PAGE_SIZE)


def _group_mean_matrix():
    g = np.arange(SGU_WIDTH) // HEAD_DIM
    return jnp.asarray((g[:, None] == g[None, :]).astype(np.float32) / HEAD_DIM, BF16)


def _key_offset_bias():
    off = np.arange(MOBA_BLOCK, dtype=np.float32)[None, :, None]
    per_head = np.broadcast_to(np.asarray(_SLOPES, np.float32)[:, None, None] * off,
                               (N_ATT_HEADS, MOBA_BLOCK, MOBA_BLOCK))
    n_groups = N_ATT_HEADS // HEADS_PER_GROUP
    grouped = per_head.reshape(n_groups, HEADS_PER_GROUP, MOBA_BLOCK, MOBA_BLOCK).transpose(0, 2, 1, 3)
    return jnp.asarray(grouped.reshape(n_groups, MOBA_BLOCK, HEADS_PER_GROUP * MOBA_BLOCK))


def kernel(x_prompt, x_sample, cache_k, cache_v, page_table, c_prompt, c_sample, g_mix, w_ada, b_ada, w_in,
           g_sgu, w_sgu, b_sgu, g_out_att, g_out_sgu, w_out, g_ffn, w_router, b_router, w_gu, b_gu, w_down,
           b_down, g_final):
    depth = w_in.shape[0]
    batch, seq, _ = x_prompt.shape
    dec_batch, dec_seq, _ = x_sample.shape
    assert depth == 1, "the final norm is fused into the last stage of the single layer"
    assert batch == 1 and seq % MOBA_BLOCK == 0
    n_dec = dec_batch * dec_seq
    assert n_dec % CHUNK == 0 and CHUNK % dec_seq == 0 and dec_seq <= MOBA_BLOCK
    n_tok = seq + n_dec
    n_pool = cache_k.shape[1]

    bd = _group_mean_matrix()
    kp = _key_offset_bias()
    xp = x_prompt.reshape(seq, D_MODEL)
    xs = x_sample.reshape(n_dec, D_MODEL)
    c_all = jnp.concatenate([c_prompt, c_sample], axis=0)
    c_rows = -(-c_all.shape[0] // 8) * 8
    c_all = jnp.pad(c_all, ((0, c_rows - c_all.shape[0]), (0, 0)))
    k_rows, v_rows, ks_rows, vs_rows, sgu_rows = [], [], [], [], []

    for l in range(depth):
        mod = _ada(c_all, w_ada[l], b_ada[l])
        mod_p = [mod[0:1, t * D_MODEL:(t + 1) * D_MODEL] for t in range(6)]
        mod_rows = jnp.broadcast_to(mod[1:1 + dec_batch, None, :], (dec_batch, dec_seq, 6 * D_MODEL))
        mod_rows = mod_rows.reshape(n_dec, 6 * D_MODEL)
        mod_s = [mod_rows[:, t * D_MODEL:(t + 1) * D_MODEL] for t in range(6)]
        w_in_bf = w_in[l].astype(BF16)
        w_out_bf = w_out[l].astype(BF16)
        gs_flat = g_sgu[l].reshape(SGU_WIDTH)

        q, k, v, ug, vsn, kb, vt, kmean = _inproj(xp, g_mix[l], mod_p[1], mod_p[0], w_in_bf, gs_flat, bd, True)
        attn = _moba_prompt(q, kb, vt, kmean.reshape(seq // MOBA_BLOCK, ATT_WIDTH), kp)
        bias_p = jnp.repeat(b_sgu[l].T, HEAD_DIM, axis=1)
        h2 = jnp.zeros((n_tok * ROW_TILE, 128), F32)
        x1_p, h2, route_p, cnt_p = _mix(attn, ug, vsn, xp, w_sgu[l], bias_p, g_out_att[l], g_out_sgu[l],
                                        w_out_bf, mod_p[2], mod_p[4], mod_p[3], g_ffn[l], w_router[l],
                                        b_router[l], jnp.zeros((1, N_EXPERTS), F32), MOBA_BLOCK, h2, 0)
        k_rows.append(k)
        v_rows.append(v)

        qs, ks, vs, ugs, vsns = _inproj(xs, g_mix[l], mod_s[1], mod_s[0], w_in_bf, gs_flat, bd, False)
        attn_s = _moba_decode(qs.reshape(dec_batch, dec_seq, ATT_WIDTH), ks.reshape(dec_batch, dec_seq, ATT_WIDTH),
                              vs.reshape(dec_batch, dec_seq, ATT_WIDTH),
                              _pages_keys_minor(cache_k[l]), _pages_keys_minor(cache_v[l]), page_table)
        eye = jnp.eye(CHUNK // dec_seq, dtype=F32)
        w_s = jax.vmap(lambda w: jnp.kron(eye, w))(w_sgu[l][:, :dec_seq, :dec_seq])
        bias_s = jnp.tile(jnp.repeat(b_sgu[l].T[:dec_seq], HEAD_DIM, axis=1), (CHUNK // dec_seq, 1))
        x1_s, h2, route_s, cnt = _mix(attn_s.reshape(n_dec, ATT_WIDTH), ugs, vsns, xs, w_s, bias_s, g_out_att[l],
                                      g_out_sgu[l], w_out_bf, mod_s[2], mod_s[4], mod_s[3], g_ffn[l],
                                      w_router[l], b_router[l], cnt_p, CHUNK, h2, seq)
        ks_rows.append(ks)
        vs_rows.append(vs)
        sgu_rows.append(vsns)

        route = jnp.concatenate([route_p, route_s], axis=0)
        e_idx = route[:, :TOP_K].astype(jnp.int32)
        rank = route[:, 2 * TOP_K:3 * TOP_K].astype(jnp.int32)
        cnt_i = cnt.reshape(N_EXPERTS).astype(jnp.int32)
        pcnt = (cnt_i + MOE_TILE - 1) // MOE_TILE * MOE_TILE
        pend = jnp.cumsum(pcnt)
        pstart = pend - pcnt
        dest = (pstart[e_idx] + rank).reshape(-1)
        n_tiles = -(-(n_tok * TOP_K) // MOE_TILE) + N_EXPERTS
        tile_start = jnp.arange(n_tiles, dtype=jnp.int32) * MOE_TILE
        tile_e = jnp.minimum(jnp.sum((pend[None, :] <= tile_start[:, None]).astype(jnp.int32), axis=1), N_EXPERTS - 1)
        n_used = (pend[-1:] // MOE_TILE).astype(jnp.int32)
        tile_rows = jnp.clip((pstart + cnt_i)[tile_e] - tile_start, 0, MOE_TILE).astype(jnp.int32)
        slot_assign = jnp.zeros((n_tiles * MOE_TILE,), jnp.int32).at[dest].set(
            jnp.arange(n_tok * TOP_K, dtype=jnp.int32))
        slot_src = slot_assign // TOP_K
        slot_dst = (slot_assign % TOP_K) * n_tok + slot_src

        ys = _experts(h2, slot_src, slot_dst, tile_e, tile_rows, n_used, w_gu[l], b_gu[l], w_down[l], b_down[l])
        xp = _combine(ys, 0, x1_p, route_p, mod_p[5], g_final)
        xs = _combine(ys, seq, x1_s, route_s, mod_s[5], g_final)

    hd = (N_ATT_HEADS, HEAD_DIM)
    k_p = jnp.stack(k_rows).reshape(depth, batch, seq // PAGE_SIZE, PAGE_SIZE, *hd)
    v_p = jnp.stack(v_rows).reshape(depth, batch, seq // PAGE_SIZE, PAGE_SIZE, *hd)
    k_s = jnp.stack(ks_rows).reshape(depth, dec_batch, dec_seq, *hd)
    v_s = jnp.stack(vs_rows).reshape(depth, dec_batch, dec_seq, *hd)
    sgu_v = jnp.stack(sgu_rows).reshape(depth, dec_batch, dec_seq, N_SGU_GROUPS, HEAD_DIM)
    return (xp.reshape(batch, seq, D_MODEL), xs.reshape(dec_batch, dec_seq, D_MODEL), k_p, v_p, k_s, v_s, sgu_v)
```

```python
import functools

import numpy as np
import jax
import jax.numpy as jnp
from jax import lax
from jax.experimental import pallas as pl
from jax.experimental.pallas import tpu as pltpu

F32 = jnp.float32
BF16 = jnp.bfloat16

D_MODEL = 1024
HEAD_DIM = 64
N_ATT_HEADS = 8
N_SGU_GROUPS = 8
ATT_WIDTH = N_ATT_HEADS * HEAD_DIM
SGU_WIDTH = N_SGU_GROUPS * HEAD_DIM
IN_WIDTH = 3 * ATT_WIDTH + 2 * SGU_WIDTH
CHUNK = 128
MOBA_BLOCK = 256
MOBA_TOPK = 3
PAGE_SIZE = 128
N_EXPERTS = 32
TOP_K = 4
D_FF = D_MODEL
SWIGLU_LIMIT = 7.0
SWIGLU_ALPHA = 1.702
EPS = 1e-6
MASK_VALUE = -1e30

HEADS_PER_GROUP = 4
GROUP_WIDTH = HEADS_PER_GROUP * HEAD_DIM
ROUTE_LANES = 128
MOE_TILE = 256
KEY_BLOCKS_PER_STEP = 4
DMA_UNROLL = 8
COMBINE_TOKENS = 128
OWN_ROWS = 16
VMEM_LIMIT = 56 * 1024 * 1024

_SLOPES = [2.0 ** (-8.0 * (h + 1) / N_ATT_HEADS) for h in range(N_ATT_HEADS)]


def _params(*sem, flags=None):
    return pltpu.CompilerParams(dimension_semantics=sem, vmem_limit_bytes=VMEM_LIMIT, flags=flags)


def _split(a):
    hi = a.astype(BF16)
    lo = (a - hi.astype(F32)).astype(BF16)
    return hi, lo


def _dot(a, b):
    return jnp.dot(a, b, preferred_element_type=F32)


def _dot_nt(a, b):
    return lax.dot_general(a, b, (((1,), (1,)), ((), ())), preferred_element_type=F32)


def _dot3(a, b, dot=_dot):
    a_hi, a_lo = _split(a)
    b_hi, b_lo = _split(b)
    return dot(a_hi, b_hi) + (dot(a_hi, b_lo) + dot(a_lo, b_hi))


def _rms(x, g):
    return x * lax.rsqrt(jnp.mean(x * x, axis=-1, keepdims=True) + EPS) * g


def _gelu(x):
    return x * (0.5 * (1.0 + jnp.tanh(0.7978845608028654 * (x + 0.044715 * (x * x * x)))))


ROW_TILE = D_MODEL // 128


def _store_token_tiles(ref, x):
    for c in range(ROW_TILE):
        ref[pl.ds(c, x.shape[0], stride=ROW_TILE), :] = x[:, c * 128:(c + 1) * 128]


def _load_token_tiles(ref, rows):
    return jnp.concatenate([ref[pl.ds(c, rows, stride=ROW_TILE), :] for c in range(ROW_TILE)], axis=1)


def _tile_of(ref, token):
    return ref.at[pl.ds(pl.multiple_of(token * ROW_TILE, ROW_TILE), ROW_TILE)]


def _tiles_of(ref, n_tokens):
    return ref.at[pl.ds(0, pl.multiple_of(n_tokens * ROW_TILE, ROW_TILE))]


def _ada_body(c_ref, w_ref, b_ref, o_ref):
    c = c_ref[...]
    s = c * (1.0 / (1.0 + jnp.exp(-c)))
    o_ref[...] = _dot3(s, w_ref[...]) + b_ref[...]


def _ada(c, w, b):
    rows = c.shape[0]
    n = w.shape[1]
    tn = 1536
    return pl.pallas_call(
        _ada_body,
        grid=(n // tn,),
        in_specs=[pl.BlockSpec((rows, D_MODEL), lambda j: (0, 0)),
                  pl.BlockSpec((D_MODEL, tn), lambda j: (0, j)),
                  pl.BlockSpec((1, tn), lambda j: (0, j))],
        out_specs=pl.BlockSpec((rows, tn), lambda j: (0, j)),
        out_shape=jax.ShapeDtypeStruct((rows, n), F32),
        compiler_params=_params("arbitrary"),
        name="ada",
    )(c, w, b.reshape(1, n))


def _inproj_body(x_ref, g_ref, sc_ref, sh_ref, w_ref, gs_ref, bd_ref, q_ref, k_ref, v_ref, ug_ref, vsn_ref,
                 *prompt_refs):
    h = _rms(x_ref[...], g_ref[...]) * (1.0 + sc_ref[...]) + sh_ref[...]
    z = _dot(h.astype(BF16), w_ref[...])
    a = ATT_WIDTH
    q_ref[...] = z[:, :a]
    k = z[:, a:2 * a]
    v = z[:, 2 * a:3 * a]
    k_ref[...] = k
    v_ref[...] = v
    ug_ref[...] = _gelu(z[:, 3 * a:3 * a + SGU_WIDTH])
    vg = _gelu(z[:, 3 * a + SGU_WIDTH:])
    sq_hi, sq_lo = _split(vg * vg)
    gmean = _dot(sq_hi, bd_ref[...]) + _dot(sq_lo, bd_ref[...])
    vsn_ref[...] = vg * lax.rsqrt(gmean + EPS) * gs_ref[...]
    if prompt_refs:
        kb_ref, vt_ref, km_ref = prompt_refs
        kb_ref[...] = k.astype(BF16)
        vt_ref[...] = v.T.astype(BF16)
        km_ref[0] = jnp.mean(k, axis=0, keepdims=True)


def _inproj(x, g_mix, sc, sh, w_in_bf, g_sgu, bd, prompt):
    n = x.shape[0]
    tm = MOBA_BLOCK if prompt else n
    mod_rows = sc.shape[0]
    mod_spec = (pl.BlockSpec((1, D_MODEL), lambda i: (0, 0)) if mod_rows == 1
                else pl.BlockSpec((tm, D_MODEL), lambda i: (i, 0)))
    row = lambda w: pl.BlockSpec((tm, w), lambda i: (i, 0))
    out_specs = [row(ATT_WIDTH)] * 3 + [row(SGU_WIDTH)] * 2
    out_shape = [jax.ShapeDtypeStruct((n, ATT_WIDTH), F32)] * 3 + [jax.ShapeDtypeStruct((n, SGU_WIDTH), F32)] * 2
    if prompt:
        out_specs += [row(ATT_WIDTH), pl.BlockSpec((ATT_WIDTH, tm), lambda i: (0, i)),
                      pl.BlockSpec((1, 1, ATT_WIDTH), lambda i: (i, 0, 0))]
        out_shape += [jax.ShapeDtypeStruct((n, ATT_WIDTH), BF16), jax.ShapeDtypeStruct((ATT_WIDTH, n), BF16),
                      jax.ShapeDtypeStruct((n // tm, 1, ATT_WIDTH), F32)]
    return pl.pallas_call(
        _inproj_body,
        grid=(n // tm,),
        in_specs=[row(D_MODEL),
                  pl.BlockSpec((1, D_MODEL), lambda i: (0, 0)),
                  mod_spec, mod_spec,
                  pl.BlockSpec((D_MODEL, IN_WIDTH), lambda i: (0, 0)),
                  pl.BlockSpec((1, SGU_WIDTH), lambda i: (0, 0)),
                  pl.BlockSpec((SGU_WIDTH, SGU_WIDTH), lambda i: (0, 0))],
        out_specs=out_specs,
        out_shape=out_shape,
        compiler_params=_params("arbitrary"),
        name="inproj_prompt" if prompt else "inproj_sample",
    )(x, g_mix.reshape(1, D_MODEL), sc, sh, w_in_bf, g_sgu.reshape(1, SGU_WIDTH), bd)


def _select_bias(gate, n_past, axis):
    blk = lax.broadcasted_iota(jnp.int32, gate.shape, axis).astype(F32)
    past = blk < n_past
    gate = jnp.where(past, gate, -jnp.inf)
    bias = jnp.full(gate.shape, MASK_VALUE, F32)
    for _ in range(MOBA_TOPK):
        mx = jnp.max(gate, axis=axis, keepdims=True)
        first = jnp.min(jnp.where(gate == mx, blk, float(gate.shape[axis])), axis=axis, keepdims=True)
        hit = blk == first
        bias = jnp.where(hit, 0.0, bias)
        gate = jnp.where(hit, -jnp.inf, gate)
    return jnp.where(past, bias, MASK_VALUE)


def _moba_body(i_of, p_of, q_ref, kb_ref, vt_ref, km_ref, kp_ref, o_ref, qm_ref, sel_ref, m_ref, l_ref, acc_ref,
               s_ref):
    s = pl.program_id(0)
    i = i_of[s]
    pair = p_of[s]
    own_pair = i // KEY_BLOCKS_PER_STEP
    bq = MOBA_BLOCK
    wide = HEADS_PER_GROUP * bq
    n_groups = N_ATT_HEADS // HEADS_PER_GROUP
    head_of_row = lax.broadcasted_iota(jnp.int32, (GROUP_WIDTH, bq), 0) // HEAD_DIM

    def sweep(diag, half):
        j = KEY_BLOCKS_PER_STEP * pair + half
        keys = slice(half * bq, (half + 1) * bq)
        col = lax.broadcasted_iota(jnp.int32, (1, wide), 1)
        dist0 = ((i - j) * MOBA_BLOCK + col % bq).astype(F32)
        if diag:
            key = lax.broadcasted_iota(jnp.int32, (bq, wide), 0)
            qry = lax.broadcasted_iota(jnp.int32, (bq, wide), 1) % bq
            causal = jnp.where(key <= qry, 0.0, MASK_VALUE)
        for g in range(n_groups):
            slope = jnp.zeros((1, wide), F32)
            for hl in range(HEADS_PER_GROUP):
                slope = jnp.where(col // bq == hl, _SLOPES[g * HEADS_PER_GROUP + hl], slope)
            lanes = slice(g * GROUP_WIDTH, (g + 1) * GROUP_WIDTH)
            s0 = _dot(kb_ref[keys, lanes], qm_ref[g]) + kp_ref[g]
            brow = -slope * dist0
            if diag:
                s0 = s0 + causal
                m_old = jnp.full((1, wide), MASK_VALUE, F32)
                l_old = jnp.zeros((1, wide), F32)
            else:
                brow = brow + sel_ref[g, pl.ds(j, 1), :]
                m_old = m_ref[g:g + 1, :]
                l_old = l_ref[g:g + 1, :]
            s_ref[g] = s0
            m_new = jnp.maximum(m_old, jnp.max(s_ref[g], axis=0, keepdims=True) + brow)
            p32 = jnp.exp(s_ref[g] - (m_new - brow))
            alpha = jnp.exp(m_old - m_new)
            l_ref[g:g + 1, :] = alpha * l_old + jnp.sum(p32, axis=0, keepdims=True)
            m_ref[g:g + 1, :] = m_new
            p = p32.astype(BF16)
            vt = vt_ref[lanes, keys]
            vt_heads = jnp.concatenate([jnp.where(head_of_row == hl, vt, jnp.zeros_like(vt))
                                        for hl in range(HEADS_PER_GROUP)], axis=1)
            p_heads = jnp.concatenate([p[:, hl * bq:(hl + 1) * bq] for hl in range(HEADS_PER_GROUP)], axis=0)
            pv = _dot(vt_heads, p_heads)
            if diag:
                acc_ref[g] = pv
            else:
                alpha_rows = jnp.concatenate(
                    [jnp.broadcast_to(alpha[:, hl * bq:(hl + 1) * bq], (HEAD_DIM, bq)) for hl in range(HEADS_PER_GROUP)],
                    axis=0)
                acc_ref[g] = alpha_rows * acc_ref[g] + pv

    @pl.when(pair == own_pair)
    def _first():
        qt = (q_ref[...] * (HEAD_DIM ** -0.5)).T
        for g in range(n_groups):
            lanes = slice(g * GROUP_WIDTH, (g + 1) * GROUP_WIDTH)
            for hl in range(HEADS_PER_GROUP):
                qm = jnp.where(head_of_row == hl, qt[lanes, :], 0.0)
                qm_ref[g, :, hl * bq:(hl + 1) * bq] = qm.astype(BF16)
                gate = _dot3(km_ref[:, lanes], qm)
                sel_ref[g, :, hl * bq:(hl + 1) * bq] = _select_bias(gate, i.astype(F32), 0)

    for own_at in range(KEY_BLOCKS_PER_STEP):
        @pl.when((pair == own_pair) & (i % KEY_BLOCKS_PER_STEP == own_at))
        def _own_run(own_at=own_at):
            sweep(True, own_at)
            for half in range(own_at):
                sweep(False, half)

    @pl.when(pair != own_pair)
    def _past():
        for half in range(KEY_BLOCKS_PER_STEP):
            sweep(False, half)

    @pl.when((pair == own_pair - 1) | (own_pair == 0))
    def _last():
        outs = []
        for g in range(n_groups):
            inv = jnp.concatenate(
                [jnp.broadcast_to(l_ref[g:g + 1, hl * bq:(hl + 1) * bq], (HEAD_DIM, bq)) for hl in range(HEADS_PER_GROUP)],
                axis=0)
            outs.append(acc_ref[g] / inv)
        o_ref[...] = jnp.concatenate(outs, axis=0).T


def _moba_prompt(q, kb, vt, kmean, kp):
    t = q.shape[0]
    nb = t // MOBA_BLOCK
    per = KEY_BLOCKS_PER_STEP
    assert nb % per == 0
    i_of = np.concatenate([np.full(i // per + 1, i, np.int32) for i in range(nb)])
    p_of = np.concatenate([np.concatenate([[i // per], np.arange(i // per)]).astype(np.int32) for i in range(nb)])
    n_groups = N_ATT_HEADS // HEADS_PER_GROUP
    wide = HEADS_PER_GROUP * MOBA_BLOCK
    grid_spec = pltpu.PrefetchScalarGridSpec(
        num_scalar_prefetch=2,
        grid=(len(i_of),),
        in_specs=[pl.BlockSpec((MOBA_BLOCK, ATT_WIDTH), lambda s, io, po: (io[s], 0)),
                  pl.BlockSpec((per * MOBA_BLOCK, ATT_WIDTH), lambda s, io, po: (po[s], 0)),
                  pl.BlockSpec((ATT_WIDTH, per * MOBA_BLOCK), lambda s, io, po: (0, po[s])),
                  pl.BlockSpec((nb, ATT_WIDTH), lambda s, io, jo: (0, 0)),
                  pl.BlockSpec((n_groups, MOBA_BLOCK, wide), lambda s, io, jo: (0, 0, 0))],
        out_specs=pl.BlockSpec((MOBA_BLOCK, ATT_WIDTH), lambda s, io, jo: (io[s], 0)),
        scratch_shapes=[pltpu.VMEM((n_groups, GROUP_WIDTH, wide), BF16),
                        pltpu.VMEM((n_groups, nb, wide), F32),
                        pltpu.VMEM((n_groups, wide), F32),
                        pltpu.VMEM((n_groups, wide), F32),
                        pltpu.VMEM((n_groups, GROUP_WIDTH, MOBA_BLOCK), F32),
                        pltpu.VMEM((n_groups, MOBA_BLOCK, wide), F32)])
    return pl.pallas_call(
        _moba_body,
        grid_spec=grid_spec,
        out_shape=jax.ShapeDtypeStruct((t, ATT_WIDTH), F32),
        compiler_params=_params("arbitrary"),
        name="moba_prompt",
    )(jnp.asarray(i_of), jnp.asarray(p_of), q, kb, vt, kmean, kp)


def _decode_body(pt_ref, q_ref, kn_ref, vn_ref, *refs, pages, n_blocks, past_len, n_q):
    k_pages = refs[:pages]
    v_pages = refs[pages:2 * pages]
    o_ref = refs[2 * pages]
    gate_ref, op_ref, m_ref, l_ref = refs[2 * pages + 1:]
    step = pl.program_id(1)
    rows = N_ATT_HEADS * n_q
    head_of_row = lax.broadcasted_iota(jnp.int32, (rows, ATT_WIDTH), 0) % N_ATT_HEADS
    head_of_lane = lax.broadcasted_iota(jnp.int32, (rows, ATT_WIDTH), 1) // HEAD_DIM
    own_head = head_of_row == head_of_lane
    rid = lax.broadcasted_iota(jnp.int32, (rows, 1), 0)
    slope = jnp.zeros((rows, 1), F32)
    for h in range(N_ATT_HEADS):
        slope = jnp.where(rid % N_ATT_HEADS == h, _SLOPES[h], slope)
    qidx = rid // N_ATT_HEADS
    qpos = (past_len + qidx).astype(F32)
    blk_lane = lax.broadcasted_iota(jnp.int32, (rows, 128), 1)

    @pl.when(step == 0)
    def _init():
        m_ref[...] = jnp.full(m_ref.shape, MASK_VALUE, F32)
        l_ref[...] = jnp.zeros(l_ref.shape, F32)
        gate_ref[...] = jnp.zeros(gate_ref.shape, F32)

    qm_bf = jnp.where(own_head, q_ref[0] * (HEAD_DIM ** -0.5), 0.0).astype(BF16)
    blocks = range(pages // 2)
    blk = [step * (pages // 2) + b for b in blocks]
    qk = [_dot(qm_bf, jnp.concatenate([k_pages[2 * b][0], k_pages[2 * b + 1][0]], axis=1).astype(BF16))
          for b in blocks]
    gsum = [jnp.sum(qk[b], axis=-1, keepdims=True) * (1.0 / MOBA_BLOCK) for b in blocks]
    key = lax.broadcasted_iota(jnp.int32, (1, MOBA_BLOCK), 1)
    s = [qk[b] - slope * (qpos - (blk[b] * MOBA_BLOCK + key).astype(F32)) for b in blocks]
    m = [jnp.max(s[b], axis=-1, keepdims=True) for b in blocks]
    p = [jnp.exp(s[b] - m[b]) for b in blocks]
    l = [jnp.sum(p[b], axis=-1, keepdims=True) for b in blocks]
    gate, m_all, l_all = gate_ref[...], m_ref[...], l_ref[...]
    for b in blocks:
        here = blk_lane == blk[b]
        gate = jnp.where(here, gsum[b], gate)
        m_all = jnp.where(here, m[b], m_all)
        l_all = jnp.where(here, l[b], l_all)
    gate_ref[...], m_ref[...], l_ref[...] = gate, m_all, l_all
    for b in blocks:
        vt = jnp.concatenate([v_pages[2 * b][0], v_pages[2 * b + 1][0]], axis=1).astype(BF16)
        op_ref[blk[b]] = jnp.where(own_head, _dot_nt(p[b].astype(BF16), vt), 0.0)

    @pl.when(step == pl.num_programs(1) - 1)
    def _merge():
        sel = _select_bias(gate_ref[...], float(n_blocks), 1)
        kn = kn_ref[0]
        vn = vn_ref[0]
        kidx = lax.broadcasted_iota(jnp.int32, (rows, kn.shape[0]), 1)
        dist = (qidx - kidx).astype(F32)
        s_own = _dot_nt(qm_bf, kn.astype(BF16)) - slope * dist
        s_own = jnp.where(dist >= 0, s_own, MASK_VALUE)
        m_blk = m_ref[...] + sel
        m_all = jnp.maximum(jnp.max(m_blk, axis=-1, keepdims=True), jnp.max(s_own, axis=-1, keepdims=True))
        w = jnp.exp(m_blk - m_all)
        p_own = jnp.exp(s_own - m_all)
        den = jnp.sum(w * l_ref[...], axis=-1, keepdims=True) + jnp.sum(p_own, axis=-1, keepdims=True)
        acc = jnp.where(own_head, _dot(p_own.astype(BF16), vn.astype(BF16)), 0.0)
        for b in range(n_blocks):
            acc = acc + w[:, b:b + 1] * op_ref[b]
        acc = acc / den
        for qi in range(n_q):
            o_ref[0, qi:qi + 1, :] = jnp.sum(acc[qi * N_ATT_HEADS:(qi + 1) * N_ATT_HEADS], axis=0, keepdims=True)


def _moba_decode(q, k_new, v_new, cache_k, cache_v, page_table):
    b, n_q, _ = q.shape
    n_pages = page_table.shape[1]
    past_len = n_pages * PAGE_SIZE
    n_blocks = past_len // MOBA_BLOCK
    assert n_blocks <= 128 and n_pages % 2 == 0
    pages = next(p for p in (16, 8, 4, 2) if n_pages % p == 0)
    rows = N_ATT_HEADS * n_q
    q_rows = jnp.repeat(q, N_ATT_HEADS, axis=1)
    pad = ((0, 0), (0, OWN_ROWS - n_q), (0, 0))
    by_seq = lambda r: pl.BlockSpec((1, r, ATT_WIDTH), lambda bi, st, pt: (bi, 0, 0))

    def page_spec(t):
        return pl.BlockSpec((1, ATT_WIDTH, PAGE_SIZE), lambda bi, st, pt: (pt[bi, st * pages + t], 0, 0))

    grid_spec = pltpu.PrefetchScalarGridSpec(
        num_scalar_prefetch=1,
        grid=(b, n_pages // pages),
        in_specs=[by_seq(rows), by_seq(OWN_ROWS), by_seq(OWN_ROWS)] + [page_spec(t) for t in range(pages)] * 2,
        out_specs=by_seq(n_q),
        scratch_shapes=[pltpu.VMEM((rows, 128), F32),
                        pltpu.VMEM((n_blocks, rows, ATT_WIDTH), F32),
                        pltpu.VMEM((rows, 128), F32),
                        pltpu.VMEM((rows, 128), F32)])
    return pl.pallas_call(
        functools.partial(_decode_body, pages=pages, n_blocks=n_blocks, past_len=past_len, n_q=n_q),
        grid_spec=grid_spec,
        out_shape=jax.ShapeDtypeStruct((b, n_q, ATT_WIDTH), F32),
        compiler_params=_params("arbitrary", "arbitrary"),
        name="moba_decode",
    )(page_table, q_rows, jnp.pad(k_new, pad), jnp.pad(v_new, pad), *([cache_k] * pages), *([cache_v] * pages))


def _mix_body(attn_ref, ug_ref, vsn_ref, x_ref, wsg_ref, bsg_ref, ga_ref, gs_ref, wo_ref, gt_ref, sc_ref, sh_ref,
              gf_ref, wr_ref, br_ref, cin_ref, h2_all_ref, x1_ref, h2_ref, route_ref, cnt_ref, carry_ref):
    del h2_all_ref
    tm = x_ref.shape[0]

    @pl.when(pl.program_id(0) == 0)
    def _():
        carry_ref[...] = cin_ref[...]

    r = lax.broadcasted_iota(jnp.int32, (CHUNK, CHUNK), 0)
    c = lax.broadcasted_iota(jnp.int32, (CHUNK, CHUNK), 1)
    wcat = jnp.concatenate([jnp.where(c <= r, wsg_ref[g], 0.0).astype(BF16) for g in range(N_SGU_GROUPS)], axis=1)
    lane_group = lax.broadcasted_iota(jnp.int32, (CHUNK, SGU_WIDTH), 1) // HEAD_DIM
    mixed = []
    for ch in range(tm // CHUNK):
        vs = vsn_ref[ch * CHUNK:(ch + 1) * CHUNK, :]
        stack = jnp.concatenate([jnp.where(lane_group == g, vs, 0.0).astype(BF16) for g in range(N_SGU_GROUPS)],
                                axis=0)
        mixed.append(_dot(wcat, stack) + bsg_ref[...])
    sgu = ug_ref[...] * (mixed[0] if len(mixed) == 1 else jnp.concatenate(mixed, axis=0))
    merged = jnp.concatenate([_rms(attn_ref[...], ga_ref[...]), _rms(sgu, gs_ref[...])], axis=1)
    x1 = x_ref[...] + gt_ref[...] * _dot(merged.astype(BF16), wo_ref[...])
    x1_ref[...] = x1
    h2 = _rms(x1, gf_ref[...]) * (1.0 + sc_ref[...]) + sh_ref[...]
    _store_token_tiles(h2_ref, h2)

    logits = _dot3(h2, wr_ref[...]) + br_ref[...]
    lane_f = lax.broadcasted_iota(jnp.int32, (tm, N_EXPERTS), 1).astype(F32)
    vals, hits = [], []
    work = logits
    for _ in range(TOP_K):
        mx = jnp.max(work, axis=-1, keepdims=True)
        first = jnp.min(jnp.where(work == mx, lane_f, float(N_EXPERTS)), axis=-1, keepdims=True)
        hit = lane_f == first
        vals.append(mx)
        hits.append(hit)
        work = jnp.where(hit, -jnp.inf, work)
    ex = [jnp.exp(v - vals[0]) for v in vals]
    den = ex[0] + ex[1] + ex[2] + ex[3]
    chosen = jnp.zeros((tm, N_EXPERTS), F32)
    for hit in hits:
        chosen = jnp.where(hit, 1.0, chosen)
    rr = lax.broadcasted_iota(jnp.int32, (tm, tm), 0)
    cc = lax.broadcasted_iota(jnp.int32, (tm, tm), 1)
    before = _dot(jnp.where(cc < rr, 1.0, 0.0).astype(BF16), chosen.astype(BF16)) + carry_ref[...]
    rl = lax.broadcasted_iota(jnp.int32, (tm, ROUTE_LANES), 1)
    route = jnp.zeros((tm, ROUTE_LANES), F32)
    for t in range(TOP_K):
        idx = jnp.sum(jnp.where(hits[t], lane_f, 0.0), axis=-1, keepdims=True)
        rank = jnp.sum(jnp.where(hits[t], before, 0.0), axis=-1, keepdims=True)
        route = jnp.where(rl == t, idx, route)
        route = jnp.where(rl == TOP_K + t, ex[t] / den, route)
        route = jnp.where(rl == 2 * TOP_K + t, rank, route)
    route_ref[...] = route
    carry_ref[...] = carry_ref[...] + jnp.sum(chosen, axis=0, keepdims=True)
    cnt_ref[...] = carry_ref[...]


def _mix(attn, ug, vsn, x, wsg, bsg, g_att, g_sgu, w_out_bf, gt, sc, sh, g_ffn, w_router, b_router, cnt_in, tm,
         h2_all, first_tok):
    n = x.shape[0]
    assert first_tok % tm == 0
    first_blk = first_tok // tm
    mod_rows = gt.shape[0]
    mod_spec = (pl.BlockSpec((1, D_MODEL), lambda i: (0, 0)) if mod_rows == 1
                else pl.BlockSpec((tm, D_MODEL), lambda i: (i, 0)))
    row = lambda w: pl.BlockSpec((tm, w), lambda i: (i, 0))
    full = lambda *shape: pl.BlockSpec(shape, lambda i: (0,) * len(shape))
    return pl.pallas_call(
        _mix_body,
        grid=(n // tm,),
        in_specs=[row(ATT_WIDTH), row(SGU_WIDTH), row(SGU_WIDTH), row(D_MODEL),
                  full(N_SGU_GROUPS, CHUNK, CHUNK), full(CHUNK, SGU_WIDTH),
                  full(1, ATT_WIDTH), full(1, SGU_WIDTH), full(D_MODEL, D_MODEL),
                  mod_spec, mod_spec, mod_spec,
                  full(1, D_MODEL), full(D_MODEL, N_EXPERTS), full(1, N_EXPERTS), full(1, N_EXPERTS),
                  pl.BlockSpec(memory_space=pl.ANY)],
        out_specs=[row(D_MODEL), pl.BlockSpec((tm * ROW_TILE, 128), lambda i: (first_blk + i, 0)), row(ROUTE_LANES),
                   full(1, N_EXPERTS)],
        out_shape=[jax.ShapeDtypeStruct((n, D_MODEL), F32), jax.ShapeDtypeStruct(h2_all.shape, F32),
                   jax.ShapeDtypeStruct((n, ROUTE_LANES), F32), jax.ShapeDtypeStruct((1, N_EXPERTS), F32)],
        scratch_shapes=[pltpu.VMEM((1, N_EXPERTS), F32)],
        input_output_aliases={16: 1},
        compiler_params=_params("arbitrary"),
        name="mix",
    )(attn, ug, vsn, x, wsg, bsg, g_att.reshape(1, ATT_WIDTH), g_sgu.reshape(1, SGU_WIDTH), w_out_bf,
      gt, sc, sh, g_ffn.reshape(1, D_MODEL), w_router, b_router.reshape(1, N_EXPERTS), cnt_in, h2_all)


def _experts_body(tile_e, tile_rows, n_used, src_ref, src_next_ref, dst_ref, h2_ref, wgu_ref, bgu_ref,
                  wd_ref, bd_ref, y_ref, xbuf, obuf, gsem, ssem, wgu_bf, wd_bf):
    t = pl.program_id(0)
    used = n_used[0]
    slot = t % 2

    def start_gather(src, rows, into):
        def body(r, carry):
            @pl.when(r < rows)
            def _():
                pltpu.make_async_copy(_tile_of(h2_ref, src[r]), _tile_of(xbuf.at[into], r), gsem.at[into]).start()
            return carry
        lax.fori_loop(0, MOE_TILE, body, 0, unroll=DMA_UNROLL)

    def wait_gather(rows, into):
        pltpu.make_async_copy(_tiles_of(h2_ref, rows), _tiles_of(y_ref, rows), gsem.at[into]).wait()

    def start_scatter(rows, frm):
        def body(r, carry):
            @pl.when(r < rows)
            def _():
                pltpu.make_async_copy(_tile_of(obuf.at[frm], r), _tile_of(y_ref, dst_ref[r]), ssem.at[frm]).start()
            return carry
        lax.fori_loop(0, MOE_TILE, body, 0, unroll=DMA_UNROLL)

    def wait_scatter(rows, frm):
        pltpu.make_async_copy(_tiles_of(h2_ref, rows), _tiles_of(y_ref, rows), ssem.at[frm]).wait()

    @pl.when(t == 0)
    def _():
        xbuf[...] = jnp.zeros(xbuf.shape, F32)
        start_gather(src_ref, tile_rows[0], 0)

    @pl.when(t + 1 < used)
    def _():
        start_gather(src_next_ref, tile_rows[t + 1], 1 - slot)

    e = tile_e[t]
    prev = tile_e[jnp.maximum(t - 1, 0)]

    @pl.when((t == 0) | (e != prev))
    def _():
        wgu_bf[...] = wgu_ref[0].astype(BF16)
        wd_bf[...] = wd_ref[0].astype(BF16)

    @pl.when(t < used)
    def _():
        wait_gather(tile_rows[t], slot)

        @pl.when(t >= 2)
        def _():
            wait_scatter(tile_rows[jnp.maximum(t - 2, 0)], slot)

        x = _load_token_tiles(xbuf.at[slot], MOE_TILE)
        gu = _dot(x.astype(BF16), wgu_bf[...]) + bgu_ref[0]
        g = jnp.minimum(gu[:, :D_FF], SWIGLU_LIMIT)
        lin = jnp.clip(gu[:, D_FF:], -SWIGLU_LIMIT, SWIGLU_LIMIT)
        act = g * (1.0 / (1.0 + jnp.exp(-SWIGLU_ALPHA * g))) * (lin + 1.0)
        out = _dot(act.astype(BF16), wd_bf[...]) + bd_ref[0]
        _store_token_tiles(obuf.at[slot], out)
        start_scatter(tile_rows[t], slot)

        @pl.when(t == used - 1)
        def _():
            @pl.when(t >= 1)
            def _():
                wait_scatter(tile_rows[jnp.maximum(t - 1, 0)], 1 - slot)
            wait_scatter(tile_rows[t], slot)


def _experts(h2, slot_src, slot_dst, tile_e, tile_rows, n_used, w_gu, b_gu, w_down, b_down):
    n_tok = h2.shape[0] // ROW_TILE
    n_tiles = slot_src.shape[0] // MOE_TILE
    by_e = lambda *blk: pl.BlockSpec((1,) + blk, lambda t, te, tr, nu: (te[t],) + (0,) * len(blk))
    hbm = pl.BlockSpec(memory_space=pl.ANY)
    this_tile = pl.BlockSpec((MOE_TILE,), lambda t, *_: (t,), memory_space=pltpu.SMEM)
    grid_spec = pltpu.PrefetchScalarGridSpec(
        num_scalar_prefetch=3,
        grid=(n_tiles,),
        in_specs=[this_tile,
                  pl.BlockSpec((MOE_TILE,), lambda t, *_: (jnp.minimum(t + 1, n_tiles - 1),), memory_space=pltpu.SMEM),
                  this_tile,
                  hbm,
                  by_e(D_MODEL, 2 * D_FF), by_e(1, 2 * D_FF), by_e(D_FF, D_MODEL), by_e(1, D_MODEL)],
        out_specs=hbm,
        scratch_shapes=[pltpu.VMEM((2, MOE_TILE * ROW_TILE, 128), F32), pltpu.VMEM((2, MOE_TILE * ROW_TILE, 128), F32),
                        pltpu.SemaphoreType.DMA((2,)), pltpu.SemaphoreType.DMA((2,)),
                        pltpu.VMEM((D_MODEL, 2 * D_FF), BF16), pltpu.VMEM((D_FF, D_MODEL), BF16)])
    return pl.pallas_call(
        _experts_body,
        grid_spec=grid_spec,
        out_shape=jax.ShapeDtypeStruct((TOP_K * n_tok * ROW_TILE, 128), F32),
        compiler_params=_params("arbitrary"),
        name="experts",
    )(tile_e, tile_rows, n_used, slot_src, slot_src, slot_dst, h2, w_gu,
      b_gu.reshape(N_EXPERTS, 1, 2 * D_FF), w_down, b_down.reshape(N_EXPERTS, 1, D_MODEL))


def _combine_body(*refs):
    y_refs = refs[:TOP_K]
    x1_ref, route_ref, gt_ref, gfin_ref, o_ref = refs[TOP_K:]
    tc = x1_ref.shape[0]
    route = route_ref[...]
    moe = route[:, TOP_K:TOP_K + 1] * _load_token_tiles(y_refs[0], tc)
    for t in range(1, TOP_K):
        moe = moe + route[:, TOP_K + t:TOP_K + t + 1] * _load_token_tiles(y_refs[t], tc)
    o_ref[...] = _rms(x1_ref[...] + gt_ref[...] * moe, gfin_ref[...])


def _combine(ys, first_tok, x1, route, gt, g_final):
    n = x1.shape[0]
    tc = COMBINE_TOKENS
    n_tok = ys.shape[0] // (TOP_K * ROW_TILE)
    assert n_tok % tc == 0 and first_tok % tc == 0
    mod_spec = (pl.BlockSpec((1, D_MODEL), lambda i: (0, 0)) if gt.shape[0] == 1
                else pl.BlockSpec((tc, D_MODEL), lambda i: (i, 0)))

    def y_spec(k):
        return pl.BlockSpec((tc * ROW_TILE, 128), lambda i: ((k * n_tok + first_tok) // tc + i, 0))

    return pl.pallas_call(
        _combine_body,
        grid=(n // tc,),
        in_specs=[y_spec(k) for k in range(TOP_K)] + [
            pl.BlockSpec((tc, D_MODEL), lambda i: (i, 0)),
            pl.BlockSpec((tc, ROUTE_LANES), lambda i: (i, 0)),
            mod_spec,
            pl.BlockSpec((1, D_MODEL), lambda i: (0, 0))],
        out_specs=pl.BlockSpec((tc, D_MODEL), lambda i: (i, 0)),
        out_shape=jax.ShapeDtypeStruct((n, D_MODEL), F32),
        compiler_params=_params("arbitrary"),
        name="combine",
    )(*([ys] * TOP_K), x1, route, gt, g_final.reshape(1, D_MODEL))


def _pages_keys_minor(cache):
    return cache.transpose(0, 2, 3, 1).reshape(cache.shape[0], ATT_WIDTH, PAGE_SIZE)


def _group_mean_matrix():
    g = np.arange(SGU_WIDTH) // HEAD_DIM
    return jnp.asarray((g[:, None] == g[None, :]).astype(np.float32) / HEAD_DIM, BF16)


def _key_offset_bias():
    off = np.arange(MOBA_BLOCK, dtype=np.float32)[None, :, None]
    per_head = np.broadcast_to(np.asarray(_SLOPES, np.float32)[:, None, None] * off,
                               (N_ATT_HEADS, MOBA_BLOCK, MOBA_BLOCK))
    n_groups = N_ATT_HEADS // HEADS_PER_GROUP
    grouped = per_head.reshape(n_groups, HEADS_PER_GROUP, MOBA_BLOCK, MOBA_BLOCK).transpose(0, 2, 1, 3)
    return jnp.asarray(grouped.reshape(n_groups, MOBA_BLOCK, HEADS_PER_GROUP * MOBA_BLOCK))


def kernel(x_prompt, x_sample, cache_k, cache_v, page_table, c_prompt, c_sample, g_mix, w_ada, b_ada, w_in,
           g_sgu, w_sgu, b_sgu, g_out_att, g_out_sgu, w_out, g_ffn, w_router, b_router, w_gu, b_gu, w_down,
           b_down, g_final):
    depth = w_in.shape[0]
    batch, seq, _ = x_prompt.shape
    dec_batch, dec_seq, _ = x_sample.shape
    assert depth == 1, "the final norm is fused into the last stage of the single layer"
    assert batch == 1 and seq % MOBA_BLOCK == 0
    n_dec = dec_batch * dec_seq
    assert n_dec % CHUNK == 0 and CHUNK % dec_seq == 0 and dec_seq <= MOBA_BLOCK
    n_tok = seq + n_dec
    n_pool = cache_k.shape[1]

    bd = _group_mean_matrix()
    kp = _key_offset_bias()
    xp = x_prompt.reshape(seq, D_MODEL)
    xs = x_sample.reshape(n_dec, D_MODEL)
    c_all = jnp.concatenate([c_prompt, c_sample], axis=0)
    c_rows = -(-c_all.shape[0] // 8) * 8
    c_all = jnp.pad(c_all, ((0, c_rows - c_all.shape[0]), (0, 0)))
    k_rows, v_rows, ks_rows, vs_rows, sgu_rows = [], [], [], [], []

    for l in range(depth):
        mod = _ada(c_all, w_ada[l], b_ada[l])
        mod_p = [mod[0:1, t * D_MODEL:(t + 1) * D_MODEL] for t in range(6)]
        mod_rows = jnp.broadcast_to(mod[1:1 + dec_batch, None, :], (dec_batch, dec_seq, 6 * D_MODEL))
        mod_rows = mod_rows.reshape(n_dec, 6 * D_MODEL)
        mod_s = [mod_rows[:, t * D_MODEL:(t + 1) * D_MODEL] for t in range(6)]
        w_in_bf = w_in[l].astype(BF16)
        w_out_bf = w_out[l].astype(BF16)
        gs_flat = g_sgu[l].reshape(SGU_WIDTH)

        q, k, v, ug, vsn, kb, vt, kmean = _inproj(xp, g_mix[l], mod_p[1], mod_p[0], w_in_bf, gs_flat, bd, True)
        attn = _moba_prompt(q, kb, vt, kmean.reshape(seq // MOBA_BLOCK, ATT_WIDTH), kp)
        bias_p = jnp.repeat(b_sgu[l].T, HEAD_DIM, axis=1)
        h2 = jnp.zeros((n_tok * ROW_TILE, 128), F32)
        x1_p, h2, route_p, cnt_p = _mix(attn, ug, vsn, xp, w_sgu[l], bias_p, g_out_att[l], g_out_sgu[l],
                                        w_out_bf, mod_p[2], mod_p[4], mod_p[3], g_ffn[l], w_router[l],
                                        b_router[l], jnp.zeros((1, N_EXPERTS), F32), MOBA_BLOCK, h2, 0)
        k_rows.append(k)
        v_rows.append(v)

        qs, ks, vs, ugs, vsns = _inproj(xs, g_mix[l], mod_s[1], mod_s[0], w_in_bf, gs_flat, bd, False)
        attn_s = _moba_decode(qs.reshape(dec_batch, dec_seq, ATT_WIDTH), ks.reshape(dec_batch, dec_seq, ATT_WIDTH),
                              vs.reshape(dec_batch, dec_seq, ATT_WIDTH),
                              _pages_keys_minor(cache_k[l]), _pages_keys_minor(cache_v[l]), page_table)
        eye = jnp.eye(CHUNK // dec_seq, dtype=F32)
        w_s = jax.vmap(lambda w: jnp.kron(eye, w))(w_sgu[l][:, :dec_seq, :dec_seq])
        bias_s = jnp.tile(jnp.repeat(b_sgu[l].T[:dec_seq], HEAD_DIM, axis=1), (CHUNK // dec_seq, 1))
        x1_s, h2, route_s, cnt = _mix(attn_s.reshape(n_dec, ATT_WIDTH), ugs, vsns, xs, w_s, bias_s, g_out_att[l],
                                      g_out_sgu[l], w_out_bf, mod_s[2], mod_s[4], mod_s[3], g_ffn[l],
                                      w_router[l], b_router[l], cnt_p, CHUNK, h2, seq)
        ks_rows.append(ks)
        vs_rows.append(vs)
        sgu_rows.append(vsns)

        route = jnp.concatenate([route_p, route_s], axis=0)
        e_idx = route[:, :TOP_K].astype(jnp.int32)
        rank = route[:, 2 * TOP_K:3 * TOP_K].astype(jnp.int32)
        cnt_i = cnt.reshape(N_EXPERTS).astype(jnp.int32)
        pcnt = (cnt_i + MOE_TILE - 1) // MOE_TILE * MOE_TILE
        pend = jnp.cumsum(pcnt)
        pstart = pend - pcnt
        dest = (pstart[e_idx] + rank).reshape(-1)
        n_tiles = -(-(n_tok * TOP_K) // MOE_TILE) + N_EXPERTS
        tile_start = jnp.arange(n_tiles, dtype=jnp.int32) * MOE_TILE
        tile_e = jnp.minimum(jnp.sum((pend[None, :] <= tile_start[:, None]).astype(jnp.int32), axis=1), N_EXPERTS - 1)
        n_used = (pend[-1:] // MOE_TILE).astype(jnp.int32)
        tile_rows = jnp.clip((pstart + cnt_i)[tile_e] - tile_start, 0, MOE_TILE).astype(jnp.int32)
        slot_assign = jnp.zeros((n_tiles * MOE_TILE,), jnp.int32).at[dest].set(
            jnp.arange(n_tok * TOP_K, dtype=jnp.int32), unique_indices=True)
        slot_src = slot_assign // TOP_K
        slot_dst = (slot_assign % TOP_K) * n_tok + slot_src

        ys = _experts(h2, slot_src, slot_dst, tile_e, tile_rows, n_used, w_gu[l], b_gu[l], w_down[l], b_down[l])
        xp = _combine(ys, 0, x1_p, route_p, mod_p[5], g_final)
        xs = _combine(ys, seq, x1_s, route_s, mod_s[5], g_final)

    hd = (N_ATT_HEADS, HEAD_DIM)
    k_p = jnp.stack(k_rows).reshape(depth, batch, seq // PAGE_SIZE, PAGE_SIZE, *hd)
    v_p = jnp.stack(v_rows).reshape(depth, batch, seq // PAGE_SIZE, PAGE_SIZE, *hd)
    k_s = jnp.stack(ks_rows).reshape(depth, dec_batch, dec_seq, *hd)
    v_s = jnp.stack(vs_rows).reshape(depth, dec_batch, dec_seq, *hd)
    sgu_v = jnp.stack(sgu_rows).reshape(depth, dec_batch, dec_seq, N_SGU_GROUPS, HEAD_DIM)
    return (xp.reshape(batch, seq, D_MODEL), xs.reshape(dec_batch, dec_seq, D_MODEL), k_p, v_p, k_s, v_s, sgu_v)
```

```python
import functools

import numpy as np
import jax
import jax.numpy as jnp
from jax import lax
from jax.experimental import pallas as pl
from jax.experimental.pallas import tpu as pltpu

F32 = jnp.float32
BF16 = jnp.bfloat16

D_MODEL = 1024
HEAD_DIM = 64
N_ATT_HEADS = 8
N_SGU_GROUPS = 8
ATT_WIDTH = N_ATT_HEADS * HEAD_DIM
SGU_WIDTH = N_SGU_GROUPS * HEAD_DIM
IN_WIDTH = 3 * ATT_WIDTH + 2 * SGU_WIDTH
CHUNK = 128
MOBA_BLOCK = 256
MOBA_TOPK = 3
PAGE_SIZE = 128
N_EXPERTS = 32
TOP_K = 4
D_FF = D_MODEL
SWIGLU_LIMIT = 7.0
SWIGLU_ALPHA = 1.702
EPS = 1e-6
MASK_VALUE = -1e30

HEADS_PER_GROUP = 4
GROUP_WIDTH = HEADS_PER_GROUP * HEAD_DIM
ROUTE_LANES = 128
MOE_TILE = 256
KEY_BLOCKS_PER_STEP = 8
DMA_UNROLL = 8
COMBINE_TOKENS = 128
OWN_ROWS = 16
VMEM_LIMIT = 56 * 1024 * 1024

_SLOPES = [2.0 ** (-8.0 * (h + 1) / N_ATT_HEADS) for h in range(N_ATT_HEADS)]


def _params(*sem, flags=None):
    return pltpu.CompilerParams(dimension_semantics=sem, vmem_limit_bytes=VMEM_LIMIT, flags=flags)


def _split(a):
    hi = a.astype(BF16)
    lo = (a - hi.astype(F32)).astype(BF16)
    return hi, lo


def _dot(a, b):
    return jnp.dot(a, b, preferred_element_type=F32)


def _dot_nt(a, b):
    return lax.dot_general(a, b, (((1,), (1,)), ((), ())), preferred_element_type=F32)


def _dot3(a, b, dot=_dot):
    a_hi, a_lo = _split(a)
    b_hi, b_lo = _split(b)
    return dot(a_hi, b_hi) + (dot(a_hi, b_lo) + dot(a_lo, b_hi))


def _rms(x, g):
    return x * lax.rsqrt(jnp.mean(x * x, axis=-1, keepdims=True) + EPS) * g


def _gelu(x):
    return x * (0.5 * (1.0 + jnp.tanh(0.7978845608028654 * (x + 0.044715 * (x * x * x)))))


ROW_TILE = D_MODEL // 128


def _store_token_tiles(ref, x):
    for c in range(ROW_TILE):
        ref[pl.ds(c, x.shape[0], stride=ROW_TILE), :] = x[:, c * 128:(c + 1) * 128]


def _load_token_tiles(ref, rows):
    return jnp.concatenate([ref[pl.ds(c, rows, stride=ROW_TILE), :] for c in range(ROW_TILE)], axis=1)


def _tile_of(ref, token):
    return _tile_at(ref, token * ROW_TILE)


def _tile_at(ref, first_row):
    return ref.at[pl.ds(pl.multiple_of(first_row, ROW_TILE), ROW_TILE)]


def _tiles_of(ref, n_tokens):
    return ref.at[pl.ds(0, pl.multiple_of(n_tokens * ROW_TILE, ROW_TILE))]


def _ada_body(c_ref, w_ref, b_ref, o_ref):
    c = c_ref[...]
    s = c * (1.0 / (1.0 + jnp.exp(-c)))
    o_ref[...] = _dot3(s, w_ref[...]) + b_ref[...]


def _ada(c, w, b):
    rows = c.shape[0]
    n = w.shape[1]
    tn = 1536
    return pl.pallas_call(
        _ada_body,
        grid=(n // tn,),
        in_specs=[pl.BlockSpec((rows, D_MODEL), lambda j: (0, 0)),
                  pl.BlockSpec((D_MODEL, tn), lambda j: (0, j)),
                  pl.BlockSpec((1, tn), lambda j: (0, j))],
        out_specs=pl.BlockSpec((rows, tn), lambda j: (0, j)),
        out_shape=jax.ShapeDtypeStruct((rows, n), F32),
        compiler_params=_params("arbitrary"),
        name="ada",
    )(c, w, b.reshape(1, n))


def _inproj_body(x_ref, g_ref, sc_ref, sh_ref, w_ref, gs_ref, bd_ref, q_ref, k_ref, v_ref, ug_ref, vsn_ref,
                 *prompt_refs):
    h = _rms(x_ref[...], g_ref[...]) * (1.0 + sc_ref[...]) + sh_ref[...]
    z = _dot(h.astype(BF16), w_ref[...])
    a = ATT_WIDTH
    q_ref[...] = z[:, :a]
    k = z[:, a:2 * a]
    v = z[:, 2 * a:3 * a]
    if prompt_refs:
        vt_pages = []
        for pg in range(k_ref.shape[0]):
            rows = slice(pg * PAGE_SIZE, (pg + 1) * PAGE_SIZE)
            k_ref[pg] = k[rows].T
            vt_pages.append(v[rows].T)
            v_ref[pg] = vt_pages[-1]
    else:
        k_ref[...] = k
        v_ref[...] = v
    ug_ref[...] = _gelu(z[:, 3 * a:3 * a + SGU_WIDTH])
    vg = _gelu(z[:, 3 * a + SGU_WIDTH:])
    sq_hi, sq_lo = _split(vg * vg)
    gmean = _dot(sq_hi, bd_ref[...]) + _dot(sq_lo, bd_ref[...])
    vsn_ref[...] = vg * lax.rsqrt(gmean + EPS) * gs_ref[...]
    if prompt_refs:
        kb_ref, vt_ref, km_ref = prompt_refs
        kb_ref[...] = k.astype(BF16)
        vt_ref[...] = jnp.concatenate(vt_pages, axis=1).astype(BF16)
        km_ref[0] = jnp.mean(k, axis=0, keepdims=True)


def _inproj(x, g_mix, sc, sh, w_in_bf, g_sgu, bd, prompt):
    n = x.shape[0]
    tm = MOBA_BLOCK if prompt else n
    mod_rows = sc.shape[0]
    mod_spec = (pl.BlockSpec((1, D_MODEL), lambda i: (0, 0)) if mod_rows == 1
                else pl.BlockSpec((tm, D_MODEL), lambda i: (i, 0)))
    row = lambda w: pl.BlockSpec((tm, w), lambda i: (i, 0))
    if prompt:
        pages = tm // PAGE_SIZE
        kv_spec = pl.BlockSpec((pages, ATT_WIDTH, PAGE_SIZE), lambda i: (i, 0, 0))
        kv_shape = jax.ShapeDtypeStruct((n // PAGE_SIZE, ATT_WIDTH, PAGE_SIZE), F32)
    else:
        kv_spec, kv_shape = row(ATT_WIDTH), jax.ShapeDtypeStruct((n, ATT_WIDTH), F32)
    out_specs = [row(ATT_WIDTH), kv_spec, kv_spec] + [row(SGU_WIDTH)] * 2
    out_shape = [jax.ShapeDtypeStruct((n, ATT_WIDTH), F32), kv_shape, kv_shape] + \
        [jax.ShapeDtypeStruct((n, SGU_WIDTH), F32)] * 2
    if prompt:
        out_specs += [row(ATT_WIDTH), pl.BlockSpec((ATT_WIDTH, tm), lambda i: (0, i)),
                      pl.BlockSpec((1, 1, ATT_WIDTH), lambda i: (i, 0, 0))]
        out_shape += [jax.ShapeDtypeStruct((n, ATT_WIDTH), BF16), jax.ShapeDtypeStruct((ATT_WIDTH, n), BF16),
                      jax.ShapeDtypeStruct((n // tm, 1, ATT_WIDTH), F32)]
    return pl.pallas_call(
        _inproj_body,
        grid=(n // tm,),
        in_specs=[row(D_MODEL),
                  pl.BlockSpec((1, D_MODEL), lambda i: (0, 0)),
                  mod_spec, mod_spec,
                  pl.BlockSpec((D_MODEL, IN_WIDTH), lambda i: (0, 0)),
                  pl.BlockSpec((1, SGU_WIDTH), lambda i: (0, 0)),
                  pl.BlockSpec((SGU_WIDTH, SGU_WIDTH), lambda i: (0, 0))],
        out_specs=out_specs,
        out_shape=out_shape,
        compiler_params=_params("arbitrary"),
        name="inproj_prompt" if prompt else "inproj_sample",
    )(x, g_mix.reshape(1, D_MODEL), sc, sh, w_in_bf, g_sgu.reshape(1, SGU_WIDTH), bd)


def _select_bias(gate, n_past, axis):
    blk = lax.broadcasted_iota(jnp.int32, gate.shape, axis).astype(F32)
    past = blk < n_past
    gate = jnp.where(past, gate, -jnp.inf)
    bias = jnp.full(gate.shape, MASK_VALUE, F32)
    for _ in range(MOBA_TOPK):
        mx = jnp.max(gate, axis=axis, keepdims=True)
        first = jnp.min(jnp.where(gate == mx, blk, float(gate.shape[axis])), axis=axis, keepdims=True)
        hit = blk == first
        bias = jnp.where(hit, 0.0, bias)
        gate = jnp.where(hit, -jnp.inf, gate)
    return jnp.where(past, bias, MASK_VALUE)


def _moba_body(i_of, p_of, q_ref, kb_ref, vt_ref, km_ref, kp_ref, o_ref, qm_ref, sel_ref, m_ref, l_ref, acc_ref,
               s_ref):
    s = pl.program_id(0)
    i = i_of[s]
    pair = p_of[s]
    own_pair = i // KEY_BLOCKS_PER_STEP
    bq = MOBA_BLOCK
    wide = HEADS_PER_GROUP * bq
    n_groups = N_ATT_HEADS // HEADS_PER_GROUP
    head_of_row = lax.broadcasted_iota(jnp.int32, (GROUP_WIDTH, bq), 0) // HEAD_DIM

    def sweep(diag, half):
        j = KEY_BLOCKS_PER_STEP * pair + half
        keys = slice(half * bq, (half + 1) * bq)
        col = lax.broadcasted_iota(jnp.int32, (1, wide), 1)
        dist0 = ((i - j) * MOBA_BLOCK + col % bq).astype(F32)
        if diag:
            key = lax.broadcasted_iota(jnp.int32, (bq, wide), 0)
            qry = lax.broadcasted_iota(jnp.int32, (bq, wide), 1) % bq
            causal = jnp.where(key <= qry, 0.0, MASK_VALUE)
        for g in range(n_groups):
            slope = jnp.zeros((1, wide), F32)
            for hl in range(HEADS_PER_GROUP):
                slope = jnp.where(col // bq == hl, _SLOPES[g * HEADS_PER_GROUP + hl], slope)
            lanes = slice(g * GROUP_WIDTH, (g + 1) * GROUP_WIDTH)
            s0 = _dot(kb_ref[keys, lanes], qm_ref[g]) + kp_ref[g]
            brow = -slope * dist0
            if diag:
                s0 = s0 + causal
                m_old = jnp.full((1, wide), MASK_VALUE, F32)
                l_old = jnp.zeros((1, wide), F32)
            else:
                brow = brow + sel_ref[g, pl.ds(j, 1), :]
                m_old = m_ref[g:g + 1, :]
                l_old = l_ref[g:g + 1, :]
            s_ref[g] = s0
            m_new = jnp.maximum(m_old, jnp.max(s_ref[g], axis=0, keepdims=True) + brow)
            p32 = jnp.exp(s_ref[g] - (m_new - brow))
            alpha = jnp.exp(m_old - m_new)
            l_ref[g:g + 1, :] = alpha * l_old + jnp.sum(p32, axis=0, keepdims=True)
            m_ref[g:g + 1, :] = m_new
            p = p32.astype(BF16)
            vt = vt_ref[lanes, keys]
            vt_heads = jnp.concatenate([jnp.where(head_of_row == hl, vt, jnp.zeros_like(vt))
                                        for hl in range(HEADS_PER_GROUP)], axis=1)
            p_heads = jnp.concatenate([p[:, hl * bq:(hl + 1) * bq] for hl in range(HEADS_PER_GROUP)], axis=0)
            pv = _dot(vt_heads, p_heads)
            if diag:
                acc_ref[g] = pv
            else:
                alpha_rows = jnp.concatenate(
                    [jnp.broadcast_to(alpha[:, hl * bq:(hl + 1) * bq], (HEAD_DIM, bq)) for hl in range(HEADS_PER_GROUP)],
                    axis=0)
                acc_ref[g] = alpha_rows * acc_ref[g] + pv

    @pl.when(pair == own_pair)
    def _first():
        qt = (q_ref[...] * (HEAD_DIM ** -0.5)).T
        for g in range(n_groups):
            lanes = slice(g * GROUP_WIDTH, (g + 1) * GROUP_WIDTH)
            for hl in range(HEADS_PER_GROUP):
                qm = jnp.where(head_of_row == hl, qt[lanes, :], 0.0)
                qm_ref[g, :, hl * bq:(hl + 1) * bq] = qm.astype(BF16)
                gate = _dot3(km_ref[:, lanes], qm)
                sel_ref[g, :, hl * bq:(hl + 1) * bq] = _select_bias(gate, i.astype(F32), 0)

    for own_at in range(KEY_BLOCKS_PER_STEP):
        @pl.when((pair == own_pair) & (i % KEY_BLOCKS_PER_STEP == own_at))
        def _own_run(own_at=own_at):
            sweep(True, own_at)
            for half in range(own_at):
                sweep(False, half)

    @pl.when(pair != own_pair)
    def _past():
        for half in range(KEY_BLOCKS_PER_STEP):
            sweep(False, half)

    @pl.when((pair == own_pair - 1) | (own_pair == 0))
    def _last():
        outs = []
        for g in range(n_groups):
            inv = jnp.concatenate(
                [jnp.broadcast_to(l_ref[g:g + 1, hl * bq:(hl + 1) * bq], (HEAD_DIM, bq)) for hl in range(HEADS_PER_GROUP)],
                axis=0)
            outs.append(acc_ref[g] / inv)
        o_ref[...] = jnp.concatenate(outs, axis=0).T


def _moba_prompt(q, kb, vt, kmean, kp):
    t = q.shape[0]
    nb = t // MOBA_BLOCK
    per = KEY_BLOCKS_PER_STEP
    assert nb % per == 0
    i_of = np.concatenate([np.full(i // per + 1, i, np.int32) for i in range(nb)])
    p_of = np.concatenate([np.concatenate([[i // per], np.arange(i // per)]).astype(np.int32) for i in range(nb)])
    n_groups = N_ATT_HEADS // HEADS_PER_GROUP
    wide = HEADS_PER_GROUP * MOBA_BLOCK
    grid_spec = pltpu.PrefetchScalarGridSpec(
        num_scalar_prefetch=2,
        grid=(len(i_of),),
        in_specs=[pl.BlockSpec((MOBA_BLOCK, ATT_WIDTH), lambda s, io, po: (io[s], 0)),
                  pl.BlockSpec((per * MOBA_BLOCK, ATT_WIDTH), lambda s, io, po: (po[s], 0)),
                  pl.BlockSpec((ATT_WIDTH, per * MOBA_BLOCK), lambda s, io, po: (0, po[s])),
                  pl.BlockSpec((nb, ATT_WIDTH), lambda s, io, jo: (0, 0)),
                  pl.BlockSpec((n_groups, MOBA_BLOCK, wide), lambda s, io, jo: (0, 0, 0))],
        out_specs=pl.BlockSpec((MOBA_BLOCK, ATT_WIDTH), lambda s, io, jo: (io[s], 0)),
        scratch_shapes=[pltpu.VMEM((n_groups, GROUP_WIDTH, wide), BF16),
                        pltpu.VMEM((n_groups, nb, wide), F32),
                        pltpu.VMEM((n_groups, wide), F32),
                        pltpu.VMEM((n_groups, wide), F32),
                        pltpu.VMEM((n_groups, GROUP_WIDTH, MOBA_BLOCK), F32),
                        pltpu.VMEM((n_groups, MOBA_BLOCK, wide), F32)])
    return pl.pallas_call(
        _moba_body,
        grid_spec=grid_spec,
        out_shape=jax.ShapeDtypeStruct((t, ATT_WIDTH), F32),
        compiler_params=_params("arbitrary"),
        name="moba_prompt",
    )(jnp.asarray(i_of), jnp.asarray(p_of), q, kb, vt, kmean, kp)


def _decode_body(pt_ref, q_ref, kn_ref, vn_ref, *refs, pages, n_blocks, past_len, n_q):
    k_pages = refs[:pages]
    v_pages = refs[pages:2 * pages]
    o_ref = refs[2 * pages]
    gate_ref, op_ref, m_ref, l_ref = refs[2 * pages + 1:]
    step = pl.program_id(1)
    rows = N_ATT_HEADS * n_q
    head_of_row = lax.broadcasted_iota(jnp.int32, (rows, ATT_WIDTH), 0) % N_ATT_HEADS
    head_of_lane = lax.broadcasted_iota(jnp.int32, (rows, ATT_WIDTH), 1) // HEAD_DIM
    own_head = head_of_row == head_of_lane
    rid = lax.broadcasted_iota(jnp.int32, (rows, 1), 0)
    slope = jnp.zeros((rows, 1), F32)
    for h in range(N_ATT_HEADS):
        slope = jnp.where(rid % N_ATT_HEADS == h, _SLOPES[h], slope)
    qidx = rid // N_ATT_HEADS
    qpos = (past_len + qidx).astype(F32)
    blk_lane = lax.broadcasted_iota(jnp.int32, (rows, 128), 1)

    @pl.when(step == 0)
    def _init():
        m_ref[...] = jnp.full(m_ref.shape, MASK_VALUE, F32)
        l_ref[...] = jnp.zeros(l_ref.shape, F32)
        gate_ref[...] = jnp.zeros(gate_ref.shape, F32)

    qm_bf = jnp.where(own_head, q_ref[0] * (HEAD_DIM ** -0.5), 0.0).astype(BF16)
    blocks = range(pages // 2)
    blk = [step * (pages // 2) + b for b in blocks]
    qk = [_dot(qm_bf, jnp.concatenate([k_pages[2 * b][0], k_pages[2 * b + 1][0]], axis=1).astype(BF16))
          for b in blocks]
    gsum = [jnp.sum(qk[b], axis=-1, keepdims=True) * (1.0 / MOBA_BLOCK) for b in blocks]
    key = lax.broadcasted_iota(jnp.int32, (1, MOBA_BLOCK), 1)
    s = [qk[b] - slope * (qpos - (blk[b] * MOBA_BLOCK + key).astype(F32)) for b in blocks]
    m = [jnp.max(s[b], axis=-1, keepdims=True) for b in blocks]
    p = [jnp.exp(s[b] - m[b]) for b in blocks]
    l = [jnp.sum(p[b], axis=-1, keepdims=True) for b in blocks]
    gate, m_all, l_all = gate_ref[...], m_ref[...], l_ref[...]
    for b in blocks:
        here = blk_lane == blk[b]
        gate = jnp.where(here, gsum[b], gate)
        m_all = jnp.where(here, m[b], m_all)
        l_all = jnp.where(here, l[b], l_all)
    gate_ref[...], m_ref[...], l_ref[...] = gate, m_all, l_all
    for b in blocks:
        vt = jnp.concatenate([v_pages[2 * b][0], v_pages[2 * b + 1][0]], axis=1).astype(BF16)
        op_ref[blk[b]] = jnp.where(own_head, _dot_nt(p[b].astype(BF16), vt), 0.0)

    @pl.when(step == pl.num_programs(1) - 1)
    def _merge():
        sel = _select_bias(gate_ref[...], float(n_blocks), 1)
        kn = kn_ref[0]
        vn = vn_ref[0]
        kidx = lax.broadcasted_iota(jnp.int32, (rows, kn.shape[0]), 1)
        dist = (qidx - kidx).astype(F32)
        s_own = _dot_nt(qm_bf, kn.astype(BF16)) - slope * dist
        s_own = jnp.where(dist >= 0, s_own, MASK_VALUE)
        m_blk = m_ref[...] + sel
        m_all = jnp.maximum(jnp.max(m_blk, axis=-1, keepdims=True), jnp.max(s_own, axis=-1, keepdims=True))
        w = jnp.exp(m_blk - m_all)
        p_own = jnp.exp(s_own - m_all)
        den = jnp.sum(w * l_ref[...], axis=-1, keepdims=True) + jnp.sum(p_own, axis=-1, keepdims=True)
        acc = jnp.where(own_head, _dot(p_own.astype(BF16), vn.astype(BF16)), 0.0)
        for b in range(n_blocks):
            acc = acc + w[:, b:b + 1] * op_ref[b]
        acc = acc / den
        for qi in range(n_q):
            o_ref[0, qi:qi + 1, :] = jnp.sum(acc[qi * N_ATT_HEADS:(qi + 1) * N_ATT_HEADS], axis=0, keepdims=True)


def _moba_decode(q, k_new, v_new, cache_k, cache_v, page_table):
    b, n_q, _ = q.shape
    n_pages = page_table.shape[1]
    past_len = n_pages * PAGE_SIZE
    n_blocks = past_len // MOBA_BLOCK
    assert n_blocks <= 128 and n_pages % 2 == 0
    pages = next(p for p in (16, 8, 4, 2) if n_pages % p == 0)
    rows = N_ATT_HEADS * n_q
    q_rows = jnp.repeat(q, N_ATT_HEADS, axis=1)
    pad = ((0, 0), (0, OWN_ROWS - n_q), (0, 0))
    by_seq = lambda r: pl.BlockSpec((1, r, ATT_WIDTH), lambda bi, st, pt: (bi, 0, 0))

    def page_spec(t):
        return pl.BlockSpec((1, ATT_WIDTH, PAGE_SIZE), lambda bi, st, pt: (pt[bi, st * pages + t], 0, 0))

    grid_spec = pltpu.PrefetchScalarGridSpec(
        num_scalar_prefetch=1,
        grid=(b, n_pages // pages),
        in_specs=[by_seq(rows), by_seq(OWN_ROWS), by_seq(OWN_ROWS)] + [page_spec(t) for t in range(pages)] * 2,
        out_specs=by_seq(n_q),
        scratch_shapes=[pltpu.VMEM((rows, 128), F32),
                        pltpu.VMEM((n_blocks, rows, ATT_WIDTH), F32),
                        pltpu.VMEM((rows, 128), F32),
                        pltpu.VMEM((rows, 128), F32)])
    return pl.pallas_call(
        functools.partial(_decode_body, pages=pages, n_blocks=n_blocks, past_len=past_len, n_q=n_q),
        grid_spec=grid_spec,
        out_shape=jax.ShapeDtypeStruct((b, n_q, ATT_WIDTH), F32),
        compiler_params=_params("arbitrary", "arbitrary"),
        name="moba_decode",
    )(page_table, q_rows, jnp.pad(k_new, pad), jnp.pad(v_new, pad), *([cache_k] * pages), *([cache_v] * pages))


def _mix_body(attn_ref, ug_ref, vsn_ref, x_ref, wsg_ref, bsg_ref, ga_ref, gs_ref, wo_ref, gt_ref, sc_ref, sh_ref,
              gf_ref, wr_ref, br_ref, cin_ref, h2_all_ref, x1_ref, h2_ref, route_ref, cnt_ref, carry_ref):
    del h2_all_ref
    tm = x_ref.shape[0]

    @pl.when(pl.program_id(0) == 0)
    def _():
        carry_ref[...] = cin_ref[...]

    r = lax.broadcasted_iota(jnp.int32, (CHUNK, CHUNK), 0)
    c = lax.broadcasted_iota(jnp.int32, (CHUNK, CHUNK), 1)
    wcat = jnp.concatenate([jnp.where(c <= r, wsg_ref[g], 0.0).astype(BF16) for g in range(N_SGU_GROUPS)], axis=1)
    lane_group = lax.broadcasted_iota(jnp.int32, (CHUNK, SGU_WIDTH), 1) // HEAD_DIM
    mixed = []
    for ch in range(tm // CHUNK):
        vs = vsn_ref[ch * CHUNK:(ch + 1) * CHUNK, :]
        stack = jnp.concatenate([jnp.where(lane_group == g, vs, 0.0).astype(BF16) for g in range(N_SGU_GROUPS)],
                                axis=0)
        mixed.append(_dot(wcat, stack) + bsg_ref[...])
    sgu = ug_ref[...] * (mixed[0] if len(mixed) == 1 else jnp.concatenate(mixed, axis=0))
    merged = jnp.concatenate([_rms(attn_ref[...], ga_ref[...]), _rms(sgu, gs_ref[...])], axis=1)
    x1 = x_ref[...] + gt_ref[...] * _dot(merged.astype(BF16), wo_ref[...])
    x1_ref[...] = x1
    h2 = _rms(x1, gf_ref[...]) * (1.0 + sc_ref[...]) + sh_ref[...]
    _store_token_tiles(h2_ref, h2)

    logits = _dot3(h2, wr_ref[...]) + br_ref[...]
    lane_f = lax.broadcasted_iota(jnp.int32, (tm, N_EXPERTS), 1).astype(F32)
    vals, hits = [], []
    work = logits
    for _ in range(TOP_K):
        mx = jnp.max(work, axis=-1, keepdims=True)
        first = jnp.min(jnp.where(work == mx, lane_f, float(N_EXPERTS)), axis=-1, keepdims=True)
        hit = lane_f == first
        vals.append(mx)
        hits.append(hit)
        work = jnp.where(hit, -jnp.inf, work)
    ex = [jnp.exp(v - vals[0]) for v in vals]
    den = ex[0] + ex[1] + ex[2] + ex[3]
    chosen = jnp.zeros((tm, N_EXPERTS), F32)
    for hit in hits:
        chosen = jnp.where(hit, 1.0, chosen)
    rr = lax.broadcasted_iota(jnp.int32, (tm, tm), 0)
    cc = lax.broadcasted_iota(jnp.int32, (tm, tm), 1)
    before = _dot(jnp.where(cc < rr, 1.0, 0.0).astype(BF16), chosen.astype(BF16)) + carry_ref[...]
    rl = lax.broadcasted_iota(jnp.int32, (tm, ROUTE_LANES), 1)
    route = jnp.zeros((tm, ROUTE_LANES), F32)
    for t in range(TOP_K):
        idx = jnp.sum(jnp.where(hits[t], lane_f, 0.0), axis=-1, keepdims=True)
        rank = jnp.sum(jnp.where(hits[t], before, 0.0), axis=-1, keepdims=True)
        route = jnp.where(rl == t, idx, route)
        route = jnp.where(rl == TOP_K + t, ex[t] / den, route)
        route = jnp.where(rl == 2 * TOP_K + t, rank, route)
    route_ref[...] = route
    carry_ref[...] = carry_ref[...] + jnp.sum(chosen, axis=0, keepdims=True)
    cnt_ref[...] = carry_ref[...]


def _mix(attn, ug, vsn, x, wsg, bsg, g_att, g_sgu, w_out_bf, gt, sc, sh, g_ffn, w_router, b_router, cnt_in, tm,
         h2_all, first_tok):
    n = x.shape[0]
    assert first_tok % tm == 0
    first_blk = first_tok // tm
    mod_rows = gt.shape[0]
    mod_spec = (pl.BlockSpec((1, D_MODEL), lambda i: (0, 0)) if mod_rows == 1
                else pl.BlockSpec((tm, D_MODEL), lambda i: (i, 0)))
    row = lambda w: pl.BlockSpec((tm, w), lambda i: (i, 0))
    full = lambda *shape: pl.BlockSpec(shape, lambda i: (0,) * len(shape))
    return pl.pallas_call(
        _mix_body,
        grid=(n // tm,),
        in_specs=[row(ATT_WIDTH), row(SGU_WIDTH), row(SGU_WIDTH), row(D_MODEL),
                  full(N_SGU_GROUPS, CHUNK, CHUNK), full(CHUNK, SGU_WIDTH),
                  full(1, ATT_WIDTH), full(1, SGU_WIDTH), full(D_MODEL, D_MODEL),
                  mod_spec, mod_spec, mod_spec,
                  full(1, D_MODEL), full(D_MODEL, N_EXPERTS), full(1, N_EXPERTS), full(1, N_EXPERTS),
                  pl.BlockSpec(memory_space=pl.ANY)],
        out_specs=[row(D_MODEL), pl.BlockSpec((tm * ROW_TILE, 128), lambda i: (first_blk + i, 0)), row(ROUTE_LANES),
                   full(1, N_EXPERTS)],
        out_shape=[jax.ShapeDtypeStruct((n, D_MODEL), F32), jax.ShapeDtypeStruct(h2_all.shape, F32),
                   jax.ShapeDtypeStruct((n, ROUTE_LANES), F32), jax.ShapeDtypeStruct((1, N_EXPERTS), F32)],
        scratch_shapes=[pltpu.VMEM((1, N_EXPERTS), F32)],
        input_output_aliases={16: 1},
        compiler_params=_params("arbitrary"),
        name="mix",
    )(attn, ug, vsn, x, wsg, bsg, g_att.reshape(1, ATT_WIDTH), g_sgu.reshape(1, SGU_WIDTH), w_out_bf,
      gt, sc, sh, g_ffn.reshape(1, D_MODEL), w_router, b_router.reshape(1, N_EXPERTS), cnt_in, h2_all)


def _experts_body(tile_e, tile_rows, n_used, src_ref, src_next_ref, dst_ref, h2_ref, wgu_ref, bgu_ref,
                  wd_ref, bd_ref, y_ref, xbuf, obuf, gsem, ssem, wgu_bf, wd_bf):
    t = pl.program_id(0)
    used = n_used[0]
    slot = t % 2

    def start_gather(src, rows, into):
        def body(r, carry):
            @pl.when(r < rows)
            def _():
                pltpu.make_async_copy(_tile_at(h2_ref, src[r]), _tile_of(xbuf.at[into], r), gsem.at[into]).start()
            return carry
        lax.fori_loop(0, MOE_TILE, body, 0, unroll=DMA_UNROLL)

    def wait_gather(rows, into):
        pltpu.make_async_copy(_tiles_of(h2_ref, rows), _tiles_of(y_ref, rows), gsem.at[into]).wait()

    def start_scatter(rows, frm):
        def body(r, carry):
            @pl.when(r < rows)
            def _():
                pltpu.make_async_copy(_tile_of(obuf.at[frm], r), _tile_at(y_ref, dst_ref[r]), ssem.at[frm]).start()
            return carry
        lax.fori_loop(0, MOE_TILE, body, 0, unroll=DMA_UNROLL)

    def wait_scatter(rows, frm):
        pltpu.make_async_copy(_tiles_of(h2_ref, rows), _tiles_of(y_ref, rows), ssem.at[frm]).wait()

    @pl.when(t == 0)
    def _():
        xbuf[...] = jnp.zeros(xbuf.shape, F32)
        start_gather(src_ref, tile_rows[0], 0)

    @pl.when(t + 1 < used)
    def _():
        start_gather(src_next_ref, tile_rows[t + 1], 1 - slot)

    e = tile_e[t]
    prev = tile_e[jnp.maximum(t - 1, 0)]

    @pl.when((t == 0) | (e != prev))
    def _():
        wgu_bf[...] = wgu_ref[0].astype(BF16)
        wd_bf[...] = wd_ref[0].astype(BF16)

    @pl.when(t < used)
    def _():
        wait_gather(tile_rows[t], slot)

        @pl.when(t >= 2)
        def _():
            wait_scatter(tile_rows[jnp.maximum(t - 2, 0)], slot)

        x = _load_token_tiles(xbuf.at[slot], MOE_TILE)
        gu = _dot(x.astype(BF16), wgu_bf[...]) + bgu_ref[0]
        g = jnp.minimum(gu[:, :D_FF], SWIGLU_LIMIT)
        lin = jnp.clip(gu[:, D_FF:], -SWIGLU_LIMIT, SWIGLU_LIMIT)
        act = g * (1.0 / (1.0 + jnp.exp(-SWIGLU_ALPHA * g))) * (lin + 1.0)
        out = _dot(act.astype(BF16), wd_bf[...]) + bd_ref[0]
        _store_token_tiles(obuf.at[slot], out)
        start_scatter(tile_rows[t], slot)

        @pl.when(t == used - 1)
        def _():
            @pl.when(t >= 1)
            def _():
                wait_scatter(tile_rows[jnp.maximum(t - 1, 0)], 1 - slot)
            wait_scatter(tile_rows[t], slot)


def _experts(h2, slot_src, slot_dst, tile_e, tile_rows, n_used, w_gu, b_gu, w_down, b_down):
    n_tok = h2.shape[0] // ROW_TILE
    n_tiles = slot_src.shape[0] // MOE_TILE
    by_e = lambda *blk: pl.BlockSpec((1,) + blk, lambda t, te, tr, nu: (te[t],) + (0,) * len(blk))
    hbm = pl.BlockSpec(memory_space=pl.ANY)
    this_tile = pl.BlockSpec((MOE_TILE,), lambda t, *_: (t,), memory_space=pltpu.SMEM)
    grid_spec = pltpu.PrefetchScalarGridSpec(
        num_scalar_prefetch=3,
        grid=(n_tiles,),
        in_specs=[this_tile,
                  pl.BlockSpec((MOE_TILE,), lambda t, *_: (jnp.minimum(t + 1, n_tiles - 1),), memory_space=pltpu.SMEM),
                  this_tile,
                  hbm,
                  by_e(D_MODEL, 2 * D_FF), by_e(1, 2 * D_FF), by_e(D_FF, D_MODEL), by_e(1, D_MODEL)],
        out_specs=hbm,
        scratch_shapes=[pltpu.VMEM((2, MOE_TILE * ROW_TILE, 128), F32), pltpu.VMEM((2, MOE_TILE * ROW_TILE, 128), F32),
                        pltpu.SemaphoreType.DMA((2,)), pltpu.SemaphoreType.DMA((2,)),
                        pltpu.VMEM((D_MODEL, 2 * D_FF), BF16), pltpu.VMEM((D_FF, D_MODEL), BF16)])
    return pl.pallas_call(
        _experts_body,
        grid_spec=grid_spec,
        out_shape=jax.ShapeDtypeStruct((TOP_K * n_tok * ROW_TILE, 128), F32),
        compiler_params=_params("arbitrary"),
        name="experts",
    )(tile_e, tile_rows, n_used, slot_src, slot_src, slot_dst, h2, w_gu,
      b_gu.reshape(N_EXPERTS, 1, 2 * D_FF), w_down, b_down.reshape(N_EXPERTS, 1, D_MODEL))


def _combine_body(*refs):
    y_refs = refs[:TOP_K]
    x1_ref, route_ref, gt_ref, gfin_ref, o_ref = refs[TOP_K:]
    tc = x1_ref.shape[0]
    route = route_ref[...]
    moe = route[:, TOP_K:TOP_K + 1] * _load_token_tiles(y_refs[0], tc)
    for t in range(1, TOP_K):
        moe = moe + route[:, TOP_K + t:TOP_K + t + 1] * _load_token_tiles(y_refs[t], tc)
    o_ref[...] = _rms(x1_ref[...] + gt_ref[...] * moe, gfin_ref[...])


def _combine(ys, first_tok, x1, route, gt, g_final):
    n = x1.shape[0]
    tc = COMBINE_TOKENS
    n_tok = ys.shape[0] // (TOP_K * ROW_TILE)
    assert n_tok % tc == 0 and first_tok % tc == 0
    mod_spec = (pl.BlockSpec((1, D_MODEL), lambda i: (0, 0)) if gt.shape[0] == 1
                else pl.BlockSpec((tc, D_MODEL), lambda i: (i, 0)))

    def y_spec(k):
        return pl.BlockSpec((tc * ROW_TILE, 128), lambda i: ((k * n_tok + first_tok) // tc + i, 0))

    return pl.pallas_call(
        _combine_body,
        grid=(n // tc,),
        in_specs=[y_spec(k) for k in range(TOP_K)] + [
            pl.BlockSpec((tc, D_MODEL), lambda i: (i, 0)),
            pl.BlockSpec((tc, ROUTE_LANES), lambda i: (i, 0)),
            mod_spec,
            pl.BlockSpec((1, D_MODEL), lambda i: (0, 0))],
        out_specs=pl.BlockSpec((tc, D_MODEL), lambda i: (i, 0)),
        out_shape=jax.ShapeDtypeStruct((n, D_MODEL), F32),
        compiler_params=_params("arbitrary"),
        name="combine",
    )(*([ys] * TOP_K), x1, route, gt, g_final.reshape(1, D_MODEL))


def _pages_keys_minor(cache):
    return cache.transpose(0, 2, 3, 1).reshape(cache.shape[0], ATT_WIDTH, PAGE_SIZE)


def _group_mean_matrix():
    g = np.arange(SGU_WIDTH) // HEAD_DIM
    return jnp.asarray((g[:, None] == g[None, :]).astype(np.float32) / HEAD_DIM, BF16)


def _key_offset_bias():
    off = np.arange(MOBA_BLOCK, dtype=np.float32)[None, :, None]
    per_head = np.broadcast_to(np.asarray(_SLOPES, np.float32)[:, None, None] * off,
                               (N_ATT_HEADS, MOBA_BLOCK, MOBA_BLOCK))
    n_groups = N_ATT_HEADS // HEADS_PER_GROUP
    grouped = per_head.reshape(n_groups, HEADS_PER_GROUP, MOBA_BLOCK, MOBA_BLOCK).transpose(0, 2, 1, 3)
    return jnp.asarray(grouped.reshape(n_groups, MOBA_BLOCK, HEADS_PER_GROUP * MOBA_BLOCK))


def kernel(x_prompt, x_sample, cache_k, cache_v, page_table, c_prompt, c_sample, g_mix, w_ada, b_ada, w_in,
           g_sgu, w_sgu, b_sgu, g_out_att, g_out_sgu, w_out, g_ffn, w_router, b_router, w_gu, b_gu, w_down,
           b_down, g_final):
    depth = w_in.shape[0]
    batch, seq, _ = x_prompt.shape
    dec_batch, dec_seq, _ = x_sample.shape
    assert depth == 1, "the final norm is fused into the last stage of the single layer"
    assert batch == 1 and seq % MOBA_BLOCK == 0
    n_dec = dec_batch * dec_seq
    assert n_dec % CHUNK == 0 and CHUNK % dec_seq == 0 and dec_seq <= MOBA_BLOCK
    n_tok = seq + n_dec
    n_pool = cache_k.shape[1]

    bd = _group_mean_matrix()
    kp = _key_offset_bias()
    xp = x_prompt.reshape(seq, D_MODEL)
    xs = x_sample.reshape(n_dec, D_MODEL)
    c_all = jnp.concatenate([c_prompt, c_sample], axis=0)
    c_rows = -(-c_all.shape[0] // 8) * 8
    c_all = jnp.pad(c_all, ((0, c_rows - c_all.shape[0]), (0, 0)))
    k_rows, v_rows, ks_rows, vs_rows, sgu_rows = [], [], [], [], []

    for l in range(depth):
        mod = _ada(c_all, w_ada[l], b_ada[l])
        mod_p = [mod[0:1, t * D_MODEL:(t + 1) * D_MODEL] for t in range(6)]
        mod_rows = jnp.broadcast_to(mod[1:1 + dec_batch, None, :], (dec_batch, dec_seq, 6 * D_MODEL))
        mod_rows = mod_rows.reshape(n_dec, 6 * D_MODEL)
        mod_s = [mod_rows[:, t * D_MODEL:(t + 1) * D_MODEL] for t in range(6)]
        w_in_bf = w_in[l].astype(BF16)
        w_out_bf = w_out[l].astype(BF16)
        gs_flat = g_sgu[l].reshape(SGU_WIDTH)

        q, k, v, ug, vsn, kb, vt, kmean = _inproj(xp, g_mix[l], mod_p[1], mod_p[0], w_in_bf, gs_flat, bd, True)
        attn = _moba_prompt(q, kb, vt, kmean.reshape(seq // MOBA_BLOCK, ATT_WIDTH), kp)
        bias_p = jnp.repeat(b_sgu[l].T, HEAD_DIM, axis=1)
        h2 = jnp.zeros((n_tok * ROW_TILE, 128), F32)
        x1_p, h2, route_p, cnt_p = _mix(attn, ug, vsn, xp, w_sgu[l], bias_p, g_out_att[l], g_out_sgu[l],
                                        w_out_bf, mod_p[2], mod_p[4], mod_p[3], g_ffn[l], w_router[l],
                                        b_router[l], jnp.zeros((1, N_EXPERTS), F32), MOBA_BLOCK, h2, 0)
        k_rows.append(k)
        v_rows.append(v)

        qs, ks, vs, ugs, vsns = _inproj(xs, g_mix[l], mod_s[1], mod_s[0], w_in_bf, gs_flat, bd, False)
        attn_s = _moba_decode(qs.reshape(dec_batch, dec_seq, ATT_WIDTH), ks.reshape(dec_batch, dec_seq, ATT_WIDTH),
                              vs.reshape(dec_batch, dec_seq, ATT_WIDTH),
                              _pages_keys_minor(cache_k[l]), _pages_keys_minor(cache_v[l]), page_table)
        eye = jnp.eye(CHUNK // dec_seq, dtype=F32)
        w_s = jax.vmap(lambda w: jnp.kron(eye, w))(w_sgu[l][:, :dec_seq, :dec_seq])
        bias_s = jnp.tile(jnp.repeat(b_sgu[l].T[:dec_seq], HEAD_DIM, axis=1), (CHUNK // dec_seq, 1))
        x1_s, h2, route_s, cnt = _mix(attn_s.reshape(n_dec, ATT_WIDTH), ugs, vsns, xs, w_s, bias_s, g_out_att[l],
                                      g_out_sgu[l], w_out_bf, mod_s[2], mod_s[4], mod_s[3], g_ffn[l],
                                      w_router[l], b_router[l], cnt_p, CHUNK, h2, seq)
        ks_rows.append(ks)
        vs_rows.append(vs)
        sgu_rows.append(vsns)

        route = jnp.concatenate([route_p, route_s], axis=0)
        e_idx = route[:, :TOP_K].astype(jnp.int32)
        rank = route[:, 2 * TOP_K:3 * TOP_K].astype(jnp.int32)
        cnt_i = cnt.reshape(N_EXPERTS).astype(jnp.int32)
        pcnt = (cnt_i + MOE_TILE - 1) // MOE_TILE * MOE_TILE
        pend = jnp.cumsum(pcnt)
        pstart = pend - pcnt
        dest = (pstart[e_idx] + rank).reshape(-1)
        n_tiles = -(-(n_tok * TOP_K) // MOE_TILE) + N_EXPERTS
        tile_start = jnp.arange(n_tiles, dtype=jnp.int32) * MOE_TILE
        tile_e = jnp.minimum(jnp.sum((pend[None, :] <= tile_start[:, None]).astype(jnp.int32), axis=1), N_EXPERTS - 1)
        n_used = (pend[-1:] // MOE_TILE).astype(jnp.int32)
        tile_rows = jnp.clip((pstart + cnt_i)[tile_e] - tile_start, 0, MOE_TILE).astype(jnp.int32)
        slot_assign = jnp.zeros((n_tiles * MOE_TILE,), jnp.int32).at[dest].set(
            jnp.arange(n_tok * TOP_K, dtype=jnp.int32), unique_indices=True)
        slot_tok = slot_assign // TOP_K
        slot_src = slot_tok * ROW_TILE
        slot_dst = ((slot_assign % TOP_K) * n_tok + slot_tok) * ROW_TILE

        ys = _experts(h2, slot_src, slot_dst, tile_e, tile_rows, n_used, w_gu[l], b_gu[l], w_down[l], b_down[l])
        xp = _combine(ys, 0, x1_p, route_p, mod_p[5], g_final)
        xs = _combine(ys, seq, x1_s, route_s, mod_s[5], g_final)

    hd = (N_ATT_HEADS, HEAD_DIM)
    def paged(rows):
        t = jnp.stack(rows).reshape(depth, batch, seq // PAGE_SIZE, *hd, PAGE_SIZE)
        return t.transpose(0, 1, 2, 5, 3, 4)

    k_p = paged(k_rows)
    v_p = paged(v_rows)
    k_s = jnp.stack(ks_rows).reshape(depth, dec_batch, dec_seq, *hd)
    v_s = jnp.stack(vs_rows).reshape(depth, dec_batch, dec_seq, *hd)
    sgu_v = jnp.stack(sgu_rows).reshape(depth, dec_batch, dec_seq, N_SGU_GROUPS, HEAD_DIM)
    return (xp.reshape(batch, seq, D_MODEL), xs.reshape(dec_batch, dec_seq, D_MODEL), k_p, v_p, k_s, v_s, sgu_v)
```

```python
import functools

import numpy as np
import jax
import jax.numpy as jnp
from jax import lax
from jax.experimental import pallas as pl
from jax.experimental.pallas import tpu as pltpu

F32 = jnp.float32
BF16 = jnp.bfloat16

D_MODEL = 1024
HEAD_DIM = 64
N_ATT_HEADS = 8
N_SGU_GROUPS = 8
ATT_WIDTH = N_ATT_HEADS * HEAD_DIM
SGU_WIDTH = N_SGU_GROUPS * HEAD_DIM
IN_WIDTH = 3 * ATT_WIDTH + 2 * SGU_WIDTH
CHUNK = 128
MOBA_BLOCK = 256
MOBA_TOPK = 3
PAGE_SIZE = 128
N_EXPERTS = 32
TOP_K = 4
D_FF = D_MODEL
SWIGLU_LIMIT = 7.0
SWIGLU_ALPHA = 1.702
EPS = 1e-6
MASK_VALUE = -1e30

HEADS_PER_GROUP = 4
GROUP_WIDTH = HEADS_PER_GROUP * HEAD_DIM
ROUTE_LANES = 128
MOE_TILE = 256
KEY_BLOCKS_PER_STEP = 8
DMA_UNROLL = 8
COMBINE_TOKENS = 128
OWN_ROWS = 16
VMEM_LIMIT = 56 * 1024 * 1024

_SLOPES = [2.0 ** (-8.0 * (h + 1) / N_ATT_HEADS) for h in range(N_ATT_HEADS)]


def _params(*sem, flags=None):
    return pltpu.CompilerParams(dimension_semantics=sem, vmem_limit_bytes=VMEM_LIMIT, flags=flags)


def _split(a):
    hi = a.astype(BF16)
    lo = (a - hi.astype(F32)).astype(BF16)
    return hi, lo


def _dot(a, b):
    return jnp.dot(a, b, preferred_element_type=F32)


def _dot_nt(a, b):
    return lax.dot_general(a, b, (((1,), (1,)), ((), ())), preferred_element_type=F32)


def _dot3(a, b, dot=_dot):
    a_hi, a_lo = _split(a)
    b_hi, b_lo = _split(b)
    return dot(a_hi, b_hi) + (dot(a_hi, b_lo) + dot(a_lo, b_hi))


def _rms(x, g):
    return x * lax.rsqrt(jnp.mean(x * x, axis=-1, keepdims=True) + EPS) * g


def _gelu(x):
    return x * (0.5 * (1.0 + jnp.tanh(0.7978845608028654 * (x + 0.044715 * (x * x * x)))))


ROW_TILE = D_MODEL // 128


def _store_token_tiles(ref, x):
    for c in range(ROW_TILE):
        ref[pl.ds(c, x.shape[0], stride=ROW_TILE), :] = x[:, c * 128:(c + 1) * 128]


def _load_token_tiles(ref, rows):
    return jnp.concatenate([ref[pl.ds(c, rows, stride=ROW_TILE), :] for c in range(ROW_TILE)], axis=1)


def _tile_of(ref, token):
    return _tile_at(ref, token * ROW_TILE)


def _tile_at(ref, first_row):
    return ref.at[pl.ds(pl.multiple_of(first_row, ROW_TILE), ROW_TILE)]


def _tiles_of(ref, n_tokens):
    return ref.at[pl.ds(0, pl.multiple_of(n_tokens * ROW_TILE, ROW_TILE))]


def _ada_body(c_ref, w_ref, b_ref, o_ref):
    c = c_ref[...]
    s = c * (1.0 / (1.0 + jnp.exp(-c)))
    o_ref[...] = _dot3(s, w_ref[...]) + b_ref[...]


def _ada(c, w, b):
    rows = c.shape[0]
    n = w.shape[1]
    tn = 1536
    return pl.pallas_call(
        _ada_body,
        grid=(n // tn,),
        in_specs=[pl.BlockSpec((rows, D_MODEL), lambda j: (0, 0)),
                  pl.BlockSpec((D_MODEL, tn), lambda j: (0, j)),
                  pl.BlockSpec((1, tn), lambda j: (0, j))],
        out_specs=pl.BlockSpec((rows, tn), lambda j: (0, j)),
        out_shape=jax.ShapeDtypeStruct((rows, n), F32),
        compiler_params=_params("arbitrary"),
        name="ada",
    )(c, w, b.reshape(1, n))


def _inproj_body(x_ref, g_ref, sc_ref, sh_ref, w_ref, gs_ref, bd_ref, q_ref, k_ref, v_ref, ug_ref, vsn_ref,
                 *prompt_refs):
    h = _rms(x_ref[...], g_ref[...]) * (1.0 + sc_ref[...]) + sh_ref[...]
    z = _dot(h.astype(BF16), w_ref[...])
    a = ATT_WIDTH
    q_ref[...] = z[:, :a]
    k = z[:, a:2 * a]
    v = z[:, 2 * a:3 * a]
    if prompt_refs:
        vt_pages = []
        for pg in range(k_ref.shape[0]):
            rows = slice(pg * PAGE_SIZE, (pg + 1) * PAGE_SIZE)
            k_ref[pg] = k[rows].T
            vt_pages.append(v[rows].T)
            v_ref[pg] = vt_pages[-1]
    else:
        k_ref[...] = k
        v_ref[...] = v
    ug_ref[...] = _gelu(z[:, 3 * a:3 * a + SGU_WIDTH])
    vg = _gelu(z[:, 3 * a + SGU_WIDTH:])
    sq_hi, sq_lo = _split(vg * vg)
    gmean = _dot(sq_hi, bd_ref[...]) + _dot(sq_lo, bd_ref[...])
    vsn_ref[...] = vg * lax.rsqrt(gmean + EPS) * gs_ref[...]
    if prompt_refs:
        kb_ref, vt_ref, km_ref = prompt_refs
        kb_ref[...] = k.astype(BF16)
        vt_ref[...] = jnp.concatenate(vt_pages, axis=1).astype(BF16)
        km_ref[0] = jnp.mean(k, axis=0, keepdims=True)


def _inproj(x, g_mix, sc, sh, w_in_bf, g_sgu, bd, prompt):
    n = x.shape[0]
    tm = MOBA_BLOCK if prompt else n
    mod_rows = sc.shape[0]
    mod_spec = (pl.BlockSpec((1, D_MODEL), lambda i: (0, 0)) if mod_rows == 1
                else pl.BlockSpec((tm, D_MODEL), lambda i: (i, 0)))
    row = lambda w: pl.BlockSpec((tm, w), lambda i: (i, 0))
    if prompt:
        pages = tm // PAGE_SIZE
        kv_spec = pl.BlockSpec((pages, ATT_WIDTH, PAGE_SIZE), lambda i: (i, 0, 0))
        kv_shape = jax.ShapeDtypeStruct((n // PAGE_SIZE, ATT_WIDTH, PAGE_SIZE), F32)
    else:
        kv_spec, kv_shape = row(ATT_WIDTH), jax.ShapeDtypeStruct((n, ATT_WIDTH), F32)
    out_specs = [row(ATT_WIDTH), kv_spec, kv_spec] + [row(SGU_WIDTH)] * 2
    out_shape = [jax.ShapeDtypeStruct((n, ATT_WIDTH), F32), kv_shape, kv_shape] + \
        [jax.ShapeDtypeStruct((n, SGU_WIDTH), F32)] * 2
    if prompt:
        out_specs += [row(ATT_WIDTH), pl.BlockSpec((ATT_WIDTH, tm), lambda i: (0, i)),
                      pl.BlockSpec((1, 1, ATT_WIDTH), lambda i: (i, 0, 0))]
        out_shape += [jax.ShapeDtypeStruct((n, ATT_WIDTH), BF16), jax.ShapeDtypeStruct((ATT_WIDTH, n), BF16),
                      jax.ShapeDtypeStruct((n // tm, 1, ATT_WIDTH), F32)]
    return pl.pallas_call(
        _inproj_body,
        grid=(n // tm,),
        in_specs=[row(D_MODEL),
                  pl.BlockSpec((1, D_MODEL), lambda i: (0, 0)),
                  mod_spec, mod_spec,
                  pl.BlockSpec((D_MODEL, IN_WIDTH), lambda i: (0, 0)),
                  pl.BlockSpec((1, SGU_WIDTH), lambda i: (0, 0)),
                  pl.BlockSpec((SGU_WIDTH, SGU_WIDTH), lambda i: (0, 0))],
        out_specs=out_specs,
        out_shape=out_shape,
        compiler_params=_params("arbitrary"),
        name="inproj_prompt" if prompt else "inproj_sample",
    )(x, g_mix.reshape(1, D_MODEL), sc, sh, w_in_bf, g_sgu.reshape(1, SGU_WIDTH), bd)


def _select_bias(gate, n_past, axis):
    blk = lax.broadcasted_iota(jnp.int32, gate.shape, axis).astype(F32)
    past = blk < n_past
    gate = jnp.where(past, gate, -jnp.inf)
    bias = jnp.full(gate.shape, MASK_VALUE, F32)
    for _ in range(MOBA_TOPK):
        mx = jnp.max(gate, axis=axis, keepdims=True)
        first = jnp.min(jnp.where(gate == mx, blk, float(gate.shape[axis])), axis=axis, keepdims=True)
        hit = blk == first
        bias = jnp.where(hit, 0.0, bias)
        gate = jnp.where(hit, -jnp.inf, gate)
    return jnp.where(past, bias, MASK_VALUE)


def _moba_body(i_of, p_of, q_ref, kb_ref, vt_ref, km_ref, kp_ref, o_ref, qm_ref, sel_ref, m_ref, l_ref, acc_ref,
               s_ref):
    s = pl.program_id(0)
    i = i_of[s]
    pair = p_of[s]
    own_pair = i // KEY_BLOCKS_PER_STEP
    bq = MOBA_BLOCK
    wide = HEADS_PER_GROUP * bq
    n_groups = N_ATT_HEADS // HEADS_PER_GROUP
    head_of_row = lax.broadcasted_iota(jnp.int32, (GROUP_WIDTH, bq), 0) // HEAD_DIM

    def sweep(diag, half):
        j = KEY_BLOCKS_PER_STEP * pair + half
        keys = slice(half * bq, (half + 1) * bq)
        col = lax.broadcasted_iota(jnp.int32, (1, wide), 1)
        dist0 = ((i - j) * MOBA_BLOCK + col % bq).astype(F32)
        if diag:
            key = lax.broadcasted_iota(jnp.int32, (bq, wide), 0)
            qry = lax.broadcasted_iota(jnp.int32, (bq, wide), 1) % bq
            causal = jnp.where(key <= qry, 0.0, MASK_VALUE)
        for g in range(n_groups):
            slope = jnp.zeros((1, wide), F32)
            for hl in range(HEADS_PER_GROUP):
                slope = jnp.where(col // bq == hl, _SLOPES[g * HEADS_PER_GROUP + hl], slope)
            lanes = slice(g * GROUP_WIDTH, (g + 1) * GROUP_WIDTH)
            s0 = _dot(kb_ref[keys, lanes], qm_ref[g]) + kp_ref[g]
            brow = -slope * dist0
            if diag:
                s0 = s0 + causal
                m_old = jnp.full((1, wide), MASK_VALUE, F32)
                l_old = jnp.zeros((1, wide), F32)
            else:
                brow = brow + sel_ref[g, pl.ds(j, 1), :]
                m_old = m_ref[g:g + 1, :]
                l_old = l_ref[g:g + 1, :]
            s_ref[g] = s0
            m_new = jnp.maximum(m_old, jnp.max(s_ref[g], axis=0, keepdims=True) + brow)
            p32 = jnp.exp(s_ref[g] - (m_new - brow))
            alpha = jnp.exp(m_old - m_new)
            l_ref[g:g + 1, :] = alpha * l_old + jnp.sum(p32, axis=0, keepdims=True)
            m_ref[g:g + 1, :] = m_new
            p = p32.astype(BF16)
            vt = vt_ref[lanes, keys]
            vt_heads = jnp.concatenate([jnp.where(head_of_row == hl, vt, jnp.zeros_like(vt))
                                        for hl in range(HEADS_PER_GROUP)], axis=1)
            p_heads = jnp.concatenate([p[:, hl * bq:(hl + 1) * bq] for hl in range(HEADS_PER_GROUP)], axis=0)
            pv = _dot(vt_heads, p_heads)
            if diag:
                acc_ref[g] = pv
            else:
                alpha_rows = jnp.concatenate(
                    [jnp.broadcast_to(alpha[:, hl * bq:(hl + 1) * bq], (HEAD_DIM, bq)) for hl in range(HEADS_PER_GROUP)],
                    axis=0)
                acc_ref[g] = alpha_rows * acc_ref[g] + pv

    @pl.when(pair == own_pair)
    def _first():
        qt = (q_ref[...] * (HEAD_DIM ** -0.5)).T
        for g in range(n_groups):
            lanes = slice(g * GROUP_WIDTH, (g + 1) * GROUP_WIDTH)
            for hl in range(HEADS_PER_GROUP):
                qm = jnp.where(head_of_row == hl, qt[lanes, :], 0.0)
                qm_ref[g, :, hl * bq:(hl + 1) * bq] = qm.astype(BF16)
                gate = _dot3(km_ref[:, lanes], qm)
                sel_ref[g, :, hl * bq:(hl + 1) * bq] = _select_bias(gate, i.astype(F32), 0)

    for own_at in range(KEY_BLOCKS_PER_STEP):
        @pl.when((pair == own_pair) & (i % KEY_BLOCKS_PER_STEP == own_at))
        def _own_run(own_at=own_at):
            sweep(True, own_at)
            for half in range(own_at):
                sweep(False, half)

    @pl.when(pair != own_pair)
    def _past():
        for half in range(KEY_BLOCKS_PER_STEP):
            sweep(False, half)

    @pl.when((pair == own_pair - 1) | (own_pair == 0))
    def _last():
        outs = []
        for g in range(n_groups):
            inv = jnp.concatenate(
                [jnp.broadcast_to(l_ref[g:g + 1, hl * bq:(hl + 1) * bq], (HEAD_DIM, bq)) for hl in range(HEADS_PER_GROUP)],
                axis=0)
            outs.append(acc_ref[g] / inv)
        o_ref[...] = jnp.concatenate(outs, axis=0).T


def _moba_prompt(q, kb, vt, kmean, kp):
    t = q.shape[0]
    nb = t // MOBA_BLOCK
    per = KEY_BLOCKS_PER_STEP
    assert nb % per == 0
    i_of = np.concatenate([np.full(i // per + 1, i, np.int32) for i in range(nb)])
    p_of = np.concatenate([np.concatenate([[i // per], np.arange(i // per)]).astype(np.int32) for i in range(nb)])
    n_groups = N_ATT_HEADS // HEADS_PER_GROUP
    wide = HEADS_PER_GROUP * MOBA_BLOCK
    grid_spec = pltpu.PrefetchScalarGridSpec(
        num_scalar_prefetch=2,
        grid=(len(i_of),),
        in_specs=[pl.BlockSpec((MOBA_BLOCK, ATT_WIDTH), lambda s, io, po: (io[s], 0)),
                  pl.BlockSpec((per * MOBA_BLOCK, ATT_WIDTH), lambda s, io, po: (po[s], 0)),
                  pl.BlockSpec((ATT_WIDTH, per * MOBA_BLOCK), lambda s, io, po: (0, po[s])),
                  pl.BlockSpec((nb, ATT_WIDTH), lambda s, io, jo: (0, 0)),
                  pl.BlockSpec((n_groups, MOBA_BLOCK, wide), lambda s, io, jo: (0, 0, 0))],
        out_specs=pl.BlockSpec((MOBA_BLOCK, ATT_WIDTH), lambda s, io, jo: (io[s], 0)),
        scratch_shapes=[pltpu.VMEM((n_groups, GROUP_WIDTH, wide), BF16),
                        pltpu.VMEM((n_groups, nb, wide), F32),
                        pltpu.VMEM((n_groups, wide), F32),
                        pltpu.VMEM((n_groups, wide), F32),
                        pltpu.VMEM((n_groups, GROUP_WIDTH, MOBA_BLOCK), F32),
                        pltpu.VMEM((n_groups, MOBA_BLOCK, wide), F32)])
    return pl.pallas_call(
        _moba_body,
        grid_spec=grid_spec,
        out_shape=jax.ShapeDtypeStruct((t, ATT_WIDTH), F32),
        compiler_params=_params("arbitrary"),
        name="moba_prompt",
    )(jnp.asarray(i_of), jnp.asarray(p_of), q, kb, vt, kmean, kp)


def _decode_body(pt_ref, q_ref, kn_ref, vn_ref, *refs, pages, n_blocks, past_len, n_q):
    k_pages = refs[:pages]
    v_pages = refs[pages:2 * pages]
    o_ref = refs[2 * pages]
    gate_ref, op_ref, m_ref, l_ref = refs[2 * pages + 1:]
    step = pl.program_id(1)
    rows = N_ATT_HEADS * n_q
    head_of_row = lax.broadcasted_iota(jnp.int32, (rows, ATT_WIDTH), 0) % N_ATT_HEADS
    head_of_lane = lax.broadcasted_iota(jnp.int32, (rows, ATT_WIDTH), 1) // HEAD_DIM
    own_head = head_of_row == head_of_lane
    rid = lax.broadcasted_iota(jnp.int32, (rows, 1), 0)
    slope = jnp.zeros((rows, 1), F32)
    for h in range(N_ATT_HEADS):
        slope = jnp.where(rid % N_ATT_HEADS == h, _SLOPES[h], slope)
    qidx = rid // N_ATT_HEADS
    qpos = (past_len + qidx).astype(F32)
    blk_lane = lax.broadcasted_iota(jnp.int32, (rows, 128), 1)

    @pl.when(step == 0)
    def _init():
        m_ref[...] = jnp.full(m_ref.shape, MASK_VALUE, F32)
        l_ref[...] = jnp.zeros(l_ref.shape, F32)
        gate_ref[...] = jnp.zeros(gate_ref.shape, F32)

    qm_bf = jnp.where(own_head, q_ref[0] * (HEAD_DIM ** -0.5), 0.0).astype(BF16)
    blocks = range(pages // 2)
    blk = [step * (pages // 2) + b for b in blocks]
    qk = [_dot(qm_bf, jnp.concatenate([k_pages[2 * b][0], k_pages[2 * b + 1][0]], axis=1).astype(BF16))
          for b in blocks]
    gsum = [jnp.sum(qk[b], axis=-1, keepdims=True) * (1.0 / MOBA_BLOCK) for b in blocks]
    key = lax.broadcasted_iota(jnp.int32, (1, MOBA_BLOCK), 1)
    s = [qk[b] - slope * (qpos - (blk[b] * MOBA_BLOCK + key).astype(F32)) for b in blocks]
    m = [jnp.max(s[b], axis=-1, keepdims=True) for b in blocks]
    p = [jnp.exp(s[b] - m[b]) for b in blocks]
    l = [jnp.sum(p[b], axis=-1, keepdims=True) for b in blocks]
    gate, m_all, l_all = gate_ref[...], m_ref[...], l_ref[...]
    for b in blocks:
        here = blk_lane == blk[b]
        gate = jnp.where(here, gsum[b], gate)
        m_all = jnp.where(here, m[b], m_all)
        l_all = jnp.where(here, l[b], l_all)
    gate_ref[...], m_ref[...], l_ref[...] = gate, m_all, l_all
    for b in blocks:
        vt = jnp.concatenate([v_pages[2 * b][0], v_pages[2 * b + 1][0]], axis=1).astype(BF16)
        op_ref[blk[b]] = jnp.where(own_head, _dot_nt(p[b].astype(BF16), vt), 0.0)

    @pl.when(step == pl.num_programs(1) - 1)
    def _merge():
        sel = _select_bias(gate_ref[...], float(n_blocks), 1)
        kn = kn_ref[0]
        vn = vn_ref[0]
        kidx = lax.broadcasted_iota(jnp.int32, (rows, kn.shape[0]), 1)
        dist = (qidx - kidx).astype(F32)
        s_own = _dot_nt(qm_bf, kn.astype(BF16)) - slope * dist
        s_own = jnp.where(dist >= 0, s_own, MASK_VALUE)
        m_blk = m_ref[...] + sel
        m_all = jnp.maximum(jnp.max(m_blk, axis=-1, keepdims=True), jnp.max(s_own, axis=-1, keepdims=True))
        w = jnp.exp(m_blk - m_all)
        p_own = jnp.exp(s_own - m_all)
        den = jnp.sum(w * l_ref[...], axis=-1, keepdims=True) + jnp.sum(p_own, axis=-1, keepdims=True)
        acc = jnp.where(own_head, _dot(p_own.astype(BF16), vn.astype(BF16)), 0.0)
        for b in range(n_blocks):
            acc = acc + w[:, b:b + 1] * op_ref[b]
        acc = acc / den
        for qi in range(n_q):
            o_ref[0, qi:qi + 1, :] = jnp.sum(acc[qi * N_ATT_HEADS:(qi + 1) * N_ATT_HEADS], axis=0, keepdims=True)


def _moba_decode(q, k_new, v_new, cache_k, cache_v, page_table):
    b, n_q, _ = q.shape
    n_pages = page_table.shape[1]
    past_len = n_pages * PAGE_SIZE
    n_blocks = past_len // MOBA_BLOCK
    assert n_blocks <= 128 and n_pages % 2 == 0
    pages = next(p for p in (16, 8, 4, 2) if n_pages % p == 0)
    rows = N_ATT_HEADS * n_q
    q_rows = jnp.repeat(q, N_ATT_HEADS, axis=1)
    pad = ((0, 0), (0, OWN_ROWS - n_q), (0, 0))
    by_seq = lambda r: pl.BlockSpec((1, r, ATT_WIDTH), lambda bi, st, pt: (bi, 0, 0))

    def page_spec(t):
        return pl.BlockSpec((1, ATT_WIDTH, PAGE_SIZE), lambda bi, st, pt: (pt[bi, st * pages + t], 0, 0))

    grid_spec = pltpu.PrefetchScalarGridSpec(
        num_scalar_prefetch=1,
        grid=(b, n_pages // pages),
        in_specs=[by_seq(rows), by_seq(OWN_ROWS), by_seq(OWN_ROWS)] + [page_spec(t) for t in range(pages)] * 2,
        out_specs=by_seq(n_q),
        scratch_shapes=[pltpu.VMEM((rows, 128), F32),
                        pltpu.VMEM((n_blocks, rows, ATT_WIDTH), F32),
                        pltpu.VMEM((rows, 128), F32),
                        pltpu.VMEM((rows, 128), F32)])
    return pl.pallas_call(
        functools.partial(_decode_body, pages=pages, n_blocks=n_blocks, past_len=past_len, n_q=n_q),
        grid_spec=grid_spec,
        out_shape=jax.ShapeDtypeStruct((b, n_q, ATT_WIDTH), F32),
        compiler_params=_params("arbitrary", "arbitrary"),
        name="moba_decode",
    )(page_table, q_rows, jnp.pad(k_new, pad), jnp.pad(v_new, pad), *([cache_k] * pages), *([cache_v] * pages))


def _mix_body(attn_ref, ug_ref, vsn_ref, x_ref, wsg_ref, bsg_ref, ga_ref, gs_ref, wo_ref, gt_ref, sc_ref, sh_ref,
              gf_ref, wr_ref, br_ref, cin_ref, h2_all_ref, x1_ref, h2_ref, route_ref, cnt_ref, carry_ref):
    del h2_all_ref
    tm = x_ref.shape[0]

    @pl.when(pl.program_id(0) == 0)
    def _():
        carry_ref[...] = cin_ref[...]

    r = lax.broadcasted_iota(jnp.int32, (CHUNK, CHUNK), 0)
    c = lax.broadcasted_iota(jnp.int32, (CHUNK, CHUNK), 1)
    wcat = jnp.concatenate([jnp.where(c <= r, wsg_ref[g], 0.0).astype(BF16) for g in range(N_SGU_GROUPS)], axis=1)
    lane_group = lax.broadcasted_iota(jnp.int32, (CHUNK, SGU_WIDTH), 1) // HEAD_DIM
    mixed = []
    for ch in range(tm // CHUNK):
        vs = vsn_ref[ch * CHUNK:(ch + 1) * CHUNK, :]
        stack = jnp.concatenate([jnp.where(lane_group == g, vs, 0.0).astype(BF16) for g in range(N_SGU_GROUPS)],
                                axis=0)
        mixed.append(_dot(wcat, stack) + bsg_ref[...])
    sgu = ug_ref[...] * (mixed[0] if len(mixed) == 1 else jnp.concatenate(mixed, axis=0))
    merged = jnp.concatenate([_rms(attn_ref[...], ga_ref[...]), _rms(sgu, gs_ref[...])], axis=1)
    x1 = x_ref[...] + gt_ref[...] * _dot(merged.astype(BF16), wo_ref[...])
    x1_ref[...] = x1
    h2 = _rms(x1, gf_ref[...]) * (1.0 + sc_ref[...]) + sh_ref[...]
    _store_token_tiles(h2_ref, h2)

    logits = _dot3(h2, wr_ref[...]) + br_ref[...]
    lane_f = lax.broadcasted_iota(jnp.int32, (tm, N_EXPERTS), 1).astype(F32)
    vals, hits = [], []
    work = logits
    for _ in range(TOP_K):
        mx = jnp.max(work, axis=-1, keepdims=True)
        first = jnp.min(jnp.where(work == mx, lane_f, float(N_EXPERTS)), axis=-1, keepdims=True)
        hit = lane_f == first
        vals.append(mx)
        hits.append(hit)
        work = jnp.where(hit, -jnp.inf, work)
    ex = [jnp.exp(v - vals[0]) for v in vals]
    den = ex[0] + ex[1] + ex[2] + ex[3]
    chosen = jnp.zeros((tm, N_EXPERTS), F32)
    for hit in hits:
        chosen = jnp.where(hit, 1.0, chosen)
    rr = lax.broadcasted_iota(jnp.int32, (tm, tm), 0)
    cc = lax.broadcasted_iota(jnp.int32, (tm, tm), 1)
    before = _dot(jnp.where(cc < rr, 1.0, 0.0).astype(BF16), chosen.astype(BF16)) + carry_ref[...]
    rl = lax.broadcasted_iota(jnp.int32, (tm, ROUTE_LANES), 1)
    route = jnp.zeros((tm, ROUTE_LANES), F32)
    for t in range(TOP_K):
        idx = jnp.sum(jnp.where(hits[t], lane_f, 0.0), axis=-1, keepdims=True)
        rank = jnp.sum(jnp.where(hits[t], before, 0.0), axis=-1, keepdims=True)
        route = jnp.where(rl == t, idx, route)
        route = jnp.where(rl == TOP_K + t, ex[t] / den, route)
        route = jnp.where(rl == 2 * TOP_K + t, rank, route)
    route_ref[...] = route
    carry_ref[...] = carry_ref[...] + jnp.sum(chosen, axis=0, keepdims=True)
    cnt_ref[...] = carry_ref[...]


def _mix(attn, ug, vsn, x, wsg, bsg, g_att, g_sgu, w_out_bf, gt, sc, sh, g_ffn, w_router, b_router, cnt_in, tm,
         h2_all, first_tok):
    n = x.shape[0]
    assert first_tok % tm == 0
    first_blk = first_tok // tm
    mod_rows = gt.shape[0]
    mod_spec = (pl.BlockSpec((1, D_MODEL), lambda i: (0, 0)) if mod_rows == 1
                else pl.BlockSpec((tm, D_MODEL), lambda i: (i, 0)))
    row = lambda w: pl.BlockSpec((tm, w), lambda i: (i, 0))
    full = lambda *shape: pl.BlockSpec(shape, lambda i: (0,) * len(shape))
    return pl.pallas_call(
        _mix_body,
        grid=(n // tm,),
        in_specs=[row(ATT_WIDTH), row(SGU_WIDTH), row(SGU_WIDTH), row(D_MODEL),
                  full(N_SGU_GROUPS, CHUNK, CHUNK), full(CHUNK, SGU_WIDTH),
                  full(1, ATT_WIDTH), full(1, SGU_WIDTH), full(D_MODEL, D_MODEL),
                  mod_spec, mod_spec, mod_spec,
                  full(1, D_MODEL), full(D_MODEL, N_EXPERTS), full(1, N_EXPERTS), full(1, N_EXPERTS),
                  pl.BlockSpec(memory_space=pl.ANY)],
        out_specs=[row(D_MODEL), pl.BlockSpec((tm * ROW_TILE, 128), lambda i: (first_blk + i, 0)), row(ROUTE_LANES),
                   full(1, N_EXPERTS)],
        out_shape=[jax.ShapeDtypeStruct((n, D_MODEL), F32), jax.ShapeDtypeStruct(h2_all.shape, F32),
                   jax.ShapeDtypeStruct((n, ROUTE_LANES), F32), jax.ShapeDtypeStruct((1, N_EXPERTS), F32)],
        scratch_shapes=[pltpu.VMEM((1, N_EXPERTS), F32)],
        input_output_aliases={16: 1},
        compiler_params=_params("arbitrary"),
        name="mix",
    )(attn, ug, vsn, x, wsg, bsg, g_att.reshape(1, ATT_WIDTH), g_sgu.reshape(1, SGU_WIDTH), w_out_bf,
      gt, sc, sh, g_ffn.reshape(1, D_MODEL), w_router, b_router.reshape(1, N_EXPERTS), cnt_in, h2_all)


def _experts_body(tile_e, tile_rows, n_used, src_ref, src_next_ref, dst_ref, h2_ref, wgu_ref, bgu_ref,
                  wd_ref, bd_ref, y_ref, xbuf, obuf, gsem, ssem, wgu_bf, wd_bf):
    t = pl.program_id(0)
    used = n_used[0]
    slot = t % 2

    def start_gather(src, into):
        def body(r, carry):
            pltpu.make_async_copy(_tile_at(h2_ref, src[r]), _tile_of(xbuf.at[into], r), gsem.at[into]).start()
            return carry
        lax.fori_loop(0, MOE_TILE, body, 0, unroll=DMA_UNROLL)

    def wait_gather(into):
        pltpu.make_async_copy(_tiles_of(h2_ref, MOE_TILE), _tiles_of(y_ref, MOE_TILE), gsem.at[into]).wait()

    def start_scatter(rows, frm):
        def body(r, carry):
            first_row = dst_ref[r]

            @pl.when(r < rows)
            def _():
                pltpu.make_async_copy(_tile_of(obuf.at[frm], r), _tile_at(y_ref, first_row), ssem.at[frm]).start()
            return carry
        lax.fori_loop(0, MOE_TILE, body, 0, unroll=DMA_UNROLL)

    def wait_scatter(rows, frm):
        pltpu.make_async_copy(_tiles_of(h2_ref, rows), _tiles_of(y_ref, rows), ssem.at[frm]).wait()

    @pl.when(t == 0)
    def _():
        start_gather(src_ref, 0)

    @pl.when(t + 1 < used)
    def _():
        start_gather(src_next_ref, 1 - slot)

    e = tile_e[t]
    prev = tile_e[jnp.maximum(t - 1, 0)]

    @pl.when((t == 0) | (e != prev))
    def _():
        wgu_bf[...] = wgu_ref[0].astype(BF16)
        wd_bf[...] = wd_ref[0].astype(BF16)

    @pl.when(t < used)
    def _():
        wait_gather(slot)

        @pl.when(t >= 2)
        def _():
            wait_scatter(tile_rows[jnp.maximum(t - 2, 0)], slot)

        x = _load_token_tiles(xbuf.at[slot], MOE_TILE)
        gu = _dot(x.astype(BF16), wgu_bf[...]) + bgu_ref[0]
        g = jnp.minimum(gu[:, :D_FF], SWIGLU_LIMIT)
        lin = jnp.clip(gu[:, D_FF:], -SWIGLU_LIMIT, SWIGLU_LIMIT)
        act = g * (1.0 / (1.0 + jnp.exp(-SWIGLU_ALPHA * g))) * (lin + 1.0)
        out = _dot(act.astype(BF16), wd_bf[...]) + bd_ref[0]
        _store_token_tiles(obuf.at[slot], out)
        start_scatter(tile_rows[t], slot)

        @pl.when(t == used - 1)
        def _():
            @pl.when(t >= 1)
            def _():
                wait_scatter(tile_rows[jnp.maximum(t - 1, 0)], 1 - slot)
            wait_scatter(tile_rows[t], slot)


def _experts(h2, slot_src, slot_dst, tile_e, tile_rows, n_used, w_gu, b_gu, w_down, b_down):
    n_tok = h2.shape[0] // ROW_TILE
    n_tiles = slot_src.shape[0] // MOE_TILE
    by_e = lambda *blk: pl.BlockSpec((1,) + blk, lambda t, te, tr, nu: (te[t],) + (0,) * len(blk))
    hbm = pl.BlockSpec(memory_space=pl.ANY)
    this_tile = pl.BlockSpec((MOE_TILE,), lambda t, *_: (t,), memory_space=pltpu.SMEM)
    grid_spec = pltpu.PrefetchScalarGridSpec(
        num_scalar_prefetch=3,
        grid=(n_tiles,),
        in_specs=[this_tile,
                  pl.BlockSpec((MOE_TILE,), lambda t, *_: (jnp.minimum(t + 1, n_tiles - 1),), memory_space=pltpu.SMEM),
                  this_tile,
                  hbm,
                  by_e(D_MODEL, 2 * D_FF), by_e(1, 2 * D_FF), by_e(D_FF, D_MODEL), by_e(1, D_MODEL)],
        out_specs=hbm,
        scratch_shapes=[pltpu.VMEM((2, MOE_TILE * ROW_TILE, 128), F32), pltpu.VMEM((2, MOE_TILE * ROW_TILE, 128), F32),
                        pltpu.SemaphoreType.DMA((2,)), pltpu.SemaphoreType.DMA((2,)),
                        pltpu.VMEM((D_MODEL, 2 * D_FF), BF16), pltpu.VMEM((D_FF, D_MODEL), BF16)])
    return pl.pallas_call(
        _experts_body,
        grid_spec=grid_spec,
        out_shape=jax.ShapeDtypeStruct((TOP_K * n_tok * ROW_TILE, 128), F32),
        compiler_params=_params("arbitrary"),
        name="experts",
    )(tile_e, tile_rows, n_used, slot_src, slot_src, slot_dst, h2, w_gu,
      b_gu.reshape(N_EXPERTS, 1, 2 * D_FF), w_down, b_down.reshape(N_EXPERTS, 1, D_MODEL))


def _combine_body(*refs):
    y_refs = refs[:TOP_K]
    x1_ref, route_ref, gt_ref, gfin_ref, o_ref = refs[TOP_K:]
    tc = x1_ref.shape[0]
    route = route_ref[...]
    moe = route[:, TOP_K:TOP_K + 1] * _load_token_tiles(y_refs[0], tc)
    for t in range(1, TOP_K):
        moe = moe + route[:, TOP_K + t:TOP_K + t + 1] * _load_token_tiles(y_refs[t], tc)
    o_ref[...] = _rms(x1_ref[...] + gt_ref[...] * moe, gfin_ref[...])


def _combine(ys, first_tok, x1, route, gt, g_final):
    n = x1.shape[0]
    tc = COMBINE_TOKENS
    n_tok = ys.shape[0] // (TOP_K * ROW_TILE)
    assert n_tok % tc == 0 and first_tok % tc == 0
    mod_spec = (pl.BlockSpec((1, D_MODEL), lambda i: (0, 0)) if gt.shape[0] == 1
                else pl.BlockSpec((tc, D_MODEL), lambda i: (i, 0)))

    def y_spec(k):
        return pl.BlockSpec((tc * ROW_TILE, 128), lambda i: ((k * n_tok + first_tok) // tc + i, 0))

    return pl.pallas_call(
        _combine_body,
        grid=(n // tc,),
        in_specs=[y_spec(k) for k in range(TOP_K)] + [
            pl.BlockSpec((tc, D_MODEL), lambda i: (i, 0)),
            pl.BlockSpec((tc, ROUTE_LANES), lambda i: (i, 0)),
            mod_spec,
            pl.BlockSpec((1, D_MODEL), lambda i: (0, 0))],
        out_specs=pl.BlockSpec((tc, D_MODEL), lambda i: (i, 0)),
        out_shape=jax.ShapeDtypeStruct((n, D_MODEL), F32),
        compiler_params=_params("arbitrary"),
        name="combine",
    )(*([ys] * TOP_K), x1, route, gt, g_final.reshape(1, D_MODEL))


def _pages_keys_minor(cache):
    return cache.transpose(0, 2, 3, 1).reshape(cache.shape[0], ATT_WIDTH, PAGE_SIZE)


def _group_mean_matrix():
    g = np.arange(SGU_WIDTH) // HEAD_DIM
    return jnp.asarray((g[:, None] == g[None, :]).astype(np.float32) / HEAD_DIM, BF16)


def _key_offset_bias():
    off = np.arange(MOBA_BLOCK, dtype=np.float32)[None, :, None]
    per_head = np.broadcast_to(np.asarray(_SLOPES, np.float32)[:, None, None] * off,
                               (N_ATT_HEADS, MOBA_BLOCK, MOBA_BLOCK))
    n_groups = N_ATT_HEADS // HEADS_PER_GROUP
    grouped = per_head.reshape(n_groups, HEADS_PER_GROUP, MOBA_BLOCK, MOBA_BLOCK).transpose(0, 2, 1, 3)
    return jnp.asarray(grouped.reshape(n_groups, MOBA_BLOCK, HEADS_PER_GROUP * MOBA_BLOCK))


def kernel(x_prompt, x_sample, cache_k, cache_v, page_table, c_prompt, c_sample, g_mix, w_ada, b_ada, w_in,
           g_sgu, w_sgu, b_sgu, g_out_att, g_out_sgu, w_out, g_ffn, w_router, b_router, w_gu, b_gu, w_down,
           b_down, g_final):
    depth = w_in.shape[0]
    batch, seq, _ = x_prompt.shape
    dec_batch, dec_seq, _ = x_sample.shape
    assert depth == 1, "the final norm is fused into the last stage of the single layer"
    assert batch == 1 and seq % MOBA_BLOCK == 0
    n_dec = dec_batch * dec_seq
    assert n_dec % CHUNK == 0 and CHUNK % dec_seq == 0 and dec_seq <= MOBA_BLOCK
    n_tok = seq + n_dec
    n_pool = cache_k.shape[1]

    bd = _group_mean_matrix()
    kp = _key_offset_bias()
    xp = x_prompt.reshape(seq, D_MODEL)
    xs = x_sample.reshape(n_dec, D_MODEL)
    c_all = jnp.concatenate([c_prompt, c_sample], axis=0)
    c_rows = -(-c_all.shape[0] // 8) * 8
    c_all = jnp.pad(c_all, ((0, c_rows - c_all.shape[0]), (0, 0)))
    k_rows, v_rows, ks_rows, vs_rows, sgu_rows = [], [], [], [], []

    for l in range(depth):
        mod = _ada(c_all, w_ada[l], b_ada[l])
        mod_p = [mod[0:1, t * D_MODEL:(t + 1) * D_MODEL] for t in range(6)]
        mod_rows = jnp.broadcast_to(mod[1:1 + dec_batch, None, :], (dec_batch, dec_seq, 6 * D_MODEL))
        mod_rows = mod_rows.reshape(n_dec, 6 * D_MODEL)
        mod_s = [mod_rows[:, t * D_MODEL:(t + 1) * D_MODEL] for t in range(6)]
        w_in_bf = w_in[l].astype(BF16)
        w_out_bf = w_out[l].astype(BF16)
        gs_flat = g_sgu[l].reshape(SGU_WIDTH)

        q, k, v, ug, vsn, kb, vt, kmean = _inproj(xp, g_mix[l], mod_p[1], mod_p[0], w_in_bf, gs_flat, bd, True)
        attn = _moba_prompt(q, kb, vt, kmean.reshape(seq // MOBA_BLOCK, ATT_WIDTH), kp)
        bias_p = jnp.repeat(b_sgu[l].T, HEAD_DIM, axis=1)
        h2 = jnp.zeros((n_tok * ROW_TILE, 128), F32)
        x1_p, h2, route_p, cnt_p = _mix(attn, ug, vsn, xp, w_sgu[l], bias_p, g_out_att[l], g_out_sgu[l],
                                        w_out_bf, mod_p[2], mod_p[4], mod_p[3], g_ffn[l], w_router[l],
                                        b_router[l], jnp.zeros((1, N_EXPERTS), F32), MOBA_BLOCK, h2, 0)
        k_rows.append(k)
        v_rows.append(v)

        qs, ks, vs, ugs, vsns = _inproj(xs, g_mix[l], mod_s[1], mod_s[0], w_in_bf, gs_flat, bd, False)
        attn_s = _moba_decode(qs.reshape(dec_batch, dec_seq, ATT_WIDTH), ks.reshape(dec_batch, dec_seq, ATT_WIDTH),
                              vs.reshape(dec_batch, dec_seq, ATT_WIDTH),
                              _pages_keys_minor(cache_k[l]), _pages_keys_minor(cache_v[l]), page_table)
        eye = jnp.eye(CHUNK // dec_seq, dtype=F32)
        w_s = jax.vmap(lambda w: jnp.kron(eye, w))(w_sgu[l][:, :dec_seq, :dec_seq])
        bias_s = jnp.tile(jnp.repeat(b_sgu[l].T[:dec_seq], HEAD_DIM, axis=1), (CHUNK // dec_seq, 1))
        x1_s, h2, route_s, cnt = _mix(attn_s.reshape(n_dec, ATT_WIDTH), ugs, vsns, xs, w_s, bias_s, g_out_att[l],
                                      g_out_sgu[l], w_out_bf, mod_s[2], mod_s[4], mod_s[3], g_ffn[l],
                                      w_router[l], b_router[l], cnt_p, CHUNK, h2, seq)
        ks_rows.append(ks)
        vs_rows.append(vs)
        sgu_rows.append(vsns)

        route = jnp.concatenate([route_p, route_s], axis=0)
        e_idx = route[:, :TOP_K].astype(jnp.int32)
        rank = route[:, 2 * TOP_K:3 * TOP_K].astype(jnp.int32)
        cnt_i = cnt.reshape(N_EXPERTS).astype(jnp.int32)
        pcnt = (cnt_i + MOE_TILE - 1) // MOE_TILE * MOE_TILE
        pend = jnp.cumsum(pcnt)
        pstart = pend - pcnt
        dest = (pstart[e_idx] + rank).reshape(-1)
        n_tiles = -(-(n_tok * TOP_K) // MOE_TILE) + N_EXPERTS
        tile_start = jnp.arange(n_tiles, dtype=jnp.int32) * MOE_TILE
        tile_e = jnp.minimum(jnp.sum((pend[None, :] <= tile_start[:, None]).astype(jnp.int32), axis=1), N_EXPERTS - 1)
        n_used = (pend[-1:] // MOE_TILE).astype(jnp.int32)
        tile_rows = jnp.clip((pstart + cnt_i)[tile_e] - tile_start, 0, MOE_TILE).astype(jnp.int32)
        slot_assign = jnp.zeros((n_tiles * MOE_TILE,), jnp.int32).at[dest].set(
            jnp.arange(n_tok * TOP_K, dtype=jnp.int32), unique_indices=True)
        slot_tok = slot_assign // TOP_K
        slot_src = slot_tok * ROW_TILE
        slot_dst = ((slot_assign % TOP_K) * n_tok + slot_tok) * ROW_TILE

        ys = _experts(h2, slot_src, slot_dst, tile_e, tile_rows, n_used, w_gu[l], b_gu[l], w_down[l], b_down[l])
        xp = _combine(ys, 0, x1_p, route_p, mod_p[5], g_final)
        xs = _combine(ys, seq, x1_s, route_s, mod_s[5], g_final)

    hd = (N_ATT_HEADS, HEAD_DIM)
    def paged(rows):
        t = jnp.stack(rows).reshape(depth, batch, seq // PAGE_SIZE, *hd, PAGE_SIZE)
        return t.transpose(0, 1, 2, 5, 3, 4)

    k_p = paged(k_rows)
    v_p = paged(v_rows)
    k_s = jnp.stack(ks_rows).reshape(depth, dec_batch, dec_seq, *hd)
    v_s = jnp.stack(vs_rows).reshape(depth, dec_batch, dec_seq, *hd)
    sgu_v = jnp.stack(sgu_rows).reshape(depth, dec_batch, dec_seq, N_SGU_GROUPS, HEAD_DIM)
    return (xp.reshape(batch, seq, D_MODEL), xs.reshape(dec_batch, dec_seq, D_MODEL), k_p, v_p, k_s, v_s, sgu_v)
```

```python
import functools

import numpy as np
import jax
import jax.numpy as jnp
from jax import lax
from jax.experimental import pallas as pl
from jax.experimental.pallas import tpu as pltpu

F32 = jnp.float32
BF16 = jnp.bfloat16

D_MODEL = 1024
HEAD_DIM = 64
N_ATT_HEADS = 8
N_SGU_GROUPS = 8
ATT_WIDTH = N_ATT_HEADS * HEAD_DIM
SGU_WIDTH = N_SGU_GROUPS * HEAD_DIM
IN_WIDTH = 3 * ATT_WIDTH + 2 * SGU_WIDTH
CHUNK = 128
MOBA_BLOCK = 256
MOBA_TOPK = 3
PAGE_SIZE = 128
N_EXPERTS = 32
TOP_K = 4
D_FF = D_MODEL
SWIGLU_LIMIT = 7.0
SWIGLU_ALPHA = 1.702
EPS = 1e-6
MASK_VALUE = -1e30

HEADS_PER_GROUP = 4
GROUP_WIDTH = HEADS_PER_GROUP * HEAD_DIM
ROUTE_LANES = 128
MOE_TILE = 256
KEY_BLOCKS_PER_STEP = 8
DMA_THREADS = 2
DMA_UNROLL = 8
COMBINE_TOKENS = 128
OWN_ROWS = 16
VMEM_LIMIT = 56 * 1024 * 1024

_SLOPES = [2.0 ** (-8.0 * (h + 1) / N_ATT_HEADS) for h in range(N_ATT_HEADS)]


def _params(*sem, flags=None):
    return pltpu.CompilerParams(dimension_semantics=sem, vmem_limit_bytes=VMEM_LIMIT, flags=flags)


def _split(a):
    hi = a.astype(BF16)
    lo = (a - hi.astype(F32)).astype(BF16)
    return hi, lo


def _dot(a, b):
    return jnp.dot(a, b, preferred_element_type=F32)


def _dot_nt(a, b):
    return lax.dot_general(a, b, (((1,), (1,)), ((), ())), preferred_element_type=F32)


def _dot3(a, b, dot=_dot):
    a_hi, a_lo = _split(a)
    b_hi, b_lo = _split(b)
    return dot(a_hi, b_hi) + (dot(a_hi, b_lo) + dot(a_lo, b_hi))


def _rms(x, g):
    return x * lax.rsqrt(jnp.mean(x * x, axis=-1, keepdims=True) + EPS) * g


def _gelu(x):
    return x * (0.5 * (1.0 + jnp.tanh(0.7978845608028654 * (x + 0.044715 * (x * x * x)))))


ROW_TILE = D_MODEL // 128


def _store_token_tiles(ref, x):
    for c in range(ROW_TILE):
        ref[pl.ds(c, x.shape[0], stride=ROW_TILE), :] = x[:, c * 128:(c + 1) * 128]


def _load_token_tiles(ref, rows):
    return jnp.concatenate([ref[pl.ds(c, rows, stride=ROW_TILE), :] for c in range(ROW_TILE)], axis=1)


def _tile_of(ref, token):
    return _tile_at(ref, token * ROW_TILE)


def _tile_at(ref, first_row):
    return ref.at[pl.ds(pl.multiple_of(first_row, ROW_TILE), ROW_TILE)]


def _tiles_of(ref, n_tokens):
    return ref.at[pl.ds(0, pl.multiple_of(n_tokens * ROW_TILE, ROW_TILE))]


def _ada_body(c_ref, w_ref, b_ref, o_ref):
    c = c_ref[...]
    s = c * (1.0 / (1.0 + jnp.exp(-c)))
    o_ref[...] = _dot3(s, w_ref[...]) + b_ref[...]


def _ada(c, w, b):
    rows = c.shape[0]
    n = w.shape[1]
    tn = 1536
    return pl.pallas_call(
        _ada_body,
        grid=(n // tn,),
        in_specs=[pl.BlockSpec((rows, D_MODEL), lambda j: (0, 0)),
                  pl.BlockSpec((D_MODEL, tn), lambda j: (0, j)),
                  pl.BlockSpec((1, tn), lambda j: (0, j))],
        out_specs=pl.BlockSpec((rows, tn), lambda j: (0, j)),
        out_shape=jax.ShapeDtypeStruct((rows, n), F32),
        compiler_params=_params("arbitrary"),
        name="ada",
    )(c, w, b.reshape(1, n))


def _inproj_body(x_ref, g_ref, sc_ref, sh_ref, w_ref, gs_ref, bd_ref, q_ref, k_ref, v_ref, ug_ref, vsn_ref,
                 *prompt_refs):
    h = _rms(x_ref[...], g_ref[...]) * (1.0 + sc_ref[...]) + sh_ref[...]
    z = _dot(h.astype(BF16), w_ref[...])
    a = ATT_WIDTH
    q_ref[...] = z[:, :a]
    k = z[:, a:2 * a]
    v = z[:, 2 * a:3 * a]
    if prompt_refs:
        vt_pages = []
        for pg in range(k_ref.shape[0]):
            rows = slice(pg * PAGE_SIZE, (pg + 1) * PAGE_SIZE)
            k_ref[pg] = k[rows].T
            vt_pages.append(v[rows].T)
            v_ref[pg] = vt_pages[-1]
    else:
        k_ref[...] = k
        v_ref[...] = v
    ug_ref[...] = _gelu(z[:, 3 * a:3 * a + SGU_WIDTH])
    vg = _gelu(z[:, 3 * a + SGU_WIDTH:])
    sq_hi, sq_lo = _split(vg * vg)
    gmean = _dot(sq_hi, bd_ref[...]) + _dot(sq_lo, bd_ref[...])
    vsn_ref[...] = vg * lax.rsqrt(gmean + EPS) * gs_ref[...]
    if prompt_refs:
        kb_ref, vt_ref, km_ref = prompt_refs
        kb_ref[...] = k.astype(BF16)
        vt_ref[...] = jnp.concatenate(vt_pages, axis=1).astype(BF16)
        km_ref[0] = jnp.mean(k, axis=0, keepdims=True)


def _inproj(x, g_mix, sc, sh, w_in_bf, g_sgu, bd, prompt):
    n = x.shape[0]
    tm = MOBA_BLOCK if prompt else n
    mod_rows = sc.shape[0]
    mod_spec = (pl.BlockSpec((1, D_MODEL), lambda i: (0, 0)) if mod_rows == 1
                else pl.BlockSpec((tm, D_MODEL), lambda i: (i, 0)))
    row = lambda w: pl.BlockSpec((tm, w), lambda i: (i, 0))
    if prompt:
        pages = tm // PAGE_SIZE
        kv_spec = pl.BlockSpec((pages, ATT_WIDTH, PAGE_SIZE), lambda i: (i, 0, 0))
        kv_shape = jax.ShapeDtypeStruct((n // PAGE_SIZE, ATT_WIDTH, PAGE_SIZE), F32)
    else:
        kv_spec, kv_shape = row(ATT_WIDTH), jax.ShapeDtypeStruct((n, ATT_WIDTH), F32)
    out_specs = [row(ATT_WIDTH), kv_spec, kv_spec] + [row(SGU_WIDTH)] * 2
    out_shape = [jax.ShapeDtypeStruct((n, ATT_WIDTH), F32), kv_shape, kv_shape] + \
        [jax.ShapeDtypeStruct((n, SGU_WIDTH), F32)] * 2
    if prompt:
        out_specs += [row(ATT_WIDTH), pl.BlockSpec((ATT_WIDTH, tm), lambda i: (0, i)),
                      pl.BlockSpec((1, 1, ATT_WIDTH), lambda i: (i, 0, 0))]
        out_shape += [jax.ShapeDtypeStruct((n, ATT_WIDTH), BF16), jax.ShapeDtypeStruct((ATT_WIDTH, n), BF16),
                      jax.ShapeDtypeStruct((n // tm, 1, ATT_WIDTH), F32)]
    return pl.pallas_call(
        _inproj_body,
        grid=(n // tm,),
        in_specs=[row(D_MODEL),
                  pl.BlockSpec((1, D_MODEL), lambda i: (0, 0)),
                  mod_spec, mod_spec,
                  pl.BlockSpec((D_MODEL, IN_WIDTH), lambda i: (0, 0)),
                  pl.BlockSpec((1, SGU_WIDTH), lambda i: (0, 0)),
                  pl.BlockSpec((SGU_WIDTH, SGU_WIDTH), lambda i: (0, 0))],
        out_specs=out_specs,
        out_shape=out_shape,
        compiler_params=_params("arbitrary"),
        name="inproj_prompt" if prompt else "inproj_sample",
    )(x, g_mix.reshape(1, D_MODEL), sc, sh, w_in_bf, g_sgu.reshape(1, SGU_WIDTH), bd)


def _select_bias(gate, n_past, axis):
    blk = lax.broadcasted_iota(jnp.int32, gate.shape, axis).astype(F32)
    past = blk < n_past
    gate = jnp.where(past, gate, -jnp.inf)
    bias = jnp.full(gate.shape, MASK_VALUE, F32)
    for _ in range(MOBA_TOPK):
        mx = jnp.max(gate, axis=axis, keepdims=True)
        first = jnp.min(jnp.where(gate == mx, blk, float(gate.shape[axis])), axis=axis, keepdims=True)
        hit = blk == first
        bias = jnp.where(hit, 0.0, bias)
        gate = jnp.where(hit, -jnp.inf, gate)
    return jnp.where(past, bias, MASK_VALUE)


def _moba_body(i_of, p_of, q_ref, kb_ref, vt_ref, km_ref, kp_ref, o_ref, qm_ref, sel_ref, m_ref, l_ref, acc_ref,
               s_ref):
    s = pl.program_id(0)
    i = i_of[s]
    pair = p_of[s]
    own_pair = i // KEY_BLOCKS_PER_STEP
    bq = MOBA_BLOCK
    wide = HEADS_PER_GROUP * bq
    n_groups = N_ATT_HEADS // HEADS_PER_GROUP
    head_of_row = lax.broadcasted_iota(jnp.int32, (GROUP_WIDTH, bq), 0) // HEAD_DIM

    def sweep(diag, half):
        j = KEY_BLOCKS_PER_STEP * pair + half
        keys = slice(half * bq, (half + 1) * bq)
        col = lax.broadcasted_iota(jnp.int32, (1, wide), 1)
        dist0 = ((i - j) * MOBA_BLOCK + col % bq).astype(F32)
        if diag:
            key = lax.broadcasted_iota(jnp.int32, (bq, wide), 0)
            qry = lax.broadcasted_iota(jnp.int32, (bq, wide), 1) % bq
            causal = jnp.where(key <= qry, 0.0, MASK_VALUE)
        for g in range(n_groups):
            slope = jnp.zeros((1, wide), F32)
            for hl in range(HEADS_PER_GROUP):
                slope = jnp.where(col // bq == hl, _SLOPES[g * HEADS_PER_GROUP + hl], slope)
            lanes = slice(g * GROUP_WIDTH, (g + 1) * GROUP_WIDTH)
            s0 = _dot(kb_ref[keys, lanes], qm_ref[g]) + kp_ref[g]
            brow = -slope * dist0
            if diag:
                s0 = s0 + causal
                m_old = jnp.full((1, wide), MASK_VALUE, F32)
                l_old = jnp.zeros((1, wide), F32)
            else:
                brow = brow + sel_ref[g, pl.ds(j, 1), :]
                m_old = m_ref[g:g + 1, :]
                l_old = l_ref[g:g + 1, :]
            s_ref[g] = s0
            m_new = jnp.maximum(m_old, jnp.max(s_ref[g], axis=0, keepdims=True) + brow)
            p32 = jnp.exp(s_ref[g] - (m_new - brow))
            alpha = jnp.exp(m_old - m_new)
            l_ref[g:g + 1, :] = alpha * l_old + jnp.sum(p32, axis=0, keepdims=True)
            m_ref[g:g + 1, :] = m_new
            p = p32.astype(BF16)
            vt = vt_ref[lanes, keys]
            vt_heads = jnp.concatenate([jnp.where(head_of_row == hl, vt, jnp.zeros_like(vt))
                                        for hl in range(HEADS_PER_GROUP)], axis=1)
            p_heads = jnp.concatenate([p[:, hl * bq:(hl + 1) * bq] for hl in range(HEADS_PER_GROUP)], axis=0)
            pv = _dot(vt_heads, p_heads)
            if diag:
                acc_ref[g] = pv
            else:
                alpha_rows = jnp.concatenate(
                    [jnp.broadcast_to(alpha[:, hl * bq:(hl + 1) * bq], (HEAD_DIM, bq)) for hl in range(HEADS_PER_GROUP)],
                    axis=0)
                acc_ref[g] = alpha_rows * acc_ref[g] + pv

    @pl.when(pair == own_pair)
    def _first():
        qt = (q_ref[...] * (HEAD_DIM ** -0.5)).T
        for g in range(n_groups):
            lanes = slice(g * GROUP_WIDTH, (g + 1) * GROUP_WIDTH)
            for hl in range(HEADS_PER_GROUP):
                qm = jnp.where(head_of_row == hl, qt[lanes, :], 0.0)
                qm_ref[g, :, hl * bq:(hl + 1) * bq] = qm.astype(BF16)
                gate = _dot3(km_ref[:, lanes], qm)
                sel_ref[g, :, hl * bq:(hl + 1) * bq] = _select_bias(gate, i.astype(F32), 0)

    for own_at in range(KEY_BLOCKS_PER_STEP):
        @pl.when((pair == own_pair) & (i % KEY_BLOCKS_PER_STEP == own_at))
        def _own_run(own_at=own_at):
            sweep(True, own_at)
            for half in range(own_at):
                sweep(False, half)

    @pl.when(pair != own_pair)
    def _past():
        for half in range(KEY_BLOCKS_PER_STEP):
            sweep(False, half)

    @pl.when((pair == own_pair - 1) | (own_pair == 0))
    def _last():
        outs = []
        for g in range(n_groups):
            inv = jnp.concatenate(
                [jnp.broadcast_to(l_ref[g:g + 1, hl * bq:(hl + 1) * bq], (HEAD_DIM, bq)) for hl in range(HEADS_PER_GROUP)],
                axis=0)
            outs.append(acc_ref[g] / inv)
        o_ref[...] = jnp.concatenate(outs, axis=0).T


def _moba_prompt(q, kb, vt, kmean, kp):
    t = q.shape[0]
    nb = t // MOBA_BLOCK
    per = KEY_BLOCKS_PER_STEP
    assert nb % per == 0
    i_of = np.concatenate([np.full(i // per + 1, i, np.int32) for i in range(nb)])
    p_of = np.concatenate([np.concatenate([[i // per], np.arange(i // per)]).astype(np.int32) for i in range(nb)])
    n_groups = N_ATT_HEADS // HEADS_PER_GROUP
    wide = HEADS_PER_GROUP * MOBA_BLOCK
    grid_spec = pltpu.PrefetchScalarGridSpec(
        num_scalar_prefetch=2,
        grid=(len(i_of),),
        in_specs=[pl.BlockSpec((MOBA_BLOCK, ATT_WIDTH), lambda s, io, po: (io[s], 0)),
                  pl.BlockSpec((per * MOBA_BLOCK, ATT_WIDTH), lambda s, io, po: (po[s], 0)),
                  pl.BlockSpec((ATT_WIDTH, per * MOBA_BLOCK), lambda s, io, po: (0, po[s])),
                  pl.BlockSpec((nb, ATT_WIDTH), lambda s, io, jo: (0, 0)),
                  pl.BlockSpec((n_groups, MOBA_BLOCK, wide), lambda s, io, jo: (0, 0, 0))],
        out_specs=pl.BlockSpec((MOBA_BLOCK, ATT_WIDTH), lambda s, io, jo: (io[s], 0)),
        scratch_shapes=[pltpu.VMEM((n_groups, GROUP_WIDTH, wide), BF16),
                        pltpu.VMEM((n_groups, nb, wide), F32),
                        pltpu.VMEM((n_groups, wide), F32),
                        pltpu.VMEM((n_groups, wide), F32),
                        pltpu.VMEM((n_groups, GROUP_WIDTH, MOBA_BLOCK), F32),
                        pltpu.VMEM((n_groups, MOBA_BLOCK, wide), F32)])
    return pl.pallas_call(
        _moba_body,
        grid_spec=grid_spec,
        out_shape=jax.ShapeDtypeStruct((t, ATT_WIDTH), F32),
        compiler_params=_params("arbitrary"),
        name="moba_prompt",
    )(jnp.asarray(i_of), jnp.asarray(p_of), q, kb, vt, kmean, kp)


def _decode_body(pt_ref, q_ref, kn_ref, vn_ref, *refs, pages, n_blocks, past_len, n_q):
    k_pages = refs[:pages]
    v_pages = refs[pages:2 * pages]
    o_ref = refs[2 * pages]
    gate_ref, op_ref, m_ref, l_ref = refs[2 * pages + 1:]
    step = pl.program_id(1)
    rows = N_ATT_HEADS * n_q
    head_of_row = lax.broadcasted_iota(jnp.int32, (rows, ATT_WIDTH), 0) % N_ATT_HEADS
    head_of_lane = lax.broadcasted_iota(jnp.int32, (rows, ATT_WIDTH), 1) // HEAD_DIM
    own_head = head_of_row == head_of_lane
    rid = lax.broadcasted_iota(jnp.int32, (rows, 1), 0)
    slope = jnp.zeros((rows, 1), F32)
    for h in range(N_ATT_HEADS):
        slope = jnp.where(rid % N_ATT_HEADS == h, _SLOPES[h], slope)
    qidx = rid // N_ATT_HEADS
    qpos = (past_len + qidx).astype(F32)
    blk_lane = lax.broadcasted_iota(jnp.int32, (rows, 128), 1)

    @pl.when(step == 0)
    def _init():
        m_ref[...] = jnp.full(m_ref.shape, MASK_VALUE, F32)
        l_ref[...] = jnp.zeros(l_ref.shape, F32)
        gate_ref[...] = jnp.zeros(gate_ref.shape, F32)

    qm_bf = jnp.where(own_head, q_ref[0] * (HEAD_DIM ** -0.5), 0.0).astype(BF16)
    blocks = range(pages // 2)
    blk = [step * (pages // 2) + b for b in blocks]
    qk = [_dot(qm_bf, jnp.concatenate([k_pages[2 * b][0], k_pages[2 * b + 1][0]], axis=1).astype(BF16))
          for b in blocks]
    gsum = [jnp.sum(qk[b], axis=-1, keepdims=True) * (1.0 / MOBA_BLOCK) for b in blocks]
    key = lax.broadcasted_iota(jnp.int32, (1, MOBA_BLOCK), 1)
    s = [qk[b] - slope * (qpos - (blk[b] * MOBA_BLOCK + key).astype(F32)) for b in blocks]
    m = [jnp.max(s[b], axis=-1, keepdims=True) for b in blocks]
    p = [jnp.exp(s[b] - m[b]) for b in blocks]
    l = [jnp.sum(p[b], axis=-1, keepdims=True) for b in blocks]
    gate, m_all, l_all = gate_ref[...], m_ref[...], l_ref[...]
    for b in blocks:
        here = blk_lane == blk[b]
        gate = jnp.where(here, gsum[b], gate)
        m_all = jnp.where(here, m[b], m_all)
        l_all = jnp.where(here, l[b], l_all)
    gate_ref[...], m_ref[...], l_ref[...] = gate, m_all, l_all
    for b in blocks:
        vt = jnp.concatenate([v_pages[2 * b][0], v_pages[2 * b + 1][0]], axis=1).astype(BF16)
        op_ref[blk[b]] = jnp.where(own_head, _dot_nt(p[b].astype(BF16), vt), 0.0)

    @pl.when(step == pl.num_programs(1) - 1)
    def _merge():
        sel = _select_bias(gate_ref[...], float(n_blocks), 1)
        kn = kn_ref[0]
        vn = vn_ref[0]
        kidx = lax.broadcasted_iota(jnp.int32, (rows, kn.shape[0]), 1)
        dist = (qidx - kidx).astype(F32)
        s_own = _dot_nt(qm_bf, kn.astype(BF16)) - slope * dist
        s_own = jnp.where(dist >= 0, s_own, MASK_VALUE)
        m_blk = m_ref[...] + sel
        m_all = jnp.maximum(jnp.max(m_blk, axis=-1, keepdims=True), jnp.max(s_own, axis=-1, keepdims=True))
        w = jnp.exp(m_blk - m_all)
        p_own = jnp.exp(s_own - m_all)
        den = jnp.sum(w * l_ref[...], axis=-1, keepdims=True) + jnp.sum(p_own, axis=-1, keepdims=True)
        acc = jnp.where(own_head, _dot(p_own.astype(BF16), vn.astype(BF16)), 0.0)
        for b in range(n_blocks):
            acc = acc + w[:, b:b + 1] * op_ref[b]
        acc = acc / den
        for qi in range(n_q):
            o_ref[0, qi:qi + 1, :] = jnp.sum(acc[qi * N_ATT_HEADS:(qi + 1) * N_ATT_HEADS], axis=0, keepdims=True)


def _moba_decode(q, k_new, v_new, cache_k, cache_v, page_table):
    b, n_q, _ = q.shape
    n_pages = page_table.shape[1]
    past_len = n_pages * PAGE_SIZE
    n_blocks = past_len // MOBA_BLOCK
    assert n_blocks <= 128 and n_pages % 2 == 0
    pages = next(p for p in (16, 8, 4, 2) if n_pages % p == 0)
    rows = N_ATT_HEADS * n_q
    q_rows = jnp.repeat(q, N_ATT_HEADS, axis=1)
    pad = ((0, 0), (0, OWN_ROWS - n_q), (0, 0))
    by_seq = lambda r: pl.BlockSpec((1, r, ATT_WIDTH), lambda bi, st, pt: (bi, 0, 0))

    def page_spec(t):
        return pl.BlockSpec((1, ATT_WIDTH, PAGE_SIZE), lambda bi, st, pt: (pt[bi, st * pages + t], 0, 0))

    grid_spec = pltpu.PrefetchScalarGridSpec(
        num_scalar_prefetch=1,
        grid=(b, n_pages // pages),
        in_specs=[by_seq(rows), by_seq(OWN_ROWS), by_seq(OWN_ROWS)] + [page_spec(t) for t in range(pages)] * 2,
        out_specs=by_seq(n_q),
        scratch_shapes=[pltpu.VMEM((rows, 128), F32),
                        pltpu.VMEM((n_blocks, rows, ATT_WIDTH), F32),
                        pltpu.VMEM((rows, 128), F32),
                        pltpu.VMEM((rows, 128), F32)])
    return pl.pallas_call(
        functools.partial(_decode_body, pages=pages, n_blocks=n_blocks, past_len=past_len, n_q=n_q),
        grid_spec=grid_spec,
        out_shape=jax.ShapeDtypeStruct((b, n_q, ATT_WIDTH), F32),
        compiler_params=_params("arbitrary", "arbitrary"),
        name="moba_decode",
    )(page_table, q_rows, jnp.pad(k_new, pad), jnp.pad(v_new, pad), *([cache_k] * pages), *([cache_v] * pages))


def _mix_body(attn_ref, ug_ref, vsn_ref, x_ref, wsg_ref, bsg_ref, ga_ref, gs_ref, wo_ref, gt_ref, sc_ref, sh_ref,
              gf_ref, wr_ref, br_ref, cin_ref, h2_all_ref, x1_ref, h2_ref, route_ref, cnt_ref, carry_ref):
    del h2_all_ref
    tm = x_ref.shape[0]

    @pl.when(pl.program_id(0) == 0)
    def _():
        carry_ref[...] = cin_ref[...]

    r = lax.broadcasted_iota(jnp.int32, (CHUNK, CHUNK), 0)
    c = lax.broadcasted_iota(jnp.int32, (CHUNK, CHUNK), 1)
    wcat = jnp.concatenate([jnp.where(c <= r, wsg_ref[g], 0.0).astype(BF16) for g in range(N_SGU_GROUPS)], axis=1)
    lane_group = lax.broadcasted_iota(jnp.int32, (CHUNK, SGU_WIDTH), 1) // HEAD_DIM
    mixed = []
    for ch in range(tm // CHUNK):
        vs = vsn_ref[ch * CHUNK:(ch + 1) * CHUNK, :]
        stack = jnp.concatenate([jnp.where(lane_group == g, vs, 0.0).astype(BF16) for g in range(N_SGU_GROUPS)],
                                axis=0)
        mixed.append(_dot(wcat, stack) + bsg_ref[...])
    sgu = ug_ref[...] * (mixed[0] if len(mixed) == 1 else jnp.concatenate(mixed, axis=0))
    merged = jnp.concatenate([_rms(attn_ref[...], ga_ref[...]), _rms(sgu, gs_ref[...])], axis=1)
    x1 = x_ref[...] + gt_ref[...] * _dot(merged.astype(BF16), wo_ref[...])
    x1_ref[...] = x1
    h2 = _rms(x1, gf_ref[...]) * (1.0 + sc_ref[...]) + sh_ref[...]
    _store_token_tiles(h2_ref, h2)

    logits = _dot3(h2, wr_ref[...]) + br_ref[...]
    lane_f = lax.broadcasted_iota(jnp.int32, (tm, N_EXPERTS), 1).astype(F32)
    vals, hits = [], []
    work = logits
    for _ in range(TOP_K):
        mx = jnp.max(work, axis=-1, keepdims=True)
        first = jnp.min(jnp.where(work == mx, lane_f, float(N_EXPERTS)), axis=-1, keepdims=True)
        hit = lane_f == first
        vals.append(mx)
        hits.append(hit)
        work = jnp.where(hit, -jnp.inf, work)
    ex = [jnp.exp(v - vals[0]) for v in vals]
    den = ex[0] + ex[1] + ex[2] + ex[3]
    chosen = jnp.zeros((tm, N_EXPERTS), F32)
    for hit in hits:
        chosen = jnp.where(hit, 1.0, chosen)
    rr = lax.broadcasted_iota(jnp.int32, (tm, tm), 0)
    cc = lax.broadcasted_iota(jnp.int32, (tm, tm), 1)
    before = _dot(jnp.where(cc < rr, 1.0, 0.0).astype(BF16), chosen.astype(BF16)) + carry_ref[...]
    rl = lax.broadcasted_iota(jnp.int32, (tm, ROUTE_LANES), 1)
    route = jnp.zeros((tm, ROUTE_LANES), F32)
    for t in range(TOP_K):
        idx = jnp.sum(jnp.where(hits[t], lane_f, 0.0), axis=-1, keepdims=True)
        rank = jnp.sum(jnp.where(hits[t], before, 0.0), axis=-1, keepdims=True)
        route = jnp.where(rl == t, idx, route)
        route = jnp.where(rl == TOP_K + t, ex[t] / den, route)
        route = jnp.where(rl == 2 * TOP_K + t, rank, route)
    route_ref[...] = route
    carry_ref[...] = carry_ref[...] + jnp.sum(chosen, axis=0, keepdims=True)
    cnt_ref[...] = carry_ref[...]


def _mix(attn, ug, vsn, x, wsg, bsg, g_att, g_sgu, w_out_bf, gt, sc, sh, g_ffn, w_router, b_router, cnt_in, tm,
         h2_all, first_tok):
    n = x.shape[0]
    assert first_tok % tm == 0
    first_blk = first_tok // tm
    mod_rows = gt.shape[0]
    mod_spec = (pl.BlockSpec((1, D_MODEL), lambda i: (0, 0)) if mod_rows == 1
                else pl.BlockSpec((tm, D_MODEL), lambda i: (i, 0)))
    row = lambda w: pl.BlockSpec((tm, w), lambda i: (i, 0))
    full = lambda *shape: pl.BlockSpec(shape, lambda i: (0,) * len(shape))
    return pl.pallas_call(
        _mix_body,
        grid=(n // tm,),
        in_specs=[row(ATT_WIDTH), row(SGU_WIDTH), row(SGU_WIDTH), row(D_MODEL),
                  full(N_SGU_GROUPS, CHUNK, CHUNK), full(CHUNK, SGU_WIDTH),
                  full(1, ATT_WIDTH), full(1, SGU_WIDTH), full(D_MODEL, D_MODEL),
                  mod_spec, mod_spec, mod_spec,
                  full(1, D_MODEL), full(D_MODEL, N_EXPERTS), full(1, N_EXPERTS), full(1, N_EXPERTS),
                  pl.BlockSpec(memory_space=pl.ANY)],
        out_specs=[row(D_MODEL), pl.BlockSpec((tm * ROW_TILE, 128), lambda i: (first_blk + i, 0)), row(ROUTE_LANES),
                   full(1, N_EXPERTS)],
        out_shape=[jax.ShapeDtypeStruct((n, D_MODEL), F32), jax.ShapeDtypeStruct(h2_all.shape, F32),
                   jax.ShapeDtypeStruct((n, ROUTE_LANES), F32), jax.ShapeDtypeStruct((1, N_EXPERTS), F32)],
        scratch_shapes=[pltpu.VMEM((1, N_EXPERTS), F32)],
        input_output_aliases={16: 1},
        compiler_params=_params("arbitrary"),
        name="mix",
    )(attn, ug, vsn, x, wsg, bsg, g_att.reshape(1, ATT_WIDTH), g_sgu.reshape(1, SGU_WIDTH), w_out_bf,
      gt, sc, sh, g_ffn.reshape(1, D_MODEL), w_router, b_router.reshape(1, N_EXPERTS), cnt_in, h2_all)


def _experts_body(tile_e, tile_rows, n_used, src_ref, src_next_ref, dst_ref, h2_ref, wgu_ref, bgu_ref,
                  wd_ref, bd_ref, y_ref, xbuf, obuf, gsem, ssem, wgu_bf, wd_bf):
    t = pl.program_id(0)
    used = n_used[0]
    slot = t % 2

    def for_each_row(rows, issue):
        def body(group, carry):
            for thread in range(DMA_THREADS):
                r = group * DMA_THREADS + thread
                pl.when(r < rows)(functools.partial(issue, r, thread))
            return carry
        lax.fori_loop(0, MOE_TILE // DMA_THREADS, body, 0, unroll=DMA_UNROLL // DMA_THREADS)

    def start_gather(src, rows, into):
        def issue(r, thread):
            pltpu.make_async_copy(_tile_at(h2_ref, src[r]), _tile_of(xbuf.at[into], r),
                                  gsem.at[into]).start(priority=thread)
        for_each_row(rows, issue)

    def wait_gather(rows, into):
        pltpu.make_async_copy(_tiles_of(h2_ref, rows), _tiles_of(y_ref, rows), gsem.at[into]).wait()

    def start_scatter(rows, frm):
        def issue(r, thread):
            pltpu.make_async_copy(_tile_of(obuf.at[frm], r), _tile_at(y_ref, dst_ref[r]),
                                  ssem.at[frm]).start(priority=thread)
        for_each_row(rows, issue)

    def wait_scatter(rows, frm):
        pltpu.make_async_copy(_tiles_of(h2_ref, rows), _tiles_of(y_ref, rows), ssem.at[frm]).wait()

    @pl.when(t == 0)
    def _():
        xbuf[...] = jnp.zeros(xbuf.shape, F32)
        start_gather(src_ref, tile_rows[0], 0)

    @pl.when(t + 1 < used)
    def _():
        start_gather(src_next_ref, tile_rows[t + 1], 1 - slot)

    e = tile_e[t]
    prev = tile_e[jnp.maximum(t - 1, 0)]

    @pl.when((t == 0) | (e != prev))
    def _():
        wgu_bf[...] = wgu_ref[0].astype(BF16)
        wd_bf[...] = wd_ref[0].astype(BF16)

    @pl.when(t < used)
    def _():
        wait_gather(tile_rows[t], slot)

        @pl.when(t >= 2)
        def _():
            wait_scatter(tile_rows[jnp.maximum(t - 2, 0)], slot)

        x = _load_token_tiles(xbuf.at[slot], MOE_TILE)
        gu = _dot(x.astype(BF16), wgu_bf[...]) + bgu_ref[0]
        g = jnp.minimum(gu[:, :D_FF], SWIGLU_LIMIT)
        lin = jnp.clip(gu[:, D_FF:], -SWIGLU_LIMIT, SWIGLU_LIMIT)
        act = g * (1.0 / (1.0 + jnp.exp(-SWIGLU_ALPHA * g))) * (lin + 1.0)
        out = _dot(act.astype(BF16), wd_bf[...]) + bd_ref[0]
        _store_token_tiles(obuf.at[slot], out)
        start_scatter(tile_rows[t], slot)

        @pl.when(t == used - 1)
        def _():
            @pl.when(t >= 1)
            def _():
                wait_scatter(tile_rows[jnp.maximum(t - 1, 0)], 1 - slot)
            wait_scatter(tile_rows[t], slot)


def _experts(h2, slot_src, slot_dst, tile_e, tile_rows, n_used, w_gu, b_gu, w_down, b_down):
    n_tok = h2.shape[0] // ROW_TILE
    n_tiles = slot_src.shape[0] // MOE_TILE
    by_e = lambda *blk: pl.BlockSpec((1,) + blk, lambda t, te, tr, nu: (te[t],) + (0,) * len(blk))
    hbm = pl.BlockSpec(memory_space=pl.ANY)
    this_tile = pl.BlockSpec((MOE_TILE,), lambda t, *_: (t,), memory_space=pltpu.SMEM)
    grid_spec = pltpu.PrefetchScalarGridSpec(
        num_scalar_prefetch=3,
        grid=(n_tiles,),
        in_specs=[this_tile,
                  pl.BlockSpec((MOE_TILE,), lambda t, *_: (jnp.minimum(t + 1, n_tiles - 1),), memory_space=pltpu.SMEM),
                  this_tile,
                  hbm,
                  by_e(D_MODEL, 2 * D_FF), by_e(1, 2 * D_FF), by_e(D_FF, D_MODEL), by_e(1, D_MODEL)],
        out_specs=hbm,
        scratch_shapes=[pltpu.VMEM((2, MOE_TILE * ROW_TILE, 128), F32), pltpu.VMEM((2, MOE_TILE * ROW_TILE, 128), F32),
                        pltpu.SemaphoreType.DMA((2,)), pltpu.SemaphoreType.DMA((2,)),
                        pltpu.VMEM((D_MODEL, 2 * D_FF), BF16), pltpu.VMEM((D_FF, D_MODEL), BF16)])
    return pl.pallas_call(
        _experts_body,
        grid_spec=grid_spec,
        out_shape=jax.ShapeDtypeStruct((TOP_K * n_tok * ROW_TILE, 128), F32),
        compiler_params=_params("arbitrary"),
        name="experts",
    )(tile_e, tile_rows, n_used, slot_src, slot_src, slot_dst, h2, w_gu,
      b_gu.reshape(N_EXPERTS, 1, 2 * D_FF), w_down, b_down.reshape(N_EXPERTS, 1, D_MODEL))


def _combine_body(*refs):
    y_refs = refs[:TOP_K]
    x1_ref, route_ref, gt_ref, gfin_ref, o_ref = refs[TOP_K:]
    tc = x1_ref.shape[0]
    route = route_ref[...]
    moe = route[:, TOP_K:TOP_K + 1] * _load_token_tiles(y_refs[0], tc)
    for t in range(1, TOP_K):
        moe = moe + route[:, TOP_K + t:TOP_K + t + 1] * _load_token_tiles(y_refs[t], tc)
    o_ref[...] = _rms(x1_ref[...] + gt_ref[...] * moe, gfin_ref[...])


def _combine(ys, first_tok, x1, route, gt, g_final):
    n = x1.shape[0]
    tc = COMBINE_TOKENS
    n_tok = ys.shape[0] // (TOP_K * ROW_TILE)
    assert n_tok % tc == 0 and first_tok % tc == 0
    mod_spec = (pl.BlockSpec((1, D_MODEL), lambda i: (0, 0)) if gt.shape[0] == 1
                else pl.BlockSpec((tc, D_MODEL), lambda i: (i, 0)))

    def y_spec(k):
        return pl.BlockSpec((tc * ROW_TILE, 128), lambda i: ((k * n_tok + first_tok) // tc + i, 0))

    return pl.pallas_call(
        _combine_body,
        grid=(n // tc,),
        in_specs=[y_spec(k) for k in range(TOP_K)] + [
            pl.BlockSpec((tc, D_MODEL), lambda i: (i, 0)),
            pl.BlockSpec((tc, ROUTE_LANES), lambda i: (i, 0)),
            mod_spec,
            pl.BlockSpec((1, D_MODEL), lambda i: (0, 0))],
        out_specs=pl.BlockSpec((tc, D_MODEL), lambda i: (i, 0)),
        out_shape=jax.ShapeDtypeStruct((n, D_MODEL), F32),
        compiler_params=_params("arbitrary"),
        name="combine",
    )(*([ys] * TOP_K), x1, route, gt, g_final.reshape(1, D_MODEL))


def _pages_keys_minor(cache):
    return cache.transpose(0, 2, 3, 1).reshape(cache.shape[0], ATT_WIDTH, PAGE_SIZE)


def _group_mean_matrix():
    g = np.arange(SGU_WIDTH) // HEAD_DIM
    return jnp.asarray((g[:, None] == g[None, :]).astype(np.float32) / HEAD_DIM, BF16)


def _key_offset_bias():
    off = np.arange(MOBA_BLOCK, dtype=np.float32)[None, :, None]
    per_head = np.broadcast_to(np.asarray(_SLOPES, np.float32)[:, None, None] * off,
                               (N_ATT_HEADS, MOBA_BLOCK, MOBA_BLOCK))
    n_groups = N_ATT_HEADS // HEADS_PER_GROUP
    grouped = per_head.reshape(n_groups, HEADS_PER_GROUP, MOBA_BLOCK, MOBA_BLOCK).transpose(0, 2, 1, 3)
    return jnp.asarray(grouped.reshape(n_groups, MOBA_BLOCK, HEADS_PER_GROUP * MOBA_BLOCK))


def kernel(x_prompt, x_sample, cache_k, cache_v, page_table, c_prompt, c_sample, g_mix, w_ada, b_ada, w_in,
           g_sgu, w_sgu, b_sgu, g_out_att, g_out_sgu, w_out, g_ffn, w_router, b_router, w_gu, b_gu, w_down,
           b_down, g_final):
    depth = w_in.shape[0]
    batch, seq, _ = x_prompt.shape
    dec_batch, dec_seq, _ = x_sample.shape
    assert depth == 1, "the final norm is fused into the last stage of the single layer"
    assert batch == 1 and seq % MOBA_BLOCK == 0
    n_dec = dec_batch * dec_seq
    assert n_dec % CHUNK == 0 and CHUNK % dec_seq == 0 and dec_seq <= MOBA_BLOCK
    n_tok = seq + n_dec
    n_pool = cache_k.shape[1]

    bd = _group_mean_matrix()
    kp = _key_offset_bias()
    xp = x_prompt.reshape(seq, D_MODEL)
    xs = x_sample.reshape(n_dec, D_MODEL)
    c_all = jnp.concatenate([c_prompt, c_sample], axis=0)
    c_rows = -(-c_all.shape[0] // 8) * 8
    c_all = jnp.pad(c_all, ((0, c_rows - c_all.shape[0]), (0, 0)))
    k_rows, v_rows, ks_rows, vs_rows, sgu_rows = [], [], [], [], []

    for l in range(depth):
        mod = _ada(c_all, w_ada[l], b_ada[l])
        mod_p = [mod[0:1, t * D_MODEL:(t + 1) * D_MODEL] for t in range(6)]
        mod_rows = jnp.broadcast_to(mod[1:1 + dec_batch, None, :], (dec_batch, dec_seq, 6 * D_MODEL))
        mod_rows = mod_rows.reshape(n_dec, 6 * D_MODEL)
        mod_s = [mod_rows[:, t * D_MODEL:(t + 1) * D_MODEL] for t in range(6)]
        w_in_bf = w_in[l].astype(BF16)
        w_out_bf = w_out[l].astype(BF16)
        gs_flat = g_sgu[l].reshape(SGU_WIDTH)

        q, k, v, ug, vsn, kb, vt, kmean = _inproj(xp, g_mix[l], mod_p[1], mod_p[0], w_in_bf, gs_flat, bd, True)
        attn = _moba_prompt(q, kb, vt, kmean.reshape(seq // MOBA_BLOCK, ATT_WIDTH), kp)
        bias_p = jnp.repeat(b_sgu[l].T, HEAD_DIM, axis=1)
        h2 = jnp.zeros((n_tok * ROW_TILE, 128), F32)
        x1_p, h2, route_p, cnt_p = _mix(attn, ug, vsn, xp, w_sgu[l], bias_p, g_out_att[l], g_out_sgu[l],
                                        w_out_bf, mod_p[2], mod_p[4], mod_p[3], g_ffn[l], w_router[l],
                                        b_router[l], jnp.zeros((1, N_EXPERTS), F32), MOBA_BLOCK, h2, 0)
        k_rows.append(k)
        v_rows.append(v)

        qs, ks, vs, ugs, vsns = _inproj(xs, g_mix[l], mod_s[1], mod_s[0], w_in_bf, gs_flat, bd, False)
        attn_s = _moba_decode(qs.reshape(dec_batch, dec_seq, ATT_WIDTH), ks.reshape(dec_batch, dec_seq, ATT_WIDTH),
                              vs.reshape(dec_batch, dec_seq, ATT_WIDTH),
                              _pages_keys_minor(cache_k[l]), _pages_keys_minor(cache_v[l]), page_table)
        eye = jnp.eye(CHUNK // dec_seq, dtype=F32)
        w_s = jax.vmap(lambda w: jnp.kron(eye, w))(w_sgu[l][:, :dec_seq, :dec_seq])
        bias_s = jnp.tile(jnp.repeat(b_sgu[l].T[:dec_seq], HEAD_DIM, axis=1), (CHUNK // dec_seq, 1))
        x1_s, h2, route_s, cnt = _mix(attn_s.reshape(n_dec, ATT_WIDTH), ugs, vsns, xs, w_s, bias_s, g_out_att[l],
                                      g_out_sgu[l], w_out_bf, mod_s[2], mod_s[4], mod_s[3], g_ffn[l],
                                      w_router[l], b_router[l], cnt_p, CHUNK, h2, seq)
        ks_rows.append(ks)
        vs_rows.append(vs)
        sgu_rows.append(vsns)

        route = jnp.concatenate([route_p, route_s], axis=0)
        e_idx = route[:, :TOP_K].astype(jnp.int32)
        rank = route[:, 2 * TOP_K:3 * TOP_K].astype(jnp.int32)
        cnt_i = cnt.reshape(N_EXPERTS).astype(jnp.int32)
        pcnt = (cnt_i + MOE_TILE - 1) // MOE_TILE * MOE_TILE
        pend = jnp.cumsum(pcnt)
        pstart = pend - pcnt
        dest = (pstart[e_idx] + rank).reshape(-1)
        n_tiles = -(-(n_tok * TOP_K) // MOE_TILE) + N_EXPERTS
        tile_start = jnp.arange(n_tiles, dtype=jnp.int32) * MOE_TILE
        tile_e = jnp.minimum(jnp.sum((pend[None, :] <= tile_start[:, None]).astype(jnp.int32), axis=1), N_EXPERTS - 1)
        n_used = (pend[-1:] // MOE_TILE).astype(jnp.int32)
        tile_rows = jnp.clip((pstart + cnt_i)[tile_e] - tile_start, 0, MOE_TILE).astype(jnp.int32)
        slot_assign = jnp.zeros((n_tiles * MOE_TILE,), jnp.int32).at[dest].set(
            jnp.arange(n_tok * TOP_K, dtype=jnp.int32), unique_indices=True)
        slot_tok = slot_assign // TOP_K
        slot_src = slot_tok * ROW_TILE
        slot_dst = ((slot_assign % TOP_K) * n_tok + slot_tok) * ROW_TILE

        ys = _experts(h2, slot_src, slot_dst, tile_e, tile_rows, n_used, w_gu[l], b_gu[l], w_down[l], b_down[l])
        xp = _combine(ys, 0, x1_p, route_p, mod_p[5], g_final)
        xs = _combine(ys, seq, x1_s, route_s, mod_s[5], g_final)

    hd = (N_ATT_HEADS, HEAD_DIM)
    def paged(rows):
        t = jnp.stack(rows).reshape(depth, batch, seq // PAGE_SIZE, *hd, PAGE_SIZE)
        return t.transpose(0, 1, 2, 5, 3, 4)

    k_p = paged(k_rows)
    v_p = paged(v_rows)
    k_s = jnp.stack(ks_rows).reshape(depth, dec_batch, dec_seq, *hd)
    v_s = jnp.stack(vs_rows).reshape(depth, dec_batch, dec_seq, *hd)
    sgu_v = jnp.stack(sgu_rows).reshape(depth, dec_batch, dec_seq, N_SGU_GROUPS, HEAD_DIM)
    return (xp.reshape(batch, seq, D_MODEL), xs.reshape(dec_batch, dec_seq, D_MODEL), k_p, v_p, k_s, v_s, sgu_v)
```
